```python
import jax, jax.numpy as jnp
from jax import lax
import numpy as np

D_MODEL = 4096
BATCH = 2
SEQ = 4096
DEPTH = 2

RET_HEAD_DIM = 256
RET_WIDTH = D_MODEL // 2
RET_HEADS = RET_WIDTH // RET_HEAD_DIM
RET_CHUNK = 128
HGRN_HEAD_DIM = 128
HGRN_WIDTH = D_MODEL // 4
HGRN_HEADS = HGRN_WIDTH // HGRN_HEAD_DIM
HGRN_CHUNK = 64
FOURIER_GROUP_DIM = 128
FOURIER_WIDTH = D_MODEL - RET_WIDTH - HGRN_WIDTH
FOURIER_GROUPS = FOURIER_WIDTH // FOURIER_GROUP_DIM
MIX_WIDTH = RET_WIDTH + HGRN_WIDTH + FOURIER_WIDTH
IN_PROJ_WIDTH = 4 * RET_WIDTH + 5 * HGRN_WIDTH + FOURIER_WIDTH
XATTN_HEADS = 4
XATTN_HEAD_DIM = 256
XATTN_WIDTH = XATTN_HEADS * XATTN_HEAD_DIM
MEM_TOKENS = 256
FFN_HIDDEN = ((8 * D_MODEL // 3 + 255) // 256) * 256
ROPE_BASE = 10000.0
RMS_EPS = 1e-6
GN_EPS = 1e-5

kernel_name = "hybrid_retention_hgrn2_fnet_encoder"


def _rms_norm(x, g, eps=RMS_EPS):
    xf = x.astype(jnp.float32)
    y = xf * lax.rsqrt(jnp.mean(xf * xf, axis=-1, keepdims=True) + eps)
    return (y * g.astype(jnp.float32)).astype(x.dtype)


def _to_heads(t, n_heads):
    b, s, w = t.shape
    return t.reshape(b, s, n_heads, w // n_heads).transpose(0, 2, 1, 3)


def _from_heads(t):
    b, h, s, d = t.shape
    return t.transpose(0, 2, 1, 3).reshape(b, s, h * d)


def _to_chunks(t, c):
    b, h, s, d = t.shape
    return jnp.moveaxis(t.reshape(b, h, s // c, c, d), 2, 0)


def _from_chunks(o):
    n, b, h, c, d = o.shape
    return jnp.moveaxis(o, 0, 2).reshape(b, h, n * c, d)


def _rotary(t):
    s, d = t.shape[2], t.shape[3]
    half = d // 2
    inv = jnp.power(ROPE_BASE, -jnp.arange(half, dtype=jnp.float32) / half)
    ang = jnp.arange(s, dtype=jnp.float32)[:, None] * inv[None, :]
    cos, sin = jnp.cos(ang), jnp.sin(ang)
    t1, t2 = t[..., :half], t[..., half:]
    return jnp.concatenate([t1 * cos - t2 * sin, t1 * sin + t2 * cos], axis=-1)


def _retention_one_dir(q, k, v, gamma, inclusive):
    b, h, s, d = q.shape
    dv = v.shape[-1]
    c = RET_CHUNK
    log_g = jnp.log(gamma)
    idx = jnp.arange(c, dtype=jnp.float32)
    rel = idx[:, None] - idx[None, :]
    mask = (rel >= 0) if inclusive else (rel > 0)
    decay = jnp.where(mask[None], jnp.exp(jnp.where(mask, rel, 0.0)[None] * log_g[:, None, None]), 0.0)
    q_dec = jnp.exp((idx + 1.0)[None, :] * log_g[:, None])[None, :, :, None]
    k_dec = jnp.exp((c - 1.0 - idx)[None, :] * log_g[:, None])[None, :, :, None]
    chunk_dec = jnp.exp(c * log_g)[None, :, None, None]

    def step(state, inp):
        qc, kc, vc = inp
        scores = jnp.einsum('bhid,bhjd->bhij', qc, kc) * decay[None]
        o = (jnp.einsum('bhij,bhje->bhie', scores, vc)
             + jnp.einsum('bhid,bhde->bhie', qc * q_dec, state))
        state = chunk_dec * state + jnp.einsum('bhjd,bhje->bhde', kc * k_dec, vc)
        return state, o

    state0 = jnp.zeros((b, h, d, dv), jnp.float32)
    _, o = lax.scan(step, state0, (_to_chunks(q, c), _to_chunks(k, c), _to_chunks(v, c)))
    return _from_chunks(o)


def _retention_mixer(rq, rk, rv, rg, norm_g):
    f32 = jnp.float32
    q = _rotary(_to_heads(rq, RET_HEADS).astype(f32)) * (RET_HEAD_DIM ** -0.5)
    k = _rotary(_to_heads(rk, RET_HEADS).astype(f32))
    v = _to_heads(rv, RET_HEADS).astype(f32)
    gamma_fwd = 1.0 - jnp.power(2.0, -5.0 - jnp.arange(RET_HEADS, dtype=f32))
    gamma_bwd = gamma_fwd[::-1]
    o_fwd = _retention_one_dir(q, k, v, gamma_fwd, True)
    o_bwd = jnp.flip(_retention_one_dir(jnp.flip(q, 2), jnp.flip(k, 2), jnp.flip(v, 2), gamma_bwd, False), 2)
    o = o_fwd + o_bwd
    mu = jnp.mean(o, axis=-1, keepdims=True)
    var = jnp.mean(jnp.square(o - mu), axis=-1, keepdims=True)
    o = _from_heads((o - mu) * lax.rsqrt(var + GN_EPS)) * norm_g.astype(f32)
    return (jax.nn.silu(rg.astype(f32)) * o).astype(rq.dtype)


def _gla_one_dir(q, k, v, log_f):
    b, h, s, dk = q.shape
    dv = v.shape[-1]
    c = HGRN_CHUNK
    tri = jnp.tril(jnp.ones((c, c), dtype=bool))

    def step(state, inp):
        qc, kc, vc, gc = inp
        cum = jnp.cumsum(gc, axis=2)
        rel = cum[:, :, :, None, :] - cum[:, :, None, :, :]
        w = jnp.exp(jnp.where(tri[None, None, :, :, None], rel, -jnp.inf))
        scores = jnp.einsum('bhtd,bhsd,bhtsd->bhts', qc, kc, w)
        last = cum[:, :, -1, :]
        o = (jnp.einsum('bhts,bhse->bhte', scores, vc)
             + jnp.einsum('bhtd,bhde->bhte', qc * jnp.exp(cum), state))
        state = (jnp.exp(last)[..., None] * state
                 + jnp.einsum('bhsd,bhse->bhde', kc * jnp.exp(last[:, :, None, :] - cum), vc))
        return state, o

    state0 = jnp.zeros((b, h, dk, dv), jnp.float32)
    _, o = lax.scan(step, state0, (_to_chunks(q, c), _to_chunks(k, c), _to_chunks(v, c), _to_chunks(log_f, c)))
    return _from_chunks(o)


def _hgrn2_mixer(hq, hf_fwd, hf_bwd, hi, hg, lb, norm_g):
    f32 = jnp.float32
    q = _to_heads(hq, HGRN_HEADS).astype(f32) * (HGRN_HEAD_DIM ** -0.5)
    i = _to_heads(hi, HGRN_HEADS).astype(f32)

    def gates(z, lb_dir):
        lb_h = lb_dir.reshape(HGRN_HEADS, 1, HGRN_HEAD_DIM)
        f = lb_h + (1.0 - lb_h) * jax.nn.sigmoid(_to_heads(z, HGRN_HEADS).astype(f32))
        return 1.0 - f, jnp.log(f)

    k_f, logf_f = gates(hf_fwd, lb[0])
    k_b, logf_b = gates(hf_bwd, lb[1])
    o_fwd = _gla_one_dir(q, k_f, i, logf_f)
    o_bwd = jnp.flip(_gla_one_dir(jnp.flip(q, 2), jnp.flip(k_b, 2), jnp.flip(i, 2), jnp.flip(logf_b, 2)), 2)
    o = o_fwd + o_bwd
    o = o * lax.rsqrt(jnp.mean(o * o, axis=-1, keepdims=True) + RMS_EPS)
    o = _from_heads(o) * norm_g.astype(f32)
    return (jax.nn.silu(hg.astype(f32)) * o).astype(hq.dtype)


def _fourier_mixer(z, w, bias):
    b, s, _ = z.shape
    zg = z.astype(jnp.float32).reshape(b, s, FOURIER_GROUPS, FOURIER_GROUP_DIM)
    spec = jnp.real(jnp.fft.fft2(zg, axes=(1, 3), norm='ortho'))
    y = jnp.einsum('bsgc,gcd->bsgd', spec, w.astype(jnp.float32)) + bias.astype(jnp.float32)
    return y.reshape(b, s, FOURIER_WIDTH).astype(z.dtype)


def _cross_attention(c, mem_n, wq, wk, wv, wo):
    f32 = jnp.float32
    q = _to_heads(c @ wq, XATTN_HEADS).astype(f32) * (XATTN_HEAD_DIM ** -0.5)
    k = _to_heads(mem_n @ wk, XATTN_HEADS).astype(f32)
    v = _to_heads(mem_n @ wv, XATTN_HEADS).astype(f32)
    p = jax.nn.softmax(jnp.einsum('bhsd,bhmd->bhsm', q, k), axis=-1)
    o = jnp.einsum('bhsm,bhmd->bhsd', p, v)
    return _from_heads(o).astype(c.dtype) @ wo


def setup_inputs(seed: int = 0) -> dict:
    key = jax.random.key(seed)
    ks = jax.random.split(key, 24)
    L, D = DEPTH, D_MODEL
    f32 = jnp.float32

    def nrm(k, shape, scale):
        return jax.random.normal(k, shape, f32) * scale

    def gain(k, shape):
        return 1.0 + 0.05 * jax.random.normal(k, shape, f32)

    return {
        "x": jax.random.normal(ks[0], (BATCH, SEQ, D), f32),
        "mem": jax.random.normal(ks[1], (BATCH, MEM_TOKENS, D), f32),
        "mem_norm_g": gain(ks[2], (D,)),
        "pre_mix_g": gain(ks[3], (L, D)),
        "w_in": nrm(ks[4], (L, D, IN_PROJ_WIDTH), D ** -0.5),
        "ret_norm_g": gain(ks[5], (L, RET_WIDTH)),
        "hgrn_lb_logits": 1.0 + 0.1 * jax.random.normal(ks[6], (L, 2, HGRN_WIDTH), f32),
        "hgrn_norm_g": gain(ks[7], (L, HGRN_WIDTH)),
        "fourier_w": nrm(ks[8], (L, FOURIER_GROUPS, FOURIER_GROUP_DIM, FOURIER_GROUP_DIM), FOURIER_GROUP_DIM ** -0.5),
        "fourier_b": nrm(ks[9], (L, FOURIER_GROUPS, FOURIER_GROUP_DIM), 0.02),
        "w_out": nrm(ks[10], (L, MIX_WIDTH, D), MIX_WIDTH ** -0.5),
        "post_mix_g": gain(ks[11], (L, D)),
        "pre_xattn_g": gain(ks[12], (L, D)),
        "xattn_wq": nrm(ks[13], (L, D, XATTN_WIDTH), D ** -0.5),
        "xattn_wk": nrm(ks[14], (L, D, XATTN_WIDTH), D ** -0.5),
        "xattn_wv": nrm(ks[15], (L, D, XATTN_WIDTH), D ** -0.5),
        "xattn_wo": nrm(ks[16], (L, XATTN_WIDTH, D), XATTN_WIDTH ** -0.5),
        "post_xattn_g": gain(ks[17], (L, D)),
        "pre_ffn_g": gain(ks[18], (L, D)),
        "ffn_w_gate": nrm(ks[19], (L, D, FFN_HIDDEN), D ** -0.5),
        "ffn_w_up": nrm(ks[20], (L, D, FFN_HIDDEN), D ** -0.5),
        "ffn_w_down": nrm(ks[21], (L, FFN_HIDDEN, D), FFN_HIDDEN ** -0.5),
        "post_ffn_g": gain(ks[22], (L, D)),
    }


def reference(x, mem, mem_norm_g, pre_mix_g, w_in, ret_norm_g, hgrn_lb_logits, hgrn_norm_g,
              fourier_w, fourier_b, w_out, post_mix_g, pre_xattn_g, xattn_wq, xattn_wk, xattn_wv,
              xattn_wo, post_xattn_g, pre_ffn_g, ffn_w_gate, ffn_w_up, ffn_w_down, post_ffn_g):
    sizes = [RET_WIDTH] * 4 + [HGRN_WIDTH] * 5 + [FOURIER_WIDTH]
    splits = [int(v) for v in np.cumsum(sizes)[:-1]]
    p = jax.nn.softmax(hgrn_lb_logits.astype(jnp.float32), axis=0)
    lower_bounds = jnp.cumsum(p, axis=0) - p[0:1]
    mem_n = _rms_norm(mem, mem_norm_g)

    h = x
    for l in range(DEPTH):
        a = _rms_norm(h, pre_mix_g[l])
        proj = a @ w_in[l]
        rq, rk, rv, rg, hq, hff, hfb, hi, hg, fz = jnp.split(proj, splits, axis=-1)
        y_ret = _retention_mixer(rq, rk, rv, rg, ret_norm_g[l])
        y_hgrn = _hgrn2_mixer(hq, hff, hfb, hi, hg, lower_bounds[l], hgrn_norm_g[l])
        y_fft = _fourier_mixer(fz, fourier_w[l], fourier_b[l])
        mixed = jnp.concatenate([y_ret, y_hgrn, y_fft], axis=-1).astype(h.dtype) @ w_out[l]
        h = h + _rms_norm(mixed, post_mix_g[l])
        c = _rms_norm(h, pre_xattn_g[l])
        xa = _cross_attention(c, mem_n, xattn_wq[l], xattn_wk[l], xattn_wv[l], xattn_wo[l])
        h = h + _rms_norm(xa, post_xattn_g[l])
        f = _rms_norm(h, pre_ffn_g[l])
        ff = (jax.nn.silu(f @ ffn_w_gate[l]) * (f @ ffn_w_up[l])) @ ffn_w_down[l]
        h = h + _rms_norm(ff, post_ffn_g[l])
    return h
```

```python
import functools
import math

import numpy as np
import jax
import jax.numpy as jnp
from jax import lax
from jax.experimental import pallas as pl
from jax.experimental.pallas import tpu as pltpu

F32 = jnp.float32
BF16 = jnp.bfloat16
HIGHEST = lax.Precision.HIGHEST

RET_HEAD_DIM = 256
HGRN_HEAD_DIM = 128
FOURIER_GROUP_DIM = 128
XATTN_HEADS = 4
XATTN_HEAD_DIM = 256
ROPE_BASE = 10000.0
RMS_EPS = 1e-6
GN_EPS = 1e-5

LANES = 128
V7X_VMEM_BYTES = 64 * 1024 * 1024
VMEM_LIMIT = V7X_VMEM_BYTES - 8 * 1024 * 1024

RET_CHUNK = 256
HGRN_CHUNK = 64
HGRN_SUB = 16
HGRN_ROWS = 256
FFT_N1 = 64


def _cparams(sem):
    return pltpu.CompilerParams(dimension_semantics=sem, vmem_limit_bytes=VMEM_LIMIT)


def _tile(n, preferred):
    for t in range(min(preferred, n) // LANES * LANES, 0, -LANES):
        if n % t == 0:
            return t
    return n


def _dot(a, b):
    return jnp.dot(a, b, preferred_element_type=F32)


def _dot_nt(a, b):
    return lax.dot_general(a, b, (((1,), (1,)), ((), ())), preferred_element_type=F32)


def _dot_tn(a, b):
    return lax.dot_general(a, b, (((0,), (0,)), ((), ())), preferred_element_type=F32)


def _dot_exact(a, b):
    return jnp.dot(a, b, preferred_element_type=F32, precision=HIGHEST)


def _rms(x, g):
    ms = jnp.mean(x * x, axis=-1, keepdims=True)
    return x * lax.rsqrt(ms + RMS_EPS) * g


def _silu(x):
    return x * (1.0 / (1.0 + jnp.exp(-x)))


def _norm_kernel(x_ref, g_ref, o_ref):
    o_ref[...] = _rms(x_ref[...], g_ref[...]).astype(o_ref.dtype)


def rmsnorm_rows(x, g, br=256):
    m, d = x.shape
    out_dtype = BF16
    br = min(br, m)
    return pl.pallas_call(
        _norm_kernel,
        grid=(m // br,),
        in_specs=[pl.BlockSpec((br, d), lambda i: (i, 0)),
                  pl.BlockSpec((1, d), lambda i: (0, 0))],
        out_specs=pl.BlockSpec((br, d), lambda i: (i, 0)),
        out_shape=jax.ShapeDtypeStruct((m, d), out_dtype),
        compiler_params=_cparams(("parallel",)),
        name="rmsnorm_rows",
    )(x, g.reshape(1, d))


def _add_norm_kernel(h_ref, y_ref, gp_ref, gn_ref, ho_ref, a_ref):
    h = h_ref[...] + _rms(y_ref[...], gp_ref[...])
    ho_ref[...] = h
    a_ref[...] = _rms(h, gn_ref[...]).astype(a_ref.dtype)


def _add_norm_last_kernel(h_ref, y_ref, gp_ref, ho_ref):
    ho_ref[...] = h_ref[...] + _rms(y_ref[...], gp_ref[...])


def add_norm(h, y, g_post, g_next, br=256):
    m, d = h.shape
    row = pl.BlockSpec((br, d), lambda i: (i, 0))
    vec = pl.BlockSpec((1, d), lambda i: (0, 0))
    if g_next is None:
        return pl.pallas_call(
            _add_norm_last_kernel, grid=(m // br,),
            in_specs=[row, row, vec], out_specs=row,
            out_shape=jax.ShapeDtypeStruct((m, d), F32),
            compiler_params=_cparams(("parallel",)), name="add_norm_last",
        )(h, y, g_post.reshape(1, d)), None
    return pl.pallas_call(
        _add_norm_kernel, grid=(m // br,),
        in_specs=[row, row, vec, vec], out_specs=[row, row],
        out_shape=[jax.ShapeDtypeStruct((m, d), F32), jax.ShapeDtypeStruct((m, d), BF16)],
        compiler_params=_cparams(("parallel",)), name="add_norm",
    )(h, y, g_post.reshape(1, d), g_next.reshape(1, d))


def _mm_kernel(a_ref, w_ref, o_ref, acc_ref, *, nk):
    k = pl.program_id(2)

    @pl.when(k == 0)
    def _():
        acc_ref[...] = jnp.zeros_like(acc_ref)

    acc_ref[...] += _dot(a_ref[...], w_ref[...].astype(BF16))

    @pl.when(k == nk - 1)
    def _():
        o_ref[...] = acc_ref[...].astype(o_ref.dtype)


def matmul(a, w, layer, out_dtype, bm, bn, bk):
    m, kdim = a.shape
    n = w.shape[2]
    bm, bn, bk = _tile(m, bm), _tile(n, bn), _tile(kdim, bk)
    assert m % bm == 0 and n % bn == 0 and kdim % bk == 0
    nk = kdim // bk
    return pl.pallas_call(
        functools.partial(_mm_kernel, nk=nk),
        grid=(m // bm, n // bn, nk),
        in_specs=[pl.BlockSpec((bm, bk), lambda i, j, k: (i, k)),
                  pl.BlockSpec((None, bk, bn), lambda i, j, k: (layer, k, j))],
        out_specs=pl.BlockSpec((bm, bn), lambda i, j, k: (i, j)),
        out_shape=jax.ShapeDtypeStruct((m, n), out_dtype),
        scratch_shapes=[pltpu.VMEM((bm, bn), F32)],
        compiler_params=_cparams(("parallel", "parallel", "arbitrary")),
        name="matmul",
    )(a, w)


def _swiglu_kernel(a_ref, wg_ref, wu_ref, o_ref, accg_ref, accu_ref, *, nk):
    k = pl.program_id(2)

    @pl.when(k == 0)
    def _():
        accg_ref[...] = jnp.zeros_like(accg_ref)
        accu_ref[...] = jnp.zeros_like(accu_ref)

    a = a_ref[...]
    accg_ref[...] += _dot(a, wg_ref[...].astype(BF16))
    accu_ref[...] += _dot(a, wu_ref[...].astype(BF16))

    @pl.when(k == nk - 1)
    def _():
        o_ref[...] = (_silu(accg_ref[...]) * accu_ref[...]).astype(o_ref.dtype)


def swiglu_matmul(a, wg, wu, layer, bm, bn, bk, n_cols=None):
    m, kdim = a.shape
    n = wg.shape[2] if n_cols is None else n_cols
    bm, bn, bk = _tile(m, bm), _tile(n, bn), _tile(kdim, bk)
    assert m % bm == 0 and n % bn == 0 and kdim % bk == 0
    nk = kdim // bk
    wspec = pl.BlockSpec((None, bk, bn), lambda i, j, k: (layer, k, j))
    return pl.pallas_call(
        functools.partial(_swiglu_kernel, nk=nk),
        grid=(m // bm, n // bn, nk),
        in_specs=[pl.BlockSpec((bm, bk), lambda i, j, k: (i, k)), wspec, wspec],
        out_specs=pl.BlockSpec((bm, bn), lambda i, j, k: (i, j)),
        out_shape=jax.ShapeDtypeStruct((m, n), BF16),
        scratch_shapes=[pltpu.VMEM((bm, bn), F32), pltpu.VMEM((bm, bn), F32)],
        compiler_params=_cparams(("parallel", "parallel", "arbitrary")),
        name="swiglu_matmul",
    )(a, wg, wu)


def _mm2_kernel(a_ref, w_ref, at_ref, wt_ref, o_ref, acc_ref, *, nk):
    k = pl.program_id(2)

    @pl.when(k == 0)
    def _():
        acc_ref[...] = _dot(at_ref[...], wt_ref[...].astype(BF16))

    acc_ref[...] += _dot(a_ref[...], w_ref[...].astype(BF16))

    @pl.when(k == nk - 1)
    def _():
        o_ref[...] = acc_ref[...].astype(o_ref.dtype)


def matmul_split_k(a, w, layer, a_tail, w_tail, bm, bn, bk):
    m, kmain = a.shape
    ktail = a_tail.shape[1]
    n = w.shape[2]
    bm, bn, bk = _tile(m, bm), _tile(n, bn), _tile(kmain, bk)
    assert m % bm == 0 and n % bn == 0 and kmain % bk == 0
    nk = kmain // bk
    return pl.pallas_call(
        functools.partial(_mm2_kernel, nk=nk),
        grid=(m // bm, n // bn, nk),
        in_specs=[pl.BlockSpec((bm, bk), lambda i, j, k: (i, k)),
                  pl.BlockSpec((None, bk, bn), lambda i, j, k: (layer, k, j)),
                  pl.BlockSpec((bm, ktail), lambda i, j, k: (i, 0)),
                  pl.BlockSpec((ktail, bn), lambda i, j, k: (0, j))],
        out_specs=pl.BlockSpec((bm, bn), lambda i, j, k: (i, j)),
        out_shape=jax.ShapeDtypeStruct((m, n), F32),
        scratch_shapes=[pltpu.VMEM((bm, bn), F32)],
        compiler_params=_cparams(("parallel", "parallel", "arbitrary")),
        name="matmul_split_k",
    )(a, w, a_tail, w_tail)


def _ret_kernel(lg_ref, cos_ref, sin_ref, q_ref, k_ref, v_ref, g_ref, ng_ref, o_ref,
                sf_ref, sb_ref, ob_ref, *, nc, c):
    t = pl.program_id(2)
    half = RET_HEAD_DIM // 2
    lgf = lg_ref[0, 0:1, :]
    lgb = lg_ref[0, 1:2, :]
    cos = cos_ref[...]
    sin = sin_ref[...]

    def rot(x):
        x1, x2 = x[:, :half], x[:, half:]
        return jnp.concatenate([x1 * cos - x2 * sin, x1 * sin + x2 * cos], axis=-1)

    def lanes2(d):
        return jnp.concatenate([d, d], axis=-1)

    q = rot(q_ref[...]) * (RET_HEAD_DIM ** -0.5)
    k = rot(k_ref[...])
    vb = v_ref[...].astype(BF16)
    i_row = lax.broadcasted_iota(jnp.int32, (c, half), 0).astype(F32)

    @pl.when(t == 0)
    def _():
        sf_ref[...] = jnp.zeros_like(sf_ref)
        sb_ref[...] = jnp.zeros_like(sb_ref)

    @pl.when(t < nc)
    def _():
        ci = nc - 1 - t
        sb = sb_ref[...]
        qd = q * lanes2(jnp.exp((c - i_row) * lgb))
        ob_ref[ci] = _dot(qd.astype(BF16), sb.astype(BF16))
        kd = k * lanes2(jnp.exp(i_row * lgb))
        sb_ref[...] = jnp.exp(c * lgb[:, :1]) * sb + _dot_tn(kd.astype(BF16), vb)

    @pl.when(t >= nc)
    def _():
        ci = t - nc
        sf = sf_ref[...]
        ii = lax.broadcasted_iota(jnp.int32, (c, c), 0)
        jj = lax.broadcasted_iota(jnp.int32, (c, c), 1)
        rel = (ii - jj).astype(F32)
        decay = jnp.exp(jnp.where(rel >= 0, rel * lgf[:, :1], -rel * lgb[:, :1]))
        scores = _dot_nt(q.astype(BF16), k.astype(BF16)) * decay
        qd = q * lanes2(jnp.exp((i_row + 1.0) * lgf))
        o = (_dot(scores.astype(BF16), vb) + _dot(qd.astype(BF16), sf.astype(BF16)) + ob_ref[ci])
        kd = k * lanes2(jnp.exp((c - 1.0 - i_row) * lgf))
        sf_ref[...] = jnp.exp(c * lgf[:, :1]) * sf + _dot_tn(kd.astype(BF16), vb)
        mu = jnp.mean(o, axis=-1, keepdims=True)
        oc = o - mu
        var = jnp.mean(oc * oc, axis=-1, keepdims=True)
        y = oc * lax.rsqrt(var + GN_EPS) * ng_ref[...]
        o_ref[...] = (_silu(g_ref[...]) * y).astype(o_ref.dtype)


def retention_mixer(proj, norm_g, batch, seq, col0):
    m = proj.shape[0]
    width = norm_g.shape[0]
    heads = width // RET_HEAD_DIM
    c = min(RET_CHUNK, seq)
    nc = seq // c
    half = RET_HEAD_DIM // 2
    cb0 = col0 // RET_HEAD_DIM

    hidx = jnp.arange(heads, dtype=F32)
    gamma = 1.0 - jnp.power(2.0, -5.0 - hidx)
    lg = jnp.log(gamma)
    lg2 = jnp.broadcast_to(jnp.stack([lg, lg[::-1]], axis=1)[:, :, None], (heads, 2, LANES))
    inv = jnp.power(ROPE_BASE, -jnp.arange(half, dtype=F32) / half)
    ang = jnp.arange(seq, dtype=F32)[:, None] * inv[None, :]
    cos, sin = jnp.cos(ang), jnp.sin(ang)

    def chunk(t):
        return jnp.where(t < nc, nc - 1 - t, t - nc)

    def fchunk(t):
        return jnp.maximum(t - nc, 0)

    def qkv_spec(j):
        return pl.BlockSpec((c, RET_HEAD_DIM), lambda b, h, t: (b * nc + chunk(t), cb0 + j * heads + h))

    return pl.pallas_call(
        functools.partial(_ret_kernel, nc=nc, c=c),
        grid=(batch, heads, 2 * nc),
        in_specs=[pl.BlockSpec((1, 2, LANES), lambda b, h, t: (h, 0, 0)),
                  pl.BlockSpec((c, half), lambda b, h, t: (chunk(t), 0)),
                  pl.BlockSpec((c, half), lambda b, h, t: (chunk(t), 0)),
                  qkv_spec(0), qkv_spec(1), qkv_spec(2),
                  pl.BlockSpec((c, RET_HEAD_DIM), lambda b, h, t: (b * nc + fchunk(t), cb0 + 3 * heads + h)),
                  pl.BlockSpec((1, RET_HEAD_DIM), lambda b, h, t: (0, h))],
        out_specs=pl.BlockSpec((c, RET_HEAD_DIM), lambda b, h, t: (b * nc + fchunk(t), h)),
        out_shape=jax.ShapeDtypeStruct((m, width), BF16),
        scratch_shapes=[pltpu.VMEM((RET_HEAD_DIM, RET_HEAD_DIM), F32),
                        pltpu.VMEM((RET_HEAD_DIM, RET_HEAD_DIM), F32),
                        pltpu.VMEM((nc, c, RET_HEAD_DIM), F32)],
        compiler_params=_cparams(("parallel", "parallel", "arbitrary")),
        name="retention",
    )(lg2, cos, sin, proj, proj, proj, proj, norm_g.reshape(1, width))


def _hgrn_kernel(lbl_ref, q_ref, z_ref, v_ref, g_ref, ng_ref, o_ref,
                 st_ref, ob_ref, kk_s, lf_s, loc_s, acc_s, *, nb, rows, layer, depth):
    t = pl.program_id(2)
    d = HGRN_HEAD_DIM
    cs, sub = HGRN_CHUNK, HGRN_SUB
    nchunk = rows // cs
    nsub = cs // sub

    def lower_bound(dirn):
        x = [lbl_ref[2 * l + dirn: 2 * l + dirn + 1, :] for l in range(depth)]
        mx = functools.reduce(jnp.maximum, x)
        e = [jnp.exp(xi - mx) for xi in x]
        tot = functools.reduce(lambda a, b: a + b, e)
        p = [ei / tot for ei in e]
        return functools.reduce(lambda a, b: a + b, p[:layer + 1]) - p[0]

    r_i = lax.broadcasted_iota(jnp.int32, (cs, cs), 0)
    c_i = lax.broadcasted_iota(jnp.int32, (cs, cs), 1)
    same_sub = (r_i // sub) == (c_i // sub)

    def run(rev):
        lb = lower_bound(1 if rev else 0)
        f = lb + (1.0 - lb) * (1.0 / (1.0 + jnp.exp(-z_ref[...])))
        kk_s[...] = 1.0 - f
        lf_s[...] = jnp.log(f)
        tri = (c_i >= r_i) if rev else (c_i <= r_i)
        m_cum = tri.astype(F32)
        m_loc = (tri & same_sub).astype(F32)
        scale = d ** -0.5

        for ci in (range(nchunk - 1, -1, -1) if rev else range(nchunk)):
            r0 = ci * cs
            lf = lf_s[r0:r0 + cs, :]
            cum = _dot_exact(m_cum, lf)
            loc = _dot_exact(m_loc, lf)
            loc_s[r0:r0 + cs, :] = loc
            q = q_ref[r0:r0 + cs, :] * scale
            kk = kk_s[r0:r0 + cs, :]
            vb = v_ref[r0:r0 + cs, :].astype(BF16)
            st = st_ref[...]
            o = _dot_nt((q * jnp.exp(cum)).astype(BF16), st.astype(BF16))
            edge = cum[0:1, :] if rev else cum[cs - 1:cs, :]
            kd = kk * jnp.exp(edge - cum)
            st_ref[...] = st * jnp.exp(edge) + _dot_tn(vb, kd.astype(BF16))
            acc_s[r0:r0 + cs, :] = o
            qt = q * jnp.exp(loc)
            for bi in range(nsub):
                lo, hi = bi * sub, (bi + 1) * sub
                if rev:
                    if bi == nsub - 1:
                        continue
                    src = slice(hi, cs)
                    bound = cum[hi:hi + 1, :]
                else:
                    if bi == 0:
                        continue
                    src = slice(0, lo)
                    bound = cum[lo - 1:lo, :]
                kt = kk[src, :] * jnp.exp(bound - cum[src, :])
                a = _dot_nt(qt[lo:hi, :].astype(BF16), kt.astype(BF16))
                acc_s[r0 + lo:r0 + hi, :] += _dot(a.astype(BF16), vb[src, :])

        t_i = lax.broadcasted_iota(jnp.int32, (sub, d), 0)

        def diag(bi, carry):
            r0 = pl.multiple_of(bi * sub, sub)
            qb = q_ref[pl.ds(r0, sub), :] * scale
            kb = kk_s[pl.ds(r0, sub), :]
            vv = v_ref[pl.ds(r0, sub), :]
            lc = loc_s[pl.ds(r0, sub), :]
            o = jnp.zeros((sub, d), F32)
            for s in range(sub):
                keep = (t_i <= s) if rev else (t_i >= s)
                e = jnp.where(keep, jnp.exp(jnp.minimum(lc - lc[s:s + 1, :], 0.0)), 0.0)
                a = jnp.sum(qb * kb[s:s + 1, :] * e, axis=-1, keepdims=True)
                o = o + a * vv[s:s + 1, :]
            acc_s[pl.ds(r0, sub), :] += o
            return carry

        lax.fori_loop(0, rows // sub, diag, 0)

    @pl.when(t == 0)
    def _():
        st_ref[...] = jnp.zeros_like(st_ref)

    @pl.when(t < nb)
    def _():
        run(True)
        ob_ref[nb - 1 - t] = acc_s[...]

    @pl.when(t == nb)
    def _():
        st_ref[...] = jnp.zeros_like(st_ref)

    @pl.when(t >= nb)
    def _():
        run(False)
        o = acc_s[...] + ob_ref[t - nb]
        o = o * lax.rsqrt(jnp.mean(o * o, axis=-1, keepdims=True) + RMS_EPS)
        o_ref[...] = (_silu(g_ref[...]) * (o * ng_ref[...])).astype(o_ref.dtype)


def hgrn2_mixer(proj, lb_logits, norm_g, layer, batch, seq, col0):
    m = proj.shape[0]
    width = norm_g.shape[0]
    d = HGRN_HEAD_DIM
    heads = width // d
    depth = lb_logits.shape[0]
    rows = min(HGRN_ROWS, seq)
    nb = seq // rows
    cb0 = col0 // d

    def blk(t):
        return jnp.where(t < nb, nb - 1 - t, t - nb)

    def fblk(t):
        return jnp.maximum(t - nb, 0)

    def spec(j):
        return pl.BlockSpec((rows, d), lambda b, h, t: (b * nb + blk(t), cb0 + j * heads + h))

    zspec = pl.BlockSpec((rows, d),
                         lambda b, h, t: (b * nb + blk(t), cb0 + jnp.where(t < nb, 2, 1) * heads + h))
    return pl.pallas_call(
        functools.partial(_hgrn_kernel, nb=nb, rows=rows, layer=layer, depth=depth),
        grid=(batch, heads, 2 * nb),
        in_specs=[pl.BlockSpec((2 * depth, d), lambda b, h, t: (0, h)),
                  spec(0), zspec, spec(3),
                  pl.BlockSpec((rows, d), lambda b, h, t: (b * nb + fblk(t), cb0 + 4 * heads + h)),
                  pl.BlockSpec((1, d), lambda b, h, t: (0, h))],
        out_specs=pl.BlockSpec((rows, d), lambda b, h, t: (b * nb + fblk(t), h)),
        out_shape=jax.ShapeDtypeStruct((m, width), BF16),
        scratch_shapes=[pltpu.VMEM((d, d), F32),
                        pltpu.VMEM((nb, rows, d), F32),
                        pltpu.VMEM((rows, d), F32),
                        pltpu.VMEM((rows, d), F32),
                        pltpu.VMEM((rows, d), F32),
                        pltpu.VMEM((rows, d), F32)],
        compiler_params=_cparams(("parallel", "parallel", "arbitrary")),
        name="hgrn2",
    )(lb_logits.reshape(2 * depth, width), proj, proj, proj, proj, norm_g.reshape(1, width))


def _dft_cos_sin(n):
    idx = np.arange(n)
    ang = 2.0 * np.pi * ((idx[:, None] * idx[None, :]) % n) / n
    return np.cos(ang), np.sin(ang)


def _fft_chan_kernel(z_ref, m_ref, vr_ref, vi_ref, *, groups):
    gd = FOURIER_GROUP_DIM
    mat = m_ref[...]
    for g in range(groups):
        pq = _dot_exact(z_ref[:, g * gd:(g + 1) * gd], mat)
        vr_ref[:, g * gd:(g + 1) * gd] = pq[:, :gd]
        vi_ref[:, g * gd:(g + 1) * gd] = pq[:, gd:]


def _fft_stage1_kernel(vr_ref, vi_ref, m_ref, twc_ref, tws_ref, ur_ref, ui_ref, *, n1, reps):
    x = jnp.concatenate([vr_ref[...], vi_ref[...]], axis=0)
    tt = _dot_exact(m_ref[...], x)
    tr, ti = tt[:n1], tt[n1:]
    cw = jnp.tile(twc_ref[...], (1, reps))
    sw = jnp.tile(tws_ref[...], (1, reps))
    ur_ref[...] = tr * cw + ti * sw
    ui_ref[...] = ti * cw - tr * sw


def _fft_stage2_kernel(ur_ref, ui_ref, m_ref, w_ref, b_ref, o_ref, *, groups, scale):
    gd = FOURIER_GROUP_DIM
    x = jnp.concatenate([ur_ref[...], ui_ref[...]], axis=0)
    spec = _dot_exact(m_ref[...], x) * scale
    for g in range(groups):
        y = _dot(spec[:, g * gd:(g + 1) * gd].astype(BF16), w_ref[g].astype(BF16))
        o_ref[:, g * gd:(g + 1) * gd] = (y + b_ref[:, g * gd:(g + 1) * gd]).astype(o_ref.dtype)


def fourier_mixer(proj, w, bias, batch, seq, col0):
    m = proj.shape[0]
    groups, gd, _ = w.shape
    width = groups * gd
    n1 = min(FFT_N1, seq)
    n2 = seq // n1
    br = min(512, m)

    cc, sc = _dft_cos_sin(gd)
    chan = jnp.asarray(np.concatenate([cc, -sc], axis=1), F32)
    vr, vi = pl.pallas_call(
        functools.partial(_fft_chan_kernel, groups=groups),
        grid=(m // br,),
        in_specs=[pl.BlockSpec((br, width), lambda i: (i, col0 // width)),
                  pl.BlockSpec((gd, 2 * gd), lambda i: (0, 0))],
        out_specs=[pl.BlockSpec((br, width), lambda i: (i, 0))] * 2,
        out_shape=[jax.ShapeDtypeStruct((m, width), F32)] * 2,
        compiler_params=_cparams(("parallel",)),
        name="fft_channels",
    )(proj, chan)

    c1, s1 = _dft_cos_sin(n1)
    m1 = jnp.asarray(np.block([[c1, s1], [-s1, c1]]), F32)
    k1 = np.arange(n1)
    bb = np.arange(n2)
    tw = 2.0 * np.pi * ((bb[:, None] * k1[None, :]) % seq) / seq
    twc = jnp.broadcast_to(jnp.asarray(np.cos(tw), F32)[:, :, None], (n2, n1, LANES))
    tws = jnp.broadcast_to(jnp.asarray(np.sin(tw), F32)[:, :, None], (n2, n1, LANES))
    vspec = pl.BlockSpec((None, n1, width), lambda b, j: (b, 0, j))
    tspec = pl.BlockSpec((None, n1, LANES), lambda b, j: (j, 0, 0))
    ur, ui = pl.pallas_call(
        functools.partial(_fft_stage1_kernel, n1=n1, reps=width // LANES),
        grid=(batch, n2),
        in_specs=[vspec, vspec, pl.BlockSpec((2 * n1, 2 * n1), lambda b, j: (0, 0)), tspec, tspec],
        out_specs=[vspec, vspec],
        out_shape=[jax.ShapeDtypeStruct((batch, n1, n2 * width), F32)] * 2,
        compiler_params=_cparams(("parallel", "parallel")),
        name="fft_stage1",
    )(vr.reshape(batch, n1, n2 * width), vi.reshape(batch, n1, n2 * width), m1, twc, tws)

    c2, s2 = _dft_cos_sin(n2)
    m2 = jnp.asarray(np.concatenate([c2, s2], axis=1), F32)
    uspec = pl.BlockSpec((None, n2, width), lambda b, j: (b, j, 0))
    out = pl.pallas_call(
        functools.partial(_fft_stage2_kernel, groups=groups, scale=1.0 / math.sqrt(seq * gd)),
        grid=(batch, n1),
        in_specs=[uspec, uspec, pl.BlockSpec((n2, 2 * n2), lambda b, j: (0, 0)),
                  pl.BlockSpec((groups, gd, gd), lambda b, j: (0, 0, 0)),
                  pl.BlockSpec((1, width), lambda b, j: (0, 0))],
        out_specs=pl.BlockSpec((None, n2, width), lambda b, j: (b, 0, j)),
        out_shape=jax.ShapeDtypeStruct((batch, n2, n1 * width), BF16),
        compiler_params=_cparams(("parallel", "parallel")),
        name="fft_stage2",
    )(ur.reshape(batch, n1 * n2, width), ui.reshape(batch, n1 * n2, width), m2, w, bias.reshape(1, width))
    return out.reshape(m, width)


def _xattn_kernel(q_ref, k_ref, v_ref, o_ref):
    s = _dot_nt(q_ref[...], k_ref[...]) * (XATTN_HEAD_DIM ** -0.5)
    p = jnp.exp(s - jnp.max(s, axis=-1, keepdims=True))
    denom = jnp.sum(p, axis=-1, keepdims=True)
    o_ref[...] = (_dot(p.astype(BF16), v_ref[...]) / denom).astype(o_ref.dtype)


def cross_attention_core(q, k, v, batch, seq, mem_tokens, ts=512):
    m, width = q.shape
    hd = XATTN_HEAD_DIM
    heads = width // hd
    ts = min(ts, seq)
    nt = seq // ts
    kv = pl.BlockSpec((mem_tokens, hd), lambda b, h, i: (b, h))
    qs = pl.BlockSpec((ts, hd), lambda b, h, i: (b * nt + i, h))
    return pl.pallas_call(
        _xattn_kernel,
        grid=(batch, heads, nt),
        in_specs=[qs, kv, kv],
        out_specs=qs,
        out_shape=jax.ShapeDtypeStruct((m, width), BF16),
        compiler_params=_cparams(("parallel", "parallel", "parallel")),
        name="xattn",
    )(q, k, v)


def kernel(x, mem, mem_norm_g, pre_mix_g, w_in, ret_norm_g, hgrn_lb_logits, hgrn_norm_g, fourier_w, fourier_b,
           w_out, post_mix_g, pre_xattn_g, xattn_wq, xattn_wk, xattn_wv, xattn_wo, post_xattn_g, pre_ffn_g,
           ffn_w_gate, ffn_w_up, ffn_w_down, post_ffn_g):
    batch, seq, d = x.shape
    depth = w_in.shape[0]
    mem_tokens = mem.shape[1]
    m = batch * seq
    ret_w = ret_norm_g.shape[1]
    hgrn_w = hgrn_norm_g.shape[1]
    ffn_h = ffn_w_gate.shape[2]
    tail = ffn_h % 1024
    main = ffn_h - tail

    h = x.reshape(m, d)
    mem_n = rmsnorm_rows(mem.reshape(batch * mem_tokens, d), mem_norm_g)
    a = rmsnorm_rows(h, pre_mix_g[0])
    for l in range(depth):
        proj = matmul(a, w_in, l, F32, 1024, 1024, 1024)
        y_ret = retention_mixer(proj, ret_norm_g[l], batch, seq, 0)
        y_hgrn = hgrn2_mixer(proj, hgrn_lb_logits, hgrn_norm_g[l], l, batch, seq, 4 * ret_w)
        y_fft = fourier_mixer(proj, fourier_w[l], fourier_b[l], batch, seq, 4 * ret_w + 5 * hgrn_w)
        mixed = matmul(jnp.concatenate([y_ret, y_hgrn, y_fft], axis=-1), w_out, l, F32, 1024, 1024, 1024)
        h, c = add_norm(h, mixed, post_mix_g[l], pre_xattn_g[l])
        q = matmul(c, xattn_wq, l, BF16, 1024, 1024, 1024)
        kx = matmul(mem_n, xattn_wk, l, BF16, 512, 1024, 1024)
        vx = matmul(mem_n, xattn_wv, l, BF16, 512, 1024, 1024)
        att = cross_attention_core(q, kx, vx, batch, seq, mem_tokens)
        xa = matmul(att, xattn_wo, l, F32, 1024, 1024, 1024)
        h, f = add_norm(h, xa, post_xattn_g[l], pre_ffn_g[l])
        hid = swiglu_matmul(f, ffn_w_gate, ffn_w_up, l, 1024, 1024, 1024, n_cols=main)
        hid_t = swiglu_matmul(f, ffn_w_gate[l:l + 1, :, main:], ffn_w_up[l:l + 1, :, main:], 0, 1024, tail, 1024)
        ff = matmul_split_k(hid, ffn_w_down, l, hid_t, ffn_w_down[l, main:], 1024, 1024, 1024)
        h, a = add_norm(h, ff, post_ffn_g[l], pre_mix_g[l + 1] if l + 1 < depth else None)
    return h.reshape(batch, seq, d)
```

```python
import functools
import math

import numpy as np
import jax
import jax.numpy as jnp
from jax import lax
from jax.experimental import pallas as pl
from jax.experimental.pallas import tpu as pltpu

F32 = jnp.float32
BF16 = jnp.bfloat16
HIGHEST = lax.Precision.HIGHEST

RET_HEAD_DIM = 256
HGRN_HEAD_DIM = 128
FOURIER_GROUP_DIM = 128
XATTN_HEADS = 4
XATTN_HEAD_DIM = 256
ROPE_BASE = 10000.0
RMS_EPS = 1e-6
GN_EPS = 1e-5

LANES = 128
V7X_VMEM_BYTES = 64 * 1024 * 1024
VMEM_LIMIT = V7X_VMEM_BYTES - 8 * 1024 * 1024

RET_CHUNK = 256
HGRN_ROWS = 256
FFT_N1 = 64


def _cparams(sem):
    return pltpu.CompilerParams(dimension_semantics=sem, vmem_limit_bytes=VMEM_LIMIT)


def _tile(n, preferred):
    for t in range(min(preferred, n) // LANES * LANES, 0, -LANES):
        if n % t == 0:
            return t
    return n


def _dot(a, b):
    return jnp.dot(a, b, preferred_element_type=F32)


def _dot_nt(a, b):
    return lax.dot_general(a, b, (((1,), (1,)), ((), ())), preferred_element_type=F32)


def _dot_tn(a, b):
    return lax.dot_general(a, b, (((0,), (0,)), ((), ())), preferred_element_type=F32)


def _dot_exact(a, b):
    return jnp.dot(a, b, preferred_element_type=F32, precision=HIGHEST)


def _rms(x, g):
    ms = jnp.mean(x * x, axis=-1, keepdims=True)
    return x * lax.rsqrt(ms + RMS_EPS) * g


def _silu(x):
    return x * (1.0 / (1.0 + jnp.exp(-x)))


def _norm_kernel(x_ref, g_ref, o_ref):
    o_ref[...] = _rms(x_ref[...], g_ref[...]).astype(o_ref.dtype)


def rmsnorm_rows(x, g, br=256):
    m, d = x.shape
    out_dtype = BF16
    br = min(br, m)
    return pl.pallas_call(
        _norm_kernel,
        grid=(m // br,),
        in_specs=[pl.BlockSpec((br, d), lambda i: (i, 0)),
                  pl.BlockSpec((1, d), lambda i: (0, 0))],
        out_specs=pl.BlockSpec((br, d), lambda i: (i, 0)),
        out_shape=jax.ShapeDtypeStruct((m, d), out_dtype),
        compiler_params=_cparams(("parallel",)),
        name="rmsnorm_rows",
    )(x, g.reshape(1, d))


def _add_norm_kernel(h_ref, y_ref, gp_ref, gn_ref, ho_ref, a_ref):
    h = h_ref[...] + _rms(y_ref[...], gp_ref[...])
    ho_ref[...] = h
    a_ref[...] = _rms(h, gn_ref[...]).astype(a_ref.dtype)


def _add_norm_last_kernel(h_ref, y_ref, gp_ref, ho_ref):
    ho_ref[...] = h_ref[...] + _rms(y_ref[...], gp_ref[...])


def add_norm(h, y, g_post, g_next, br=256):
    m, d = h.shape
    row = pl.BlockSpec((br, d), lambda i: (i, 0))
    vec = pl.BlockSpec((1, d), lambda i: (0, 0))
    if g_next is None:
        return pl.pallas_call(
            _add_norm_last_kernel, grid=(m // br,),
            in_specs=[row, row, vec], out_specs=row,
            out_shape=jax.ShapeDtypeStruct((m, d), F32),
            compiler_params=_cparams(("parallel",)), name="add_norm_last",
        )(h, y, g_post.reshape(1, d)), None
    return pl.pallas_call(
        _add_norm_kernel, grid=(m // br,),
        in_specs=[row, row, vec, vec], out_specs=[row, row],
        out_shape=[jax.ShapeDtypeStruct((m, d), F32), jax.ShapeDtypeStruct((m, d), BF16)],
        compiler_params=_cparams(("parallel",)), name="add_norm",
    )(h, y, g_post.reshape(1, d), g_next.reshape(1, d))


def _mm_kernel(a_ref, w_ref, o_ref, acc_ref, *, nk):
    k = pl.program_id(2)

    @pl.when(k == 0)
    def _():
        acc_ref[...] = jnp.zeros_like(acc_ref)

    acc_ref[...] += _dot(a_ref[...], w_ref[...].astype(BF16))

    @pl.when(k == nk - 1)
    def _():
        o_ref[...] = acc_ref[...].astype(o_ref.dtype)


def matmul(a, w, layer, out_dtype, bm, bn, bk):
    m, kdim = a.shape
    n = w.shape[2]
    bm, bn, bk = _tile(m, bm), _tile(n, bn), _tile(kdim, bk)
    assert m % bm == 0 and n % bn == 0 and kdim % bk == 0
    nk = kdim // bk
    return pl.pallas_call(
        functools.partial(_mm_kernel, nk=nk),
        grid=(m // bm, n // bn, nk),
        in_specs=[pl.BlockSpec((bm, bk), lambda i, j, k: (i, k)),
                  pl.BlockSpec((None, bk, bn), lambda i, j, k: (layer, k, j))],
        out_specs=pl.BlockSpec((bm, bn), lambda i, j, k: (i, j)),
        out_shape=jax.ShapeDtypeStruct((m, n), out_dtype),
        scratch_shapes=[pltpu.VMEM((bm, bn), F32)],
        compiler_params=_cparams(("parallel", "parallel", "arbitrary")),
        name="matmul",
    )(a, w)


def _swiglu_kernel(a_ref, wg_ref, wu_ref, o_ref, accg_ref, accu_ref, *, nk):
    k = pl.program_id(2)

    @pl.when(k == 0)
    def _():
        accg_ref[...] = jnp.zeros_like(accg_ref)
        accu_ref[...] = jnp.zeros_like(accu_ref)

    a = a_ref[...]
    accg_ref[...] += _dot(a, wg_ref[...].astype(BF16))
    accu_ref[...] += _dot(a, wu_ref[...].astype(BF16))

    @pl.when(k == nk - 1)
    def _():
        o_ref[...] = (_silu(accg_ref[...]) * accu_ref[...]).astype(o_ref.dtype)


def swiglu_matmul(a, wg, wu, layer, bm, bn, bk, n_cols=None):
    m, kdim = a.shape
    n = wg.shape[2] if n_cols is None else n_cols
    bm, bn, bk = _tile(m, bm), _tile(n, bn), _tile(kdim, bk)
    assert m % bm == 0 and n % bn == 0 and kdim % bk == 0
    nk = kdim // bk
    wspec = pl.BlockSpec((None, bk, bn), lambda i, j, k: (layer, k, j))
    return pl.pallas_call(
        functools.partial(_swiglu_kernel, nk=nk),
        grid=(m // bm, n // bn, nk),
        in_specs=[pl.BlockSpec((bm, bk), lambda i, j, k: (i, k)), wspec, wspec],
        out_specs=pl.BlockSpec((bm, bn), lambda i, j, k: (i, j)),
        out_shape=jax.ShapeDtypeStruct((m, n), BF16),
        scratch_shapes=[pltpu.VMEM((bm, bn), F32), pltpu.VMEM((bm, bn), F32)],
        compiler_params=_cparams(("parallel", "parallel", "arbitrary")),
        name="swiglu_matmul",
    )(a, wg, wu)


def _mm2_kernel(a_ref, w_ref, at_ref, wt_ref, o_ref, acc_ref, *, nk):
    k = pl.program_id(2)

    @pl.when(k == 0)
    def _():
        acc_ref[...] = _dot(at_ref[...], wt_ref[...].astype(BF16))

    acc_ref[...] += _dot(a_ref[...], w_ref[...].astype(BF16))

    @pl.when(k == nk - 1)
    def _():
        o_ref[...] = acc_ref[...].astype(o_ref.dtype)


def matmul_split_k(a, w, layer, a_tail, w_tail, bm, bn, bk):
    m, kmain = a.shape
    ktail = a_tail.shape[1]
    n = w.shape[2]
    bm, bn, bk = _tile(m, bm), _tile(n, bn), _tile(kmain, bk)
    assert m % bm == 0 and n % bn == 0 and kmain % bk == 0
    nk = kmain // bk
    return pl.pallas_call(
        functools.partial(_mm2_kernel, nk=nk),
        grid=(m // bm, n // bn, nk),
        in_specs=[pl.BlockSpec((bm, bk), lambda i, j, k: (i, k)),
                  pl.BlockSpec((None, bk, bn), lambda i, j, k: (layer, k, j)),
                  pl.BlockSpec((bm, ktail), lambda i, j, k: (i, 0)),
                  pl.BlockSpec((ktail, bn), lambda i, j, k: (0, j))],
        out_specs=pl.BlockSpec((bm, bn), lambda i, j, k: (i, j)),
        out_shape=jax.ShapeDtypeStruct((m, n), F32),
        scratch_shapes=[pltpu.VMEM((bm, bn), F32)],
        compiler_params=_cparams(("parallel", "parallel", "arbitrary")),
        name="matmul_split_k",
    )(a, w, a_tail, w_tail)


def _ret_kernel(lg_ref, cos_ref, sin_ref, q_ref, k_ref, v_ref, g_ref, ng_ref, o_ref,
                sf_ref, sb_ref, ob_ref, *, nc, c):
    t = pl.program_id(2)
    half = RET_HEAD_DIM // 2
    lgf = lg_ref[0, 0:1, :]
    lgb = lg_ref[0, 1:2, :]
    cos = cos_ref[...]
    sin = sin_ref[...]

    def rot(x):
        x1, x2 = x[:, :half], x[:, half:]
        return jnp.concatenate([x1 * cos - x2 * sin, x1 * sin + x2 * cos], axis=-1)

    def lanes2(d):
        return jnp.concatenate([d, d], axis=-1)

    q = rot(q_ref[...]) * (RET_HEAD_DIM ** -0.5)
    k = rot(k_ref[...])
    vb = v_ref[...].astype(BF16)
    i_row = lax.broadcasted_iota(jnp.int32, (c, half), 0).astype(F32)

    @pl.when(t == 0)
    def _():
        sf_ref[...] = jnp.zeros_like(sf_ref)
        sb_ref[...] = jnp.zeros_like(sb_ref)

    @pl.when(t < nc)
    def _():
        ci = nc - 1 - t
        sb = sb_ref[...]
        qd = q * lanes2(jnp.exp((c - i_row) * lgb))
        ob_ref[ci] = _dot(qd.astype(BF16), sb.astype(BF16))
        kd = k * lanes2(jnp.exp(i_row * lgb))
        sb_ref[...] = jnp.exp(c * lgb[:, :1]) * sb + _dot_tn(kd.astype(BF16), vb)

    @pl.when(t >= nc)
    def _():
        ci = t - nc
        sf = sf_ref[...]
        ii = lax.broadcasted_iota(jnp.int32, (c, c), 0)
        jj = lax.broadcasted_iota(jnp.int32, (c, c), 1)
        rel = (ii - jj).astype(F32)
        decay = jnp.exp(jnp.where(rel >= 0, rel * lgf[:, :1], -rel * lgb[:, :1]))
        scores = _dot_nt(q.astype(BF16), k.astype(BF16)) * decay
        qd = q * lanes2(jnp.exp((i_row + 1.0) * lgf))
        o = (_dot(scores.astype(BF16), vb) + _dot(qd.astype(BF16), sf.astype(BF16)) + ob_ref[ci])
        kd = k * lanes2(jnp.exp((c - 1.0 - i_row) * lgf))
        sf_ref[...] = jnp.exp(c * lgf[:, :1]) * sf + _dot_tn(kd.astype(BF16), vb)
        mu = jnp.mean(o, axis=-1, keepdims=True)
        oc = o - mu
        var = jnp.mean(oc * oc, axis=-1, keepdims=True)
        y = oc * lax.rsqrt(var + GN_EPS) * ng_ref[...]
        o_ref[...] = (_silu(g_ref[...]) * y).astype(o_ref.dtype)


def retention_mixer(proj, norm_g, batch, seq, col0):
    m = proj.shape[0]
    width = norm_g.shape[0]
    heads = width // RET_HEAD_DIM
    c = min(RET_CHUNK, seq)
    nc = seq // c
    half = RET_HEAD_DIM // 2
    cb0 = col0 // RET_HEAD_DIM

    hidx = jnp.arange(heads, dtype=F32)
    gamma = 1.0 - jnp.power(2.0, -5.0 - hidx)
    lg = jnp.log(gamma)
    lg2 = jnp.broadcast_to(jnp.stack([lg, lg[::-1]], axis=1)[:, :, None], (heads, 2, LANES))
    inv = jnp.power(ROPE_BASE, -jnp.arange(half, dtype=F32) / half)
    ang = jnp.arange(seq, dtype=F32)[:, None] * inv[None, :]
    cos, sin = jnp.cos(ang), jnp.sin(ang)

    def chunk(t):
        return jnp.where(t < nc, nc - 1 - t, t - nc)

    def fchunk(t):
        return jnp.maximum(t - nc, 0)

    def qkv_spec(j):
        return pl.BlockSpec((c, RET_HEAD_DIM), lambda b, h, t: (b * nc + chunk(t), cb0 + j * heads + h))

    return pl.pallas_call(
        functools.partial(_ret_kernel, nc=nc, c=c),
        grid=(batch, heads, 2 * nc),
        in_specs=[pl.BlockSpec((1, 2, LANES), lambda b, h, t: (h, 0, 0)),
                  pl.BlockSpec((c, half), lambda b, h, t: (chunk(t), 0)),
                  pl.BlockSpec((c, half), lambda b, h, t: (chunk(t), 0)),
                  qkv_spec(0), qkv_spec(1), qkv_spec(2),
                  pl.BlockSpec((c, RET_HEAD_DIM), lambda b, h, t: (b * nc + fchunk(t), cb0 + 3 * heads + h)),
                  pl.BlockSpec((1, RET_HEAD_DIM), lambda b, h, t: (0, h))],
        out_specs=pl.BlockSpec((c, RET_HEAD_DIM), lambda b, h, t: (b * nc + fchunk(t), h)),
        out_shape=jax.ShapeDtypeStruct((m, width), BF16),
        scratch_shapes=[pltpu.VMEM((RET_HEAD_DIM, RET_HEAD_DIM), F32),
                        pltpu.VMEM((RET_HEAD_DIM, RET_HEAD_DIM), F32),
                        pltpu.VMEM((nc, c, RET_HEAD_DIM), F32)],
        compiler_params=_cparams(("parallel", "parallel", "arbitrary")),
        name="retention",
    )(lg2, cos, sin, proj, proj, proj, proj, norm_g.reshape(1, width))


def _hgrn_kernel(lbl_ref, lvl_ref, q_ref, z_ref, v_ref, g_ref, ng_ref, o_ref,
                 st_ref, ob_ref, *, nb, rows, layer, depth):
    t = pl.program_id(2)
    d = HGRN_HEAD_DIM
    nlev = rows.bit_length() - 1

    def lower_bound(dirn):
        x = [lbl_ref[2 * l + dirn: 2 * l + dirn + 1, :] for l in range(depth)]
        mx = functools.reduce(jnp.maximum, x)
        e = [jnp.exp(xi - mx) for xi in x]
        tot = functools.reduce(lambda a, b: a + b, e)
        p = [ei / tot for ei in e]
        return functools.reduce(lambda a, b: a + b, p[:layer + 1]) - p[0]

    row = lax.broadcasted_iota(jnp.int32, (rows, d), 0)
    r_i = lax.broadcasted_iota(jnp.int32, (rows, rows), 0)
    c_i = lax.broadcasted_iota(jnp.int32, (rows, rows), 1)

    def split3(x):
        hi = x.astype(BF16)
        r1 = x - hi.astype(F32)
        mid = r1.astype(BF16)
        lo = (r1 - mid.astype(F32)).astype(BF16)
        return jnp.concatenate([hi, mid, lo], axis=1)

    def group_rows(x, size, pick):
        parts = [jnp.broadcast_to(x[base + pick:base + pick + 1, :], (size, d)) for base in range(0, rows, size)]
        return parts[0] if len(parts) == 1 else jnp.concatenate(parts, axis=0)

    def run(rev):
        lb = lower_bound(1 if rev else 0)
        f = lb + (1.0 - lb) * (1.0 / (1.0 + jnp.exp(-z_ref[...])))
        kk = 1.0 - f
        lf = jnp.log(f)
        q = q_ref[...] * (d ** -0.5)
        vb = v_ref[...].astype(BF16)

        tri = ((c_i >= r_i) if rev else (c_i <= r_i)).astype(BF16)
        parts = _dot(tri, split3(lf))
        cum = parts[:, :d] + parts[:, d:2 * d] + parts[:, 2 * d:]
        edge = cum[0:1, :] if rev else cum[rows - 1:rows, :]

        st = st_ref[...]
        o = _dot_nt((q * jnp.exp(cum)).astype(BF16), st.astype(BF16))
        kd = kk * jnp.exp(edge - cum)
        st_ref[...] = st * jnp.exp(edge) + _dot_tn(vb, kd.astype(BF16))

        lvl = lvl_ref[1 if rev else 0]
        a = jnp.where(lvl == nlev, jnp.sum(q * kk, axis=-1, keepdims=True), 0.0)
        lf_up = pltpu.roll(lf, rows - 1, 0)
        lf_dn = pltpu.roll(lf, 1, 0)
        for lev in range(nlev):
            m = 1 << lev
            upper = (row & m) != 0
            target = jnp.logical_not(upper) if rev else upper
            if m == 1:
                dlt = jnp.where(upper, 0.0, lf) if rev else jnp.where(upper, lf, 0.0)
            elif m == 2:
                r4 = row & 3
                if rev:
                    dlt = jnp.where(r4 == 0, lf + lf_up, jnp.where(r4 == 1, lf, jnp.where(r4 == 2, 0.0, lf_dn)))
                else:
                    dlt = jnp.where(r4 == 0, lf_up, jnp.where(r4 == 1, 0.0, jnp.where(r4 == 2, lf, lf + lf_dn)))
            else:
                bound = group_rows(cum, 2 * m, m if rev else m - 1)
                dlt = jnp.where(target, cum - bound, bound - cum)
            y = (jnp.where(target, q, kk) * jnp.exp(dlt)).astype(BF16)
            a = jnp.where(lvl == lev, _dot_nt(y, y), a)
        return o + _dot(a.astype(BF16), vb)

    @pl.when(t == 0)
    def _():
        st_ref[...] = jnp.zeros_like(st_ref)

    @pl.when(t < nb)
    def _():
        ob_ref[nb - 1 - t] = run(True)

    @pl.when(t == nb)
    def _():
        st_ref[...] = jnp.zeros_like(st_ref)

    @pl.when(t >= nb)
    def _():
        o = run(False) + ob_ref[t - nb]
        o = o * lax.rsqrt(jnp.mean(o * o, axis=-1, keepdims=True) + RMS_EPS)
        o_ref[...] = (_silu(g_ref[...]) * (o * ng_ref[...])).astype(o_ref.dtype)


def hgrn2_mixer(proj, lb_logits, norm_g, layer, batch, seq, col0):
    m = proj.shape[0]
    width = norm_g.shape[0]
    d = HGRN_HEAD_DIM
    heads = width // d
    depth = lb_logits.shape[0]
    rows = min(HGRN_ROWS, seq)
    nb = seq // rows
    cb0 = col0 // d

    def blk(t):
        return jnp.where(t < nb, nb - 1 - t, t - nb)

    def fblk(t):
        return jnp.maximum(t - nb, 0)

    def spec(j):
        return pl.BlockSpec((rows, d), lambda b, h, t: (b * nb + blk(t), cb0 + j * heads + h))

    zspec = pl.BlockSpec((rows, d),
                         lambda b, h, t: (b * nb + blk(t), cb0 + jnp.where(t < nb, 2, 1) * heads + h))
    assert rows & (rows - 1) == 0
    nlev = rows.bit_length() - 1
    idx = np.arange(rows)
    x = idx[:, None] ^ idx[None, :]
    lev = np.where(x > 0, np.floor(np.log2(np.maximum(x, 1))), nlev).astype(np.int32)
    lvl = jnp.asarray(np.stack([np.where(idx[:, None] >= idx[None, :], lev, -1),
                                np.where(idx[:, None] <= idx[None, :], lev, -1)]), jnp.int32)
    return pl.pallas_call(
        functools.partial(_hgrn_kernel, nb=nb, rows=rows, layer=layer, depth=depth),
        grid=(batch, heads, 2 * nb),
        in_specs=[pl.BlockSpec((2 * depth, d), lambda b, h, t: (0, h)),
                  pl.BlockSpec((2, rows, rows), lambda b, h, t: (0, 0, 0)),
                  spec(0), zspec, spec(3),
                  pl.BlockSpec((rows, d), lambda b, h, t: (b * nb + fblk(t), cb0 + 4 * heads + h)),
                  pl.BlockSpec((1, d), lambda b, h, t: (0, h))],
        out_specs=pl.BlockSpec((rows, d), lambda b, h, t: (b * nb + fblk(t), h)),
        out_shape=jax.ShapeDtypeStruct((m, width), BF16),
        scratch_shapes=[pltpu.VMEM((d, d), F32),
                        pltpu.VMEM((nb, rows, d), F32)],
        compiler_params=_cparams(("parallel", "parallel", "arbitrary")),
        name="hgrn2",
    )(lb_logits.reshape(2 * depth, width), lvl, proj, proj, proj, proj, norm_g.reshape(1, width))


def _dft_cos_sin(n):
    idx = np.arange(n)
    ang = 2.0 * np.pi * ((idx[:, None] * idx[None, :]) % n) / n
    return np.cos(ang), np.sin(ang)


def _fft_chan_kernel(z_ref, m_ref, vr_ref, vi_ref, *, groups):
    gd = FOURIER_GROUP_DIM
    mat = m_ref[...]
    for g in range(groups):
        pq = _dot_exact(z_ref[:, g * gd:(g + 1) * gd], mat)
        vr_ref[:, g * gd:(g + 1) * gd] = pq[:, :gd]
        vi_ref[:, g * gd:(g + 1) * gd] = pq[:, gd:]


def _fft_stage1_kernel(vr_ref, vi_ref, m_ref, twc_ref, tws_ref, ur_ref, ui_ref, *, n1, reps):
    x = jnp.concatenate([vr_ref[...], vi_ref[...]], axis=0)
    tt = _dot_exact(m_ref[...], x)
    tr, ti = tt[:n1], tt[n1:]
    cw = jnp.tile(twc_ref[...], (1, reps))
    sw = jnp.tile(tws_ref[...], (1, reps))
    ur_ref[...] = tr * cw + ti * sw
    ui_ref[...] = ti * cw - tr * sw


def _fft_stage2_kernel(ur_ref, ui_ref, m_ref, w_ref, b_ref, o_ref, *, groups, scale):
    gd = FOURIER_GROUP_DIM
    x = jnp.concatenate([ur_ref[...], ui_ref[...]], axis=0)
    spec = _dot_exact(m_ref[...], x) * scale
    for g in range(groups):
        y = _dot(spec[:, g * gd:(g + 1) * gd].astype(BF16), w_ref[g].astype(BF16))
        o_ref[:, g * gd:(g + 1) * gd] = (y + b_ref[:, g * gd:(g + 1) * gd]).astype(o_ref.dtype)


def fourier_mixer(proj, w, bias, batch, seq, col0):
    m = proj.shape[0]
    groups, gd, _ = w.shape
    width = groups * gd
    n1 = min(FFT_N1, seq)
    n2 = seq // n1
    br = min(512, m)

    cc, sc = _dft_cos_sin(gd)
    chan = jnp.asarray(np.concatenate([cc, -sc], axis=1), F32)
    vr, vi = pl.pallas_call(
        functools.partial(_fft_chan_kernel, groups=groups),
        grid=(m // br,),
        in_specs=[pl.BlockSpec((br, width), lambda i: (i, col0 // width)),
                  pl.BlockSpec((gd, 2 * gd), lambda i: (0, 0))],
        out_specs=[pl.BlockSpec((br, width), lambda i: (i, 0))] * 2,
        out_shape=[jax.ShapeDtypeStruct((m, width), F32)] * 2,
        compiler_params=_cparams(("parallel",)),
        name="fft_channels",
    )(proj, chan)

    c1, s1 = _dft_cos_sin(n1)
    m1 = jnp.asarray(np.block([[c1, s1], [-s1, c1]]), F32)
    k1 = np.arange(n1)
    bb = np.arange(n2)
    tw = 2.0 * np.pi * ((bb[:, None] * k1[None, :]) % seq) / seq
    twc = jnp.broadcast_to(jnp.asarray(np.cos(tw), F32)[:, :, None], (n2, n1, LANES))
    tws = jnp.broadcast_to(jnp.asarray(np.sin(tw), F32)[:, :, None], (n2, n1, LANES))
    vspec = pl.BlockSpec((None, n1, width), lambda b, j: (b, 0, j))
    tspec = pl.BlockSpec((None, n1, LANES), lambda b, j: (j, 0, 0))
    ur, ui = pl.pallas_call(
        functools.partial(_fft_stage1_kernel, n1=n1, reps=width // LANES),
        grid=(batch, n2),
        in_specs=[vspec, vspec, pl.BlockSpec((2 * n1, 2 * n1), lambda b, j: (0, 0)), tspec, tspec],
        out_specs=[vspec, vspec],
        out_shape=[jax.ShapeDtypeStruct((batch, n1, n2 * width), F32)] * 2,
        compiler_params=_cparams(("parallel", "parallel")),
        name="fft_stage1",
    )(vr.reshape(batch, n1, n2 * width), vi.reshape(batch, n1, n2 * width), m1, twc, tws)

    c2, s2 = _dft_cos_sin(n2)
    m2 = jnp.asarray(np.concatenate([c2, s2], axis=1), F32)
    uspec = pl.BlockSpec((None, n2, width), lambda b, j: (b, j, 0))
    out = pl.pallas_call(
        functools.partial(_fft_stage2_kernel, groups=groups, scale=1.0 / math.sqrt(seq * gd)),
        grid=(batch, n1),
        in_specs=[uspec, uspec, pl.BlockSpec((n2, 2 * n2), lambda b, j: (0, 0)),
                  pl.BlockSpec((groups, gd, gd), lambda b, j: (0, 0, 0)),
                  pl.BlockSpec((1, width), lambda b, j: (0, 0))],
        out_specs=pl.BlockSpec((None, n2, width), lambda b, j: (b, 0, j)),
        out_shape=jax.ShapeDtypeStruct((batch, n2, n1 * width), BF16),
        compiler_params=_cparams(("parallel", "parallel")),
        name="fft_stage2",
    )(ur.reshape(batch, n1 * n2, width), ui.reshape(batch, n1 * n2, width), m2, w, bias.reshape(1, width))
    return out.reshape(m, width)


def _xattn_kernel(q_ref, k_ref, v_ref, o_ref):
    s = _dot_nt(q_ref[...], k_ref[...]) * (XATTN_HEAD_DIM ** -0.5)
    p = jnp.exp(s - jnp.max(s, axis=-1, keepdims=True))
    denom = jnp.sum(p, axis=-1, keepdims=True)
    o_ref[...] = (_dot(p.astype(BF16), v_ref[...]) / denom).astype(o_ref.dtype)


def cross_attention_core(q, k, v, batch, seq, mem_tokens, ts=512):
    m, width = q.shape
    hd = XATTN_HEAD_DIM
    heads = width // hd
    ts = min(ts, seq)
    nt = seq // ts
    kv = pl.BlockSpec((mem_tokens, hd), lambda b, h, i: (b, h))
    qs = pl.BlockSpec((ts, hd), lambda b, h, i: (b * nt + i, h))
    return pl.pallas_call(
        _xattn_kernel,
        grid=(batch, heads, nt),
        in_specs=[qs, kv, kv],
        out_specs=qs,
        out_shape=jax.ShapeDtypeStruct((m, width), BF16),
        compiler_params=_cparams(("parallel", "parallel", "parallel")),
        name="xattn",
    )(q, k, v)


def kernel(x, mem, mem_norm_g, pre_mix_g, w_in, ret_norm_g, hgrn_lb_logits, hgrn_norm_g, fourier_w, fourier_b,
           w_out, post_mix_g, pre_xattn_g, xattn_wq, xattn_wk, xattn_wv, xattn_wo, post_xattn_g, pre_ffn_g,
           ffn_w_gate, ffn_w_up, ffn_w_down, post_ffn_g):
    batch, seq, d = x.shape
    depth = w_in.shape[0]
    mem_tokens = mem.shape[1]
    m = batch * seq
    ret_w = ret_norm_g.shape[1]
    hgrn_w = hgrn_norm_g.shape[1]
    ffn_h = ffn_w_gate.shape[2]
    tail = ffn_h % 1024
    main = ffn_h - tail

    h = x.reshape(m, d)
    mem_n = rmsnorm_rows(mem.reshape(batch * mem_tokens, d), mem_norm_g)
    a = rmsnorm_rows(h, pre_mix_g[0])
    for l in range(depth):
        proj = matmul(a, w_in, l, F32, 1024, 1024, 1024)
        y_ret = retention_mixer(proj, ret_norm_g[l], batch, seq, 0)
        y_hgrn = hgrn2_mixer(proj, hgrn_lb_logits, hgrn_norm_g[l], l, batch, seq, 4 * ret_w)
        y_fft = fourier_mixer(proj, fourier_w[l], fourier_b[l], batch, seq, 4 * ret_w + 5 * hgrn_w)
        mixed = matmul(jnp.concatenate([y_ret, y_hgrn, y_fft], axis=-1), w_out, l, F32, 1024, 1024, 1024)
        h, c = add_norm(h, mixed, post_mix_g[l], pre_xattn_g[l])
        q = matmul(c, xattn_wq, l, BF16, 1024, 1024, 1024)
        kx = matmul(mem_n, xattn_wk, l, BF16, 512, 1024, 1024)
        vx = matmul(mem_n, xattn_wv, l, BF16, 512, 1024, 1024)
        att = cross_attention_core(q, kx, vx, batch, seq, mem_tokens)
        xa = matmul(att, xattn_wo, l, F32, 1024, 1024, 1024)
        h, f = add_norm(h, xa, post_xattn_g[l], pre_ffn_g[l])
        hid = swiglu_matmul(f, ffn_w_gate, ffn_w_up, l, 1024, 1024, 1024, n_cols=main)
        hid_t = swiglu_matmul(f, ffn_w_gate[l:l + 1, :, main:], ffn_w_up[l:l + 1, :, main:], 0, 1024, tail, 1024)
        ff = matmul_split_k(hid, ffn_w_down, l, hid_t, ffn_w_down[l, main:], 1024, 1024, 1024)
        h, a = add_norm(h, ff, post_ffn_g[l], pre_mix_g[l + 1] if l + 1 < depth else None)
    return h.reshape(batch, seq, d)
```

```python
import functools
import math

import numpy as np
import jax
import jax.numpy as jnp
from jax import lax
from jax.experimental import pallas as pl
from jax.experimental.pallas import tpu as pltpu

F32 = jnp.float32
BF16 = jnp.bfloat16
HIGHEST = lax.Precision.HIGHEST

RET_HEAD_DIM = 256
HGRN_HEAD_DIM = 128
FOURIER_GROUP_DIM = 128
XATTN_HEADS = 4
XATTN_HEAD_DIM = 256
ROPE_BASE = 10000.0
RMS_EPS = 1e-6
GN_EPS = 1e-5

LANES = 128
MXU_TILE = 256
V7X_VMEM_BYTES = 64 * 1024 * 1024
VMEM_LIMIT = V7X_VMEM_BYTES - 8 * 1024 * 1024

RET_CHUNK = 256
HGRN_ROWS = 256
FFT_N1 = 64


def _cparams(sem):
    return pltpu.CompilerParams(dimension_semantics=sem, vmem_limit_bytes=VMEM_LIMIT)


def _tile(n, preferred):
    for t in range(min(preferred, n) // LANES * LANES, 0, -LANES):
        if n % t == 0:
            return t
    return n


def _dot(a, b):
    return jnp.dot(a, b, preferred_element_type=F32)


def _dot_nt(a, b):
    return lax.dot_general(a, b, (((1,), (1,)), ((), ())), preferred_element_type=F32)


def _dot_tn(a, b):
    return lax.dot_general(a, b, (((0,), (0,)), ((), ())), preferred_element_type=F32)


def _dot_exact(a, b):
    return jnp.dot(a, b, preferred_element_type=F32, precision=HIGHEST)


def _rms(x, g):
    ms = jnp.mean(x * x, axis=-1, keepdims=True)
    return x * lax.rsqrt(ms + RMS_EPS) * g


def _silu(x):
    return x * (1.0 / (1.0 + jnp.exp(-x)))


def _norm_kernel(x_ref, g_ref, o_ref):
    o_ref[...] = _rms(x_ref[...], g_ref[...]).astype(o_ref.dtype)


def rmsnorm_rows(x, g, br=256):
    m, d = x.shape
    out_dtype = BF16
    br = min(br, m)
    return pl.pallas_call(
        _norm_kernel,
        grid=(m // br,),
        in_specs=[pl.BlockSpec((br, d), lambda i: (i, 0)),
                  pl.BlockSpec((1, d), lambda i: (0, 0))],
        out_specs=pl.BlockSpec((br, d), lambda i: (i, 0)),
        out_shape=jax.ShapeDtypeStruct((m, d), out_dtype),
        compiler_params=_cparams(("parallel",)),
        name="rmsnorm_rows",
    )(x, g.reshape(1, d))


def _add_norm_kernel(h_ref, y_ref, gp_ref, gn_ref, ho_ref, a_ref):
    h = h_ref[...] + _rms(y_ref[...], gp_ref[...])
    ho_ref[...] = h
    a_ref[...] = _rms(h, gn_ref[...]).astype(a_ref.dtype)


def _add_norm_last_kernel(h_ref, y_ref, gp_ref, ho_ref):
    ho_ref[...] = h_ref[...] + _rms(y_ref[...], gp_ref[...])


def add_norm(h, y, g_post, g_next, br=256):
    m, d = h.shape
    row = pl.BlockSpec((br, d), lambda i: (i, 0))
    vec = pl.BlockSpec((1, d), lambda i: (0, 0))
    if g_next is None:
        return pl.pallas_call(
            _add_norm_last_kernel, grid=(m // br,),
            in_specs=[row, row, vec], out_specs=row,
            out_shape=jax.ShapeDtypeStruct((m, d), F32),
            compiler_params=_cparams(("parallel",)), name="add_norm_last",
        )(h, y, g_post.reshape(1, d)), None
    return pl.pallas_call(
        _add_norm_kernel, grid=(m // br,),
        in_specs=[row, row, vec, vec], out_specs=[row, row],
        out_shape=[jax.ShapeDtypeStruct((m, d), F32), jax.ShapeDtypeStruct((m, d), BF16)],
        compiler_params=_cparams(("parallel",)), name="add_norm",
    )(h, y, g_post.reshape(1, d), g_next.reshape(1, d))


def _mm_kernel(*refs, k_parts):
    a_refs, w_ref, o_ref = refs[:len(k_parts)], refs[len(k_parts)], refs[-1]
    acc, r0 = None, 0
    for a_ref, kp in zip(a_refs, k_parts):
        part = _dot(a_ref[...], w_ref[r0:r0 + kp, :].astype(BF16))
        acc = part if acc is None else acc + part
        r0 += kp
    o_ref[...] = acc.astype(o_ref.dtype)


def matmul(a_parts, w, layer, out_dtype, bm, bn):
    m = a_parts[0].shape[0]
    k_parts = tuple(a.shape[1] for a in a_parts)
    kdim, n = w.shape[1], w.shape[2]
    assert sum(k_parts) == kdim
    bm, bn = _tile(m, bm), _tile(n, bn)
    return pl.pallas_call(
        functools.partial(_mm_kernel, k_parts=k_parts),
        grid=(m // bm, n // bn),
        in_specs=[pl.BlockSpec((bm, kp), lambda i, j: (i, 0)) for kp in k_parts]
        + [pl.BlockSpec((None, kdim, bn), lambda i, j: (layer, 0, j))],
        out_specs=pl.BlockSpec((bm, bn), lambda i, j: (i, j)),
        out_shape=jax.ShapeDtypeStruct((m, n), out_dtype),
        compiler_params=_cparams(("parallel", "arbitrary")),
        name="matmul",
    )(*a_parts, w)


def _swiglu_kernel(a_ref, wg_ref, wu_ref, o_ref):
    a = a_ref[...]
    gate = _dot(a, wg_ref[...].astype(BF16))
    up = _dot(a, wu_ref[...].astype(BF16))
    o_ref[...] = (_silu(gate) * up).astype(o_ref.dtype)


def swiglu_matmul(a, wg, wu, layer, bm, bn):
    m, kdim = a.shape
    n = wg.shape[2]
    bm, bn = _tile(m, bm), _tile(n, bn)
    wspec = pl.BlockSpec((None, kdim, bn), lambda i, j: (layer, 0, j))
    return pl.pallas_call(
        _swiglu_kernel,
        grid=(m // bm, n // bn),
        in_specs=[pl.BlockSpec((bm, kdim), lambda i, j: (i, 0)), wspec, wspec],
        out_specs=pl.BlockSpec((bm, bn), lambda i, j: (i, j)),
        out_shape=jax.ShapeDtypeStruct((m, n), BF16),
        compiler_params=_cparams(("parallel", "arbitrary")),
        name="swiglu_matmul",
    )(a, wg, wu)


def _mm_ktiled_kernel(*refs, n_tail):
    a_ref, w_ref = refs[0], refs[1]
    t_refs, wt_ref, o_ref = refs[2:2 + n_tail], refs[2 + n_tail], refs[-1]
    k = pl.program_id(2)
    part = _dot(a_ref[...], w_ref[...].astype(BF16))

    @pl.when(k == 0)
    def _():
        tail = jnp.concatenate([t[...] for t in t_refs], axis=1)
        o_ref[...] = part + _dot(tail, wt_ref[...].astype(BF16))

    @pl.when(k > 0)
    def _():
        o_ref[...] += part


def matmul_ktiled(a, w, layer, bm, bn, bk):
    m, kdim = a.shape
    n = w.shape[2]
    bm, bn = _tile(m, bm), _tile(n, bn)
    nk = kdim // bk
    kmain = nk * bk
    ktail = kdim - kmain
    assert nk >= 1 and ktail > 0 and ktail % MXU_TILE == 0 and kmain % MXU_TILE == 0
    n_tail = ktail // MXU_TILE
    tb0 = kmain // MXU_TILE
    w_tail = w[layer, kmain:, :]
    tail_specs = [pl.BlockSpec((bm, MXU_TILE), functools.partial(lambda i, j, k, c: (i, c), c=tb0 + c))
                  for c in range(n_tail)]
    return pl.pallas_call(
        functools.partial(_mm_ktiled_kernel, n_tail=n_tail),
        grid=(m // bm, n // bn, nk),
        in_specs=[pl.BlockSpec((bm, bk), lambda i, j, k: (i, k)),
                  pl.BlockSpec((None, bk, bn), lambda i, j, k: (layer, k, j))]
        + tail_specs + [pl.BlockSpec((ktail, bn), lambda i, j, k: (0, j))],
        out_specs=pl.BlockSpec((bm, bn), lambda i, j, k: (i, j)),
        out_shape=jax.ShapeDtypeStruct((m, n), F32),
        compiler_params=_cparams(("parallel", "parallel", "arbitrary")),
        name="matmul_ktiled",
    )(a, w, *([a] * n_tail), w_tail)


def _ret_kernel(lg_ref, cos_ref, sin_ref, q_ref, k_ref, v_ref, g_ref, ng_ref, o_ref,
                sf_ref, sb_ref, ob_ref, *, nc, c):
    t = pl.program_id(2)
    half = RET_HEAD_DIM // 2
    lgf = lg_ref[0, 0:1, :]
    lgb = lg_ref[0, 1:2, :]
    cos = cos_ref[...]
    sin = sin_ref[...]

    def rot(x):
        x1, x2 = x[:, :half], x[:, half:]
        return jnp.concatenate([x1 * cos - x2 * sin, x1 * sin + x2 * cos], axis=-1)

    def lanes2(d):
        return jnp.concatenate([d, d], axis=-1)

    q = rot(q_ref[...]) * (RET_HEAD_DIM ** -0.5)
    k = rot(k_ref[...])
    vb = v_ref[...].astype(BF16)
    i_row = lax.broadcasted_iota(jnp.int32, (c, half), 0).astype(F32)

    @pl.when(t == 0)
    def _():
        sf_ref[...] = jnp.zeros_like(sf_ref)
        sb_ref[...] = jnp.zeros_like(sb_ref)

    @pl.when(t < nc)
    def _():
        ci = nc - 1 - t
        sb = sb_ref[...]
        qd = q * lanes2(jnp.exp((c - i_row) * lgb))
        ob_ref[ci] = _dot(qd.astype(BF16), sb.astype(BF16))
        kd = k * lanes2(jnp.exp(i_row * lgb))
        sb_ref[...] = jnp.exp(c * lgb[:, :1]) * sb + _dot_tn(kd.astype(BF16), vb)

    @pl.when(t >= nc)
    def _():
        ci = t - nc
        sf = sf_ref[...]
        ii = lax.broadcasted_iota(jnp.int32, (c, c), 0)
        jj = lax.broadcasted_iota(jnp.int32, (c, c), 1)
        rel = (ii - jj).astype(F32)
        decay = jnp.exp(jnp.where(rel >= 0, rel * lgf[:, :1], -rel * lgb[:, :1]))
        scores = _dot_nt(q.astype(BF16), k.astype(BF16)) * decay
        qd = q * lanes2(jnp.exp((i_row + 1.0) * lgf))
        o = (_dot(scores.astype(BF16), vb) + _dot(qd.astype(BF16), sf.astype(BF16)) + ob_ref[ci])
        kd = k * lanes2(jnp.exp((c - 1.0 - i_row) * lgf))
        sf_ref[...] = jnp.exp(c * lgf[:, :1]) * sf + _dot_tn(kd.astype(BF16), vb)
        mu = jnp.mean(o, axis=-1, keepdims=True)
        oc = o - mu
        var = jnp.mean(oc * oc, axis=-1, keepdims=True)
        y = oc * lax.rsqrt(var + GN_EPS) * ng_ref[...]
        o_ref[...] = (_silu(g_ref[...]) * y).astype(o_ref.dtype)


def retention_mixer(proj, norm_g, batch, seq, col0):
    m = proj.shape[0]
    width = norm_g.shape[0]
    heads = width // RET_HEAD_DIM
    c = min(RET_CHUNK, seq)
    nc = seq // c
    half = RET_HEAD_DIM // 2
    cb0 = col0 // RET_HEAD_DIM

    hidx = jnp.arange(heads, dtype=F32)
    gamma = 1.0 - jnp.power(2.0, -5.0 - hidx)
    lg = jnp.log(gamma)
    lg2 = jnp.broadcast_to(jnp.stack([lg, lg[::-1]], axis=1)[:, :, None], (heads, 2, LANES))
    inv = jnp.power(ROPE_BASE, -jnp.arange(half, dtype=F32) / half)
    ang = jnp.arange(seq, dtype=F32)[:, None] * inv[None, :]
    cos, sin = jnp.cos(ang), jnp.sin(ang)

    def chunk(t):
        return jnp.where(t < nc, nc - 1 - t, t - nc)

    def fchunk(t):
        return jnp.maximum(t - nc, 0)

    def qkv_spec(j):
        return pl.BlockSpec((c, RET_HEAD_DIM), lambda b, h, t: (b * nc + chunk(t), cb0 + j * heads + h))

    return pl.pallas_call(
        functools.partial(_ret_kernel, nc=nc, c=c),
        grid=(batch, heads, 2 * nc),
        in_specs=[pl.BlockSpec((1, 2, LANES), lambda b, h, t: (h, 0, 0)),
                  pl.BlockSpec((c, half), lambda b, h, t: (chunk(t), 0)),
                  pl.BlockSpec((c, half), lambda b, h, t: (chunk(t), 0)),
                  qkv_spec(0), qkv_spec(1), qkv_spec(2),
                  pl.BlockSpec((c, RET_HEAD_DIM), lambda b, h, t: (b * nc + fchunk(t), cb0 + 3 * heads + h)),
                  pl.BlockSpec((1, RET_HEAD_DIM), lambda b, h, t: (0, h))],
        out_specs=pl.BlockSpec((c, RET_HEAD_DIM), lambda b, h, t: (b * nc + fchunk(t), h)),
        out_shape=jax.ShapeDtypeStruct((m, width), BF16),
        scratch_shapes=[pltpu.VMEM((RET_HEAD_DIM, RET_HEAD_DIM), F32),
                        pltpu.VMEM((RET_HEAD_DIM, RET_HEAD_DIM), F32),
                        pltpu.VMEM((nc, c, RET_HEAD_DIM), F32)],
        compiler_params=_cparams(("parallel", "parallel", "arbitrary")),
        name="retention",
    )(lg2, cos, sin, proj, proj, proj, proj, norm_g.reshape(1, width))


def _hgrn_kernel(lbl_ref, lvl_ref, q_ref, z_ref, v_ref, g_ref, ng_ref, o_ref,
                 st_ref, ob_ref, *, nb, rows, layer, depth):
    t = pl.program_id(2)
    d = HGRN_HEAD_DIM
    nlev = rows.bit_length() - 1

    def lower_bound(dirn):
        x = [lbl_ref[2 * l + dirn: 2 * l + dirn + 1, :] for l in range(depth)]
        mx = functools.reduce(jnp.maximum, x)
        e = [jnp.exp(xi - mx) for xi in x]
        tot = functools.reduce(lambda a, b: a + b, e)
        p = [ei / tot for ei in e]
        return functools.reduce(lambda a, b: a + b, p[:layer + 1]) - p[0]

    row = lax.broadcasted_iota(jnp.int32, (rows, d), 0)
    r_i = lax.broadcasted_iota(jnp.int32, (rows, rows), 0)
    c_i = lax.broadcasted_iota(jnp.int32, (rows, rows), 1)

    def split3(x):
        hi = x.astype(BF16)
        r1 = x - hi.astype(F32)
        mid = r1.astype(BF16)
        lo = (r1 - mid.astype(F32)).astype(BF16)
        return jnp.concatenate([hi, mid, lo], axis=1)

    def group_rows(x, size, pick):
        parts = [jnp.broadcast_to(x[base + pick:base + pick + 1, :], (size, d)) for base in range(0, rows, size)]
        return parts[0] if len(parts) == 1 else jnp.concatenate(parts, axis=0)

    def run(rev):
        lb = lower_bound(1 if rev else 0)
        f = lb + (1.0 - lb) * (1.0 / (1.0 + jnp.exp(-z_ref[...])))
        kk = 1.0 - f
        lf = jnp.log(f)
        q = q_ref[...] * (d ** -0.5)
        vb = v_ref[...].astype(BF16)

        tri = ((c_i >= r_i) if rev else (c_i <= r_i)).astype(BF16)
        parts = _dot(tri, split3(lf))
        cum = parts[:, :d] + parts[:, d:2 * d] + parts[:, 2 * d:]
        edge = cum[0:1, :] if rev else cum[rows - 1:rows, :]

        st = st_ref[...]
        o = _dot_nt((q * jnp.exp(cum)).astype(BF16), st.astype(BF16))
        kd = kk * jnp.exp(edge - cum)
        st_ref[...] = st * jnp.exp(edge) + _dot_tn(vb, kd.astype(BF16))

        lvl = lvl_ref[1 if rev else 0]
        a = jnp.where(lvl == nlev, jnp.sum(q * kk, axis=-1, keepdims=True), 0.0)
        lf_up = pltpu.roll(lf, rows - 1, 0)
        lf_dn = pltpu.roll(lf, 1, 0)
        for lev in range(nlev):
            m = 1 << lev
            upper = (row & m) != 0
            target = jnp.logical_not(upper) if rev else upper
            if m == 1:
                dlt = jnp.where(upper, 0.0, lf) if rev else jnp.where(upper, lf, 0.0)
            elif m == 2:
                r4 = row & 3
                if rev:
                    dlt = jnp.where(r4 == 0, lf + lf_up, jnp.where(r4 == 1, lf, jnp.where(r4 == 2, 0.0, lf_dn)))
                else:
                    dlt = jnp.where(r4 == 0, lf_up, jnp.where(r4 == 1, 0.0, jnp.where(r4 == 2, lf, lf + lf_dn)))
            else:
                bound = group_rows(cum, 2 * m, m if rev else m - 1)
                dlt = jnp.where(target, cum - bound, bound - cum)
            y = (jnp.where(target, q, kk) * jnp.exp(dlt)).astype(BF16)
            a = jnp.where(lvl == lev, _dot_nt(y, y), a)
        return o + _dot(a.astype(BF16), vb)

    @pl.when(t == 0)
    def _():
        st_ref[...] = jnp.zeros_like(st_ref)

    @pl.when(t < nb)
    def _():
        ob_ref[nb - 1 - t] = run(True)

    @pl.when(t == nb)
    def _():
        st_ref[...] = jnp.zeros_like(st_ref)

    @pl.when(t >= nb)
    def _():
        o = run(False) + ob_ref[t - nb]
        o = o * lax.rsqrt(jnp.mean(o * o, axis=-1, keepdims=True) + RMS_EPS)
        o_ref[...] = (_silu(g_ref[...]) * (o * ng_ref[...])).astype(o_ref.dtype)


def hgrn2_mixer(proj, lb_logits, norm_g, layer, batch, seq, col0):
    m = proj.shape[0]
    width = norm_g.shape[0]
    d = HGRN_HEAD_DIM
    heads = width // d
    depth = lb_logits.shape[0]
    rows = min(HGRN_ROWS, seq)
    nb = seq // rows
    cb0 = col0 // d

    def blk(t):
        return jnp.where(t < nb, nb - 1 - t, t - nb)

    def fblk(t):
        return jnp.maximum(t - nb, 0)

    def spec(j):
        return pl.BlockSpec((rows, d), lambda b, h, t: (b * nb + blk(t), cb0 + j * heads + h))

    zspec = pl.BlockSpec((rows, d),
                         lambda b, h, t: (b * nb + blk(t), cb0 + jnp.where(t < nb, 2, 1) * heads + h))
    assert rows & (rows - 1) == 0
    nlev = rows.bit_length() - 1
    idx = np.arange(rows)
    x = idx[:, None] ^ idx[None, :]
    lev = np.where(x > 0, np.floor(np.log2(np.maximum(x, 1))), nlev).astype(np.int32)
    lvl = jnp.asarray(np.stack([np.where(idx[:, None] >= idx[None, :], lev, -1),
                                np.where(idx[:, None] <= idx[None, :], lev, -1)]), jnp.int32)
    return pl.pallas_call(
        functools.partial(_hgrn_kernel, nb=nb, rows=rows, layer=layer, depth=depth),
        grid=(batch, heads, 2 * nb),
        in_specs=[pl.BlockSpec((2 * depth, d), lambda b, h, t: (0, h)),
                  pl.BlockSpec((2, rows, rows), lambda b, h, t: (0, 0, 0)),
                  spec(0), zspec, spec(3),
                  pl.BlockSpec((rows, d), lambda b, h, t: (b * nb + fblk(t), cb0 + 4 * heads + h)),
                  pl.BlockSpec((1, d), lambda b, h, t: (0, h))],
        out_specs=pl.BlockSpec((rows, d), lambda b, h, t: (b * nb + fblk(t), h)),
        out_shape=jax.ShapeDtypeStruct((m, width), BF16),
        scratch_shapes=[pltpu.VMEM((d, d), F32),
                        pltpu.VMEM((nb, rows, d), F32)],
        compiler_params=_cparams(("parallel", "parallel", "arbitrary")),
        name="hgrn2",
    )(lb_logits.reshape(2 * depth, width), lvl, proj, proj, proj, proj, norm_g.reshape(1, width))


def _dft_cos_sin(n):
    idx = np.arange(n)
    ang = 2.0 * np.pi * ((idx[:, None] * idx[None, :]) % n) / n
    return np.cos(ang), np.sin(ang)


def _fft_chan_kernel(z_ref, m_ref, vr_ref, vi_ref, *, groups):
    gd = FOURIER_GROUP_DIM
    mat = m_ref[...]
    for g in range(groups):
        pq = _dot_exact(z_ref[:, g * gd:(g + 1) * gd], mat)
        vr_ref[:, g * gd:(g + 1) * gd] = pq[:, :gd]
        vi_ref[:, g * gd:(g + 1) * gd] = pq[:, gd:]


def _fft_stage1_kernel(vr_ref, vi_ref, m_ref, twc_ref, tws_ref, ur_ref, ui_ref, *, n1, reps):
    x = jnp.concatenate([vr_ref[...], vi_ref[...]], axis=0)
    tt = _dot_exact(m_ref[...], x)
    tr, ti = tt[:n1], tt[n1:]
    cw = jnp.tile(twc_ref[...], (1, reps))
    sw = jnp.tile(tws_ref[...], (1, reps))
    ur_ref[...] = tr * cw + ti * sw
    ui_ref[...] = ti * cw - tr * sw


def _fft_stage2_kernel(ur_ref, ui_ref, m_ref, w_ref, b_ref, o_ref, *, groups, scale):
    gd = FOURIER_GROUP_DIM
    x = jnp.concatenate([ur_ref[...], ui_ref[...]], axis=0)
    spec = _dot_exact(m_ref[...], x) * scale
    for g in range(groups):
        y = _dot(spec[:, g * gd:(g + 1) * gd].astype(BF16), w_ref[g].astype(BF16))
        o_ref[:, g * gd:(g + 1) * gd] = (y + b_ref[:, g * gd:(g + 1) * gd]).astype(o_ref.dtype)


def fourier_mixer(proj, w, bias, batch, seq, col0):
    m = proj.shape[0]
    groups, gd, _ = w.shape
    width = groups * gd
    n1 = min(FFT_N1, seq)
    n2 = seq // n1
    br = min(512, m)

    cc, sc = _dft_cos_sin(gd)
    chan = jnp.asarray(np.concatenate([cc, -sc], axis=1), F32)
    vr, vi = pl.pallas_call(
        functools.partial(_fft_chan_kernel, groups=groups),
        grid=(m // br,),
        in_specs=[pl.BlockSpec((br, width), lambda i: (i, col0 // width)),
                  pl.BlockSpec((gd, 2 * gd), lambda i: (0, 0))],
        out_specs=[pl.BlockSpec((br, width), lambda i: (i, 0))] * 2,
        out_shape=[jax.ShapeDtypeStruct((m, width), F32)] * 2,
        compiler_params=_cparams(("parallel",)),
        name="fft_channels",
    )(proj, chan)

    c1, s1 = _dft_cos_sin(n1)
    m1 = jnp.asarray(np.block([[c1, s1], [-s1, c1]]), F32)
    k1 = np.arange(n1)
    bb = np.arange(n2)
    tw = 2.0 * np.pi * ((bb[:, None] * k1[None, :]) % seq) / seq
    twc = jnp.broadcast_to(jnp.asarray(np.cos(tw), F32)[:, :, None], (n2, n1, LANES))
    tws = jnp.broadcast_to(jnp.asarray(np.sin(tw), F32)[:, :, None], (n2, n1, LANES))
    vspec = pl.BlockSpec((None, n1, width), lambda b, j: (b, 0, j))
    tspec = pl.BlockSpec((None, n1, LANES), lambda b, j: (j, 0, 0))
    ur, ui = pl.pallas_call(
        functools.partial(_fft_stage1_kernel, n1=n1, reps=width // LANES),
        grid=(batch, n2),
        in_specs=[vspec, vspec, pl.BlockSpec((2 * n1, 2 * n1), lambda b, j: (0, 0)), tspec, tspec],
        out_specs=[vspec, vspec],
        out_shape=[jax.ShapeDtypeStruct((batch, n1, n2 * width), F32)] * 2,
        compiler_params=_cparams(("parallel", "parallel")),
        name="fft_stage1",
    )(vr.reshape(batch, n1, n2 * width), vi.reshape(batch, n1, n2 * width), m1, twc, tws)

    c2, s2 = _dft_cos_sin(n2)
    m2 = jnp.asarray(np.concatenate([c2, s2], axis=1), F32)
    uspec = pl.BlockSpec((None, n2, width), lambda b, j: (b, j, 0))
    out = pl.pallas_call(
        functools.partial(_fft_stage2_kernel, groups=groups, scale=1.0 / math.sqrt(seq * gd)),
        grid=(batch, n1),
        in_specs=[uspec, uspec, pl.BlockSpec((n2, 2 * n2), lambda b, j: (0, 0)),
                  pl.BlockSpec((groups, gd, gd), lambda b, j: (0, 0, 0)),
                  pl.BlockSpec((1, width), lambda b, j: (0, 0))],
        out_specs=pl.BlockSpec((None, n2, width), lambda b, j: (b, 0, j)),
        out_shape=jax.ShapeDtypeStruct((batch, n2, n1 * width), BF16),
        compiler_params=_cparams(("parallel", "parallel")),
        name="fft_stage2",
    )(ur.reshape(batch, n1 * n2, width), ui.reshape(batch, n1 * n2, width), m2, w, bias.reshape(1, width))
    return out.reshape(m, width)


def _xattn_kernel(q_ref, k_ref, v_ref, o_ref):
    s = _dot_nt(q_ref[...], k_ref[...]) * (XATTN_HEAD_DIM ** -0.5)
    p = jnp.exp(s - jnp.max(s, axis=-1, keepdims=True))
    denom = jnp.sum(p, axis=-1, keepdims=True)
    o_ref[...] = (_dot(p.astype(BF16), v_ref[...]) / denom).astype(o_ref.dtype)


def cross_attention_core(q, k, v, batch, seq, mem_tokens, ts=512):
    m, width = q.shape
    hd = XATTN_HEAD_DIM
    heads = width // hd
    ts = min(ts, seq)
    nt = seq // ts
    kv = pl.BlockSpec((mem_tokens, hd), lambda b, h, i: (b, h))
    qs = pl.BlockSpec((ts, hd), lambda b, h, i: (b * nt + i, h))
    return pl.pallas_call(
        _xattn_kernel,
        grid=(batch, heads, nt),
        in_specs=[qs, kv, kv],
        out_specs=qs,
        out_shape=jax.ShapeDtypeStruct((m, width), BF16),
        compiler_params=_cparams(("parallel", "parallel", "parallel")),
        name="xattn",
    )(q, k, v)


def kernel(x, mem, mem_norm_g, pre_mix_g, w_in, ret_norm_g, hgrn_lb_logits, hgrn_norm_g, fourier_w, fourier_b,
           w_out, post_mix_g, pre_xattn_g, xattn_wq, xattn_wk, xattn_wv, xattn_wo, post_xattn_g, pre_ffn_g,
           ffn_w_gate, ffn_w_up, ffn_w_down, post_ffn_g):
    batch, seq, d = x.shape
    depth = w_in.shape[0]
    mem_tokens = mem.shape[1]
    m = batch * seq
    ret_w = ret_norm_g.shape[1]
    hgrn_w = hgrn_norm_g.shape[1]

    h = x.reshape(m, d)
    mem_n = rmsnorm_rows(mem.reshape(batch * mem_tokens, d), mem_norm_g)
    a = rmsnorm_rows(h, pre_mix_g[0])
    for l in range(depth):
        proj = matmul([a], w_in, l, F32, 1024, 512)
        y_ret = retention_mixer(proj, ret_norm_g[l], batch, seq, 0)
        y_hgrn = hgrn2_mixer(proj, hgrn_lb_logits, hgrn_norm_g[l], l, batch, seq, 4 * ret_w)
        y_fft = fourier_mixer(proj, fourier_w[l], fourier_b[l], batch, seq, 4 * ret_w + 5 * hgrn_w)
        mixed = matmul([y_ret, y_hgrn, y_fft], w_out, l, F32, 1024, 512)
        h, c = add_norm(h, mixed, post_mix_g[l], pre_xattn_g[l])
        q = matmul([c], xattn_wq, l, BF16, 1024, 512)
        kx = matmul([mem_n], xattn_wk, l, BF16, 512, 512)
        vx = matmul([mem_n], xattn_wv, l, BF16, 512, 512)
        att = cross_attention_core(q, kx, vx, batch, seq, mem_tokens)
        xa = matmul([att], xattn_wo, l, F32, 1024, 1024)
        h, f = add_norm(h, xa, post_xattn_g[l], pre_ffn_g[l])
        hid = swiglu_matmul(f, ffn_w_gate, ffn_w_up, l, 1024, 256)
        ff = matmul_ktiled(hid, ffn_w_down, l, 1024, 1024, 2048)
        h, a = add_norm(h, ff, post_ffn_g[l], pre_mix_g[l + 1] if l + 1 < depth else None)
    return h.reshape(batch, seq, d)
```

```python
import functools
import math

import numpy as np
import jax
import jax.numpy as jnp
from jax import lax
from jax.experimental import pallas as pl
from jax.experimental.pallas import tpu as pltpu

F32 = jnp.float32
BF16 = jnp.bfloat16
HIGHEST = lax.Precision.HIGHEST

RET_HEAD_DIM = 256
HGRN_HEAD_DIM = 128
FOURIER_GROUP_DIM = 128
XATTN_HEADS = 4
XATTN_HEAD_DIM = 256
ROPE_BASE = 10000.0
RMS_EPS = 1e-6
GN_EPS = 1e-5

LANES = 128
MXU_TILE = 256
V7X_VMEM_BYTES = 64 * 1024 * 1024
VMEM_LIMIT = V7X_VMEM_BYTES - 8 * 1024 * 1024

RET_CHUNK = MXU_TILE
RET_ROWS = 1024
HGRN_ROWS = 256
FFT_N1 = 64
FFT_SUB1 = 8
FFT_SUB2 = 16


def _cparams(sem):
    return pltpu.CompilerParams(dimension_semantics=sem, vmem_limit_bytes=VMEM_LIMIT)


def _tile(n, preferred):
    for t in range(min(preferred, n) // LANES * LANES, 0, -LANES):
        if n % t == 0:
            return t
    return n


def _dot(a, b):
    return jnp.dot(a, b, preferred_element_type=F32)


def _dot_nt(a, b):
    return lax.dot_general(a, b, (((1,), (1,)), ((), ())), preferred_element_type=F32)


def _dot_tn(a, b):
    return lax.dot_general(a, b, (((0,), (0,)), ((), ())), preferred_element_type=F32)


def _dot_exact(a, b):
    return jnp.dot(a, b, preferred_element_type=F32, precision=HIGHEST)


def _rms(x, g):
    ms = jnp.mean(x * x, axis=-1, keepdims=True)
    return x * lax.rsqrt(ms + RMS_EPS) * g


def _silu(x):
    return x * (1.0 / (1.0 + jnp.exp(-x)))


def _norm_kernel(x_ref, g_ref, o_ref):
    o_ref[...] = _rms(x_ref[...], g_ref[...]).astype(o_ref.dtype)


def rmsnorm_rows(x, g, br=256):
    m, d = x.shape
    out_dtype = BF16
    br = min(br, m)
    return pl.pallas_call(
        _norm_kernel,
        grid=(m // br,),
        in_specs=[pl.BlockSpec((br, d), lambda i: (i, 0)),
                  pl.BlockSpec((1, d), lambda i: (0, 0))],
        out_specs=pl.BlockSpec((br, d), lambda i: (i, 0)),
        out_shape=jax.ShapeDtypeStruct((m, d), out_dtype),
        compiler_params=_cparams(("parallel",)),
        name="rmsnorm_rows",
    )(x, g.reshape(1, d))


def _add_norm_kernel(h_ref, y_ref, gp_ref, gn_ref, ho_ref, a_ref):
    h = h_ref[...] + _rms(y_ref[...], gp_ref[...])
    ho_ref[...] = h
    a_ref[...] = _rms(h, gn_ref[...]).astype(a_ref.dtype)


def _add_norm_last_kernel(h_ref, y_ref, gp_ref, ho_ref):
    ho_ref[...] = h_ref[...] + _rms(y_ref[...], gp_ref[...])


def add_norm(h, y, g_post, g_next, br=256):
    m, d = h.shape
    row = pl.BlockSpec((br, d), lambda i: (i, 0))
    vec = pl.BlockSpec((1, d), lambda i: (0, 0))
    if g_next is None:
        return pl.pallas_call(
            _add_norm_last_kernel, grid=(m // br,),
            in_specs=[row, row, vec], out_specs=row,
            out_shape=jax.ShapeDtypeStruct((m, d), F32),
            compiler_params=_cparams(("parallel",)), name="add_norm_last",
        )(h, y, g_post.reshape(1, d)), None
    return pl.pallas_call(
        _add_norm_kernel, grid=(m // br,),
        in_specs=[row, row, vec, vec], out_specs=[row, row],
        out_shape=[jax.ShapeDtypeStruct((m, d), F32), jax.ShapeDtypeStruct((m, d), BF16)],
        compiler_params=_cparams(("parallel",)), name="add_norm",
    )(h, y, g_post.reshape(1, d), g_next.reshape(1, d))


def _mm_kernel(*refs, k_parts):
    a_refs, w_ref, o_ref = refs[:len(k_parts)], refs[len(k_parts)], refs[-1]
    acc, r0 = None, 0
    for a_ref, kp in zip(a_refs, k_parts):
        part = _dot(a_ref[...], w_ref[r0:r0 + kp, :].astype(BF16))
        acc = part if acc is None else acc + part
        r0 += kp
    o_ref[...] = acc.astype(o_ref.dtype)


def matmul(a_parts, w, layer, out_dtype, bm, bn):
    m = a_parts[0].shape[0]
    k_parts = tuple(a.shape[1] for a in a_parts)
    kdim, n = w.shape[1], w.shape[2]
    assert sum(k_parts) == kdim
    bm, bn = _tile(m, bm), _tile(n, bn)
    return pl.pallas_call(
        functools.partial(_mm_kernel, k_parts=k_parts),
        grid=(m // bm, n // bn),
        in_specs=[pl.BlockSpec((bm, kp), lambda i, j: (i, 0)) for kp in k_parts]
        + [pl.BlockSpec((None, kdim, bn), lambda i, j: (layer, 0, j))],
        out_specs=pl.BlockSpec((bm, bn), lambda i, j: (i, j)),
        out_shape=jax.ShapeDtypeStruct((m, n), out_dtype),
        compiler_params=_cparams(("parallel", "arbitrary")),
        name="matmul",
    )(*a_parts, w)


def _swiglu_kernel(a_ref, wg_ref, wu_ref, o_ref):
    a = a_ref[...]
    gate = _dot(a, wg_ref[...].astype(BF16))
    up = _dot(a, wu_ref[...].astype(BF16))
    o_ref[...] = (_silu(gate) * up).astype(o_ref.dtype)


def swiglu_matmul(a, wg, wu, layer, bm, bn):
    m, kdim = a.shape
    n = wg.shape[2]
    bm, bn = _tile(m, bm), _tile(n, bn)
    wspec = pl.BlockSpec((None, kdim, bn), lambda i, j: (layer, 0, j))
    return pl.pallas_call(
        _swiglu_kernel,
        grid=(m // bm, n // bn),
        in_specs=[pl.BlockSpec((bm, kdim), lambda i, j: (i, 0)), wspec, wspec],
        out_specs=pl.BlockSpec((bm, bn), lambda i, j: (i, j)),
        out_shape=jax.ShapeDtypeStruct((m, n), BF16),
        compiler_params=_cparams(("parallel", "arbitrary")),
        name="swiglu_matmul",
    )(a, wg, wu)


def _mm_ktiled_kernel(*refs, n_tail):
    a_ref, w_ref = refs[0], refs[1]
    t_refs, wt_ref, o_ref = refs[2:2 + n_tail], refs[2 + n_tail], refs[-1]
    k = pl.program_id(2)
    part = _dot(a_ref[...], w_ref[...].astype(BF16))

    @pl.when(k == 0)
    def _():
        tail = jnp.concatenate([t[...] for t in t_refs], axis=1)
        o_ref[...] = part + _dot(tail, wt_ref[...].astype(BF16))

    @pl.when(k > 0)
    def _():
        o_ref[...] += part


def matmul_ktiled(a, w, layer, bm, bn, bk):
    m, kdim = a.shape
    n = w.shape[2]
    bm, bn = _tile(m, bm), _tile(n, bn)
    nk = kdim // bk
    kmain = nk * bk
    ktail = kdim - kmain
    assert nk >= 1 and ktail > 0 and ktail % MXU_TILE == 0 and kmain % MXU_TILE == 0
    n_tail = ktail // MXU_TILE
    tb0 = kmain // MXU_TILE
    w_tail = w[layer, kmain:, :]
    tail_specs = [pl.BlockSpec((bm, MXU_TILE), functools.partial(lambda i, j, k, c: (i, c), c=tb0 + c))
                  for c in range(n_tail)]
    return pl.pallas_call(
        functools.partial(_mm_ktiled_kernel, n_tail=n_tail),
        grid=(m // bm, n // bn, nk),
        in_specs=[pl.BlockSpec((bm, bk), lambda i, j, k: (i, k)),
                  pl.BlockSpec((None, bk, bn), lambda i, j, k: (layer, k, j))]
        + tail_specs + [pl.BlockSpec((ktail, bn), lambda i, j, k: (0, j))],
        out_specs=pl.BlockSpec((bm, bn), lambda i, j, k: (i, j)),
        out_shape=jax.ShapeDtypeStruct((m, n), F32),
        compiler_params=_cparams(("parallel", "parallel", "arbitrary")),
        name="matmul_ktiled",
    )(a, w, *([a] * n_tail), w_tail)


def _ret_kernel(lg_ref, cos_ref, sin_ref, q_ref, k_ref, v_ref, g_ref, ng_ref, o_ref,
                sf_ref, sb_ref, ob_ref, qr_ref, kr_ref, dec_ref, vec_ref, *, nb, rows, c):
    t = pl.program_id(2)
    half = RET_HEAD_DIM // 2
    nch = rows // c

    @pl.when(t == 0)
    def _():
        lgf = lg_ref[0, 0:1, :]
        lgb = lg_ref[0, 1:2, :]
        i_row = lax.broadcasted_iota(jnp.int32, (c, half), 0).astype(F32)
        rel = (lax.broadcasted_iota(jnp.int32, (c, c), 0) - lax.broadcasted_iota(jnp.int32, (c, c), 1)).astype(F32)
        dec_ref[...] = jnp.exp(jnp.where(rel >= 0, rel * lgf[:, :1], -rel * lgb[:, :1]))
        vec_ref[0] = jnp.exp((i_row + 1.0) * lgf)
        vec_ref[1] = jnp.exp((c - 1.0 - i_row) * lgf)
        vec_ref[2] = jnp.exp((c - i_row) * lgb)
        vec_ref[3] = jnp.exp(i_row * lgb)
        vec_ref[4] = jnp.exp(c * lgf) + jnp.zeros((c, half), F32)
        vec_ref[5] = jnp.exp(c * lgb) + jnp.zeros((c, half), F32)
        sf_ref[...] = jnp.zeros_like(sf_ref)
        sb_ref[...] = jnp.zeros_like(sb_ref)

    def lanes2(d):
        return jnp.concatenate([d, d], axis=-1)

    @pl.when(t < nb)
    def _():
        base = (nb - 1 - t) * rows
        for ci in range(nch - 1, -1, -1):
            r0 = ci * c
            g0 = pl.multiple_of(base + r0, c)
            cos = cos_ref[r0:r0 + c, :]
            sin = sin_ref[r0:r0 + c, :]

            def rot(x):
                x1, x2 = x[:, :half], x[:, half:]
                return jnp.concatenate([x1 * cos - x2 * sin, x1 * sin + x2 * cos], axis=-1)

            q = rot(q_ref[r0:r0 + c, :]) * (RET_HEAD_DIM ** -0.5)
            k = rot(k_ref[r0:r0 + c, :])
            qr_ref[pl.ds(g0, c), :] = q
            kr_ref[pl.ds(g0, c), :] = k
            vb = v_ref[r0:r0 + c, :].astype(BF16)
            sb = sb_ref[...]
            ob_ref[pl.ds(g0, c), :] = _dot((q * lanes2(vec_ref[2])).astype(BF16), sb.astype(BF16))
            kd = k * lanes2(vec_ref[3])
            sb_ref[...] = vec_ref[5][0:1, 0:1] * sb + _dot_tn(kd.astype(BF16), vb)

    @pl.when(t >= nb)
    def _():
        base = (t - nb) * rows
        for ci in range(nch):
            r0 = ci * c
            g0 = pl.multiple_of(base + r0, c)
            q = qr_ref[pl.ds(g0, c), :]
            k = kr_ref[pl.ds(g0, c), :]
            vb = v_ref[r0:r0 + c, :].astype(BF16)
            sf = sf_ref[...]
            scores = _dot_nt(q.astype(BF16), k.astype(BF16)) * dec_ref[...]
            o = (_dot(scores.astype(BF16), vb) + _dot((q * lanes2(vec_ref[0])).astype(BF16), sf.astype(BF16))
                 + ob_ref[pl.ds(g0, c), :])
            kd = k * lanes2(vec_ref[1])
            sf_ref[...] = vec_ref[4][0:1, 0:1] * sf + _dot_tn(kd.astype(BF16), vb)
            mu = jnp.mean(o, axis=-1, keepdims=True)
            oc = o - mu
            var = jnp.mean(oc * oc, axis=-1, keepdims=True)
            y = oc * lax.rsqrt(var + GN_EPS) * ng_ref[...]
            o_ref[r0:r0 + c, :] = (_silu(g_ref[r0:r0 + c, :]) * y).astype(o_ref.dtype)


def retention_mixer(proj, norm_g, batch, seq, col0):
    m = proj.shape[0]
    width = norm_g.shape[0]
    heads = width // RET_HEAD_DIM
    c = min(RET_CHUNK, seq)
    rows = min(RET_ROWS, seq)
    nb = seq // rows
    half = RET_HEAD_DIM // 2
    cb0 = col0 // RET_HEAD_DIM

    hidx = jnp.arange(heads, dtype=F32)
    gamma = 1.0 - jnp.power(2.0, -5.0 - hidx)
    lg = jnp.log(gamma)
    lg2 = jnp.broadcast_to(jnp.stack([lg, lg[::-1]], axis=1)[:, :, None], (heads, 2, LANES))
    inv = jnp.power(ROPE_BASE, -jnp.arange(half, dtype=F32) / half)
    ang = jnp.arange(seq, dtype=F32)[:, None] * inv[None, :]
    cos, sin = jnp.cos(ang), jnp.sin(ang)

    def bblk(t):
        return jnp.maximum(nb - 1 - t, 0)

    def fblk(t):
        return jnp.maximum(t - nb, 0)

    def blk(t):
        return jnp.where(t < nb, nb - 1 - t, t - nb)

    def spec(j, which):
        return pl.BlockSpec((rows, RET_HEAD_DIM), lambda b, h, t: (b * nb + which(t), cb0 + j * heads + h))

    return pl.pallas_call(
        functools.partial(_ret_kernel, nb=nb, rows=rows, c=c),
        grid=(batch, heads, 2 * nb),
        in_specs=[pl.BlockSpec((1, 2, LANES), lambda b, h, t: (h, 0, 0)),
                  pl.BlockSpec((rows, half), lambda b, h, t: (bblk(t), 0)),
                  pl.BlockSpec((rows, half), lambda b, h, t: (bblk(t), 0)),
                  spec(0, bblk), spec(1, bblk), spec(2, blk), spec(3, fblk),
                  pl.BlockSpec((1, RET_HEAD_DIM), lambda b, h, t: (0, h))],
        out_specs=pl.BlockSpec((rows, RET_HEAD_DIM), lambda b, h, t: (b * nb + fblk(t), h)),
        out_shape=jax.ShapeDtypeStruct((m, width), BF16),
        scratch_shapes=[pltpu.VMEM((RET_HEAD_DIM, RET_HEAD_DIM), F32),
                        pltpu.VMEM((RET_HEAD_DIM, RET_HEAD_DIM), F32),
                        pltpu.VMEM((seq, RET_HEAD_DIM), F32),
                        pltpu.VMEM((seq, RET_HEAD_DIM), F32),
                        pltpu.VMEM((seq, RET_HEAD_DIM), F32),
                        pltpu.VMEM((c, c), F32),
                        pltpu.VMEM((6, c, half), F32)],
        compiler_params=_cparams(("parallel", "parallel", "arbitrary")),
        name="retention",
    )(lg2, cos, sin, proj, proj, proj, proj, norm_g.reshape(1, width))


def _hgrn_kernel(lbl_ref, lvl_ref, q_ref, z_ref, v_ref, g_ref, ng_ref, o_ref,
                 st_ref, ob_ref, *, nb, rows, layer, depth):
    t = pl.program_id(2)
    d = HGRN_HEAD_DIM
    nlev = rows.bit_length() - 1

    def lower_bound(dirn):
        x = [lbl_ref[2 * l + dirn: 2 * l + dirn + 1, :] for l in range(depth)]
        mx = functools.reduce(jnp.maximum, x)
        e = [jnp.exp(xi - mx) for xi in x]
        tot = functools.reduce(lambda a, b: a + b, e)
        p = [ei / tot for ei in e]
        return functools.reduce(lambda a, b: a + b, p[:layer + 1]) - p[0]

    row = lax.broadcasted_iota(jnp.int32, (rows, d), 0)
    r_i = lax.broadcasted_iota(jnp.int32, (rows, rows), 0)
    c_i = lax.broadcasted_iota(jnp.int32, (rows, rows), 1)

    def split3(x):
        hi = x.astype(BF16)
        r1 = x - hi.astype(F32)
        mid = r1.astype(BF16)
        lo = (r1 - mid.astype(F32)).astype(BF16)
        return jnp.concatenate([hi, mid, lo], axis=1)

    def group_rows(x, size, pick):
        parts = [jnp.broadcast_to(x[base + pick:base + pick + 1, :], (size, d)) for base in range(0, rows, size)]
        return parts[0] if len(parts) == 1 else jnp.concatenate(parts, axis=0)

    def run(rev):
        lb = lower_bound(1 if rev else 0)
        f = lb + (1.0 - lb) * (1.0 / (1.0 + jnp.exp(-z_ref[...])))
        kk = 1.0 - f
        lf = jnp.log(f)
        q = q_ref[...] * (d ** -0.5)
        vb = v_ref[...].astype(BF16)

        tri = ((c_i >= r_i) if rev else (c_i <= r_i)).astype(BF16)
        parts = _dot(tri, split3(lf))
        cum = parts[:, :d] + parts[:, d:2 * d] + parts[:, 2 * d:]
        edge = cum[0:1, :] if rev else cum[rows - 1:rows, :]

        st = st_ref[...]
        o = _dot_nt((q * jnp.exp(cum)).astype(BF16), st.astype(BF16))
        kd = kk * jnp.exp(edge - cum)
        st_ref[...] = st * jnp.exp(edge) + _dot_tn(vb, kd.astype(BF16))

        lvl = lvl_ref[1 if rev else 0]
        a = jnp.where(lvl == nlev, jnp.sum(q * kk, axis=-1, keepdims=True), 0.0)
        lf_up = pltpu.roll(lf, rows - 1, 0)
        lf_dn = pltpu.roll(lf, 1, 0)
        for lev in range(nlev):
            m = 1 << lev
            upper = (row & m) != 0
            target = jnp.logical_not(upper) if rev else upper
            if m == 1:
                dlt = jnp.where(upper, 0.0, lf) if rev else jnp.where(upper, lf, 0.0)
            elif m == 2:
                r4 = row & 3
                if rev:
                    dlt = jnp.where(r4 == 0, lf + lf_up, jnp.where(r4 == 1, lf, jnp.where(r4 == 2, 0.0, lf_dn)))
                else:
                    dlt = jnp.where(r4 == 0, lf_up, jnp.where(r4 == 1, 0.0, jnp.where(r4 == 2, lf, lf + lf_dn)))
            else:
                bound = group_rows(cum, 2 * m, m if rev else m - 1)
                dlt = jnp.where(target, cum - bound, bound - cum)
            y = (jnp.where(target, q, kk) * jnp.exp(dlt)).astype(BF16)
            a = jnp.where(lvl == lev, _dot_nt(y, y), a)
        return o + _dot(a.astype(BF16), vb)

    @pl.when(t == 0)
    def _():
        st_ref[...] = jnp.zeros_like(st_ref)

    @pl.when(t < nb)
    def _():
        ob_ref[nb - 1 - t] = run(True)

    @pl.when(t == nb)
    def _():
        st_ref[...] = jnp.zeros_like(st_ref)

    @pl.when(t >= nb)
    def _():
        o = run(False) + ob_ref[t - nb]
        o = o * lax.rsqrt(jnp.mean(o * o, axis=-1, keepdims=True) + RMS_EPS)
        o_ref[...] = (_silu(g_ref[...]) * (o * ng_ref[...])).astype(o_ref.dtype)


def hgrn2_mixer(proj, lb_logits, norm_g, layer, batch, seq, col0):
    m = proj.shape[0]
    width = norm_g.shape[0]
    d = HGRN_HEAD_DIM
    heads = width // d
    depth = lb_logits.shape[0]
    rows = min(HGRN_ROWS, seq)
    nb = seq // rows
    cb0 = col0 // d

    def blk(t):
        return jnp.where(t < nb, nb - 1 - t, t - nb)

    def fblk(t):
        return jnp.maximum(t - nb, 0)

    def spec(j):
        return pl.BlockSpec((rows, d), lambda b, h, t: (b * nb + blk(t), cb0 + j * heads + h))

    zspec = pl.BlockSpec((rows, d),
                         lambda b, h, t: (b * nb + blk(t), cb0 + jnp.where(t < nb, 2, 1) * heads + h))
    assert rows & (rows - 1) == 0
    nlev = rows.bit_length() - 1
    idx = np.arange(rows)
    x = idx[:, None] ^ idx[None, :]
    lev = np.where(x > 0, np.floor(np.log2(np.maximum(x, 1))), nlev).astype(np.int32)
    lvl = jnp.asarray(np.stack([np.where(idx[:, None] >= idx[None, :], lev, -1),
                                np.where(idx[:, None] <= idx[None, :], lev, -1)]), jnp.int32)
    return pl.pallas_call(
        functools.partial(_hgrn_kernel, nb=nb, rows=rows, layer=layer, depth=depth),
        grid=(batch, heads, 2 * nb),
        in_specs=[pl.BlockSpec((2 * depth, d), lambda b, h, t: (0, h)),
                  pl.BlockSpec((2, rows, rows), lambda b, h, t: (0, 0, 0)),
                  spec(0), zspec, spec(3),
                  pl.BlockSpec((rows, d), lambda b, h, t: (b * nb + fblk(t), cb0 + 4 * heads + h)),
                  pl.BlockSpec((1, d), lambda b, h, t: (0, h))],
        out_specs=pl.BlockSpec((rows, d), lambda b, h, t: (b * nb + fblk(t), h)),
        out_shape=jax.ShapeDtypeStruct((m, width), BF16),
        scratch_shapes=[pltpu.VMEM((d, d), F32),
                        pltpu.VMEM((nb, rows, d), F32)],
        compiler_params=_cparams(("parallel", "parallel", "arbitrary")),
        name="hgrn2",
    )(lb_logits.reshape(2 * depth, width), lvl, proj, proj, proj, proj, norm_g.reshape(1, width))


def _dft_cos_sin(n):
    idx = np.arange(n)
    ang = 2.0 * np.pi * ((idx[:, None] * idx[None, :]) % n) / n
    return np.cos(ang), np.sin(ang)


def _fft_chan_kernel(z_ref, m_ref, vr_ref, vi_ref, *, groups):
    gd = FOURIER_GROUP_DIM
    mat = m_ref[...]
    for g in range(groups):
        pq = _dot_exact(z_ref[:, g * gd:(g + 1) * gd], mat)
        vr_ref[:, g * gd:(g + 1) * gd] = pq[:, :gd]
        vi_ref[:, g * gd:(g + 1) * gd] = pq[:, gd:]


def _fft_stage1_kernel(vr_ref, vi_ref, m_ref, twc_ref, tws_ref, ur_ref, ui_ref, *, n1, reps, nsub):
    for bi in range(nsub):
        x = jnp.concatenate([vr_ref[:, bi, :], vi_ref[:, bi, :]], axis=0)
        tt = _dot_exact(m_ref[...], x)
        tr, ti = tt[:n1], tt[n1:]
        cw = jnp.tile(twc_ref[bi], (1, reps))
        sw = jnp.tile(tws_ref[bi], (1, reps))
        ur_ref[:, bi, :] = tr * cw + ti * sw
        ui_ref[:, bi, :] = ti * cw - tr * sw


def _fft_stage2_kernel(ur_ref, ui_ref, m_ref, w_ref, b_ref, o_ref, y_ref, *, groups, scale, n2, nsub):
    gd = FOURIER_GROUP_DIM
    for ki in range(nsub):
        x = jnp.concatenate([ur_ref[ki * n2:(ki + 1) * n2, :], ui_ref[ki * n2:(ki + 1) * n2, :]], axis=0)
        spec = _dot_exact(m_ref[...], x) * scale
        for g in range(groups):
            y = _dot(spec[:, g * gd:(g + 1) * gd].astype(BF16), w_ref[g].astype(BF16))
            y_ref[:, ki, g * gd:(g + 1) * gd] = y + b_ref[:, g * gd:(g + 1) * gd]
    o_ref[...] = y_ref[...].astype(o_ref.dtype)


def fourier_mixer(proj, w, bias, batch, seq, col0):
    m = proj.shape[0]
    groups, gd, _ = w.shape
    width = groups * gd
    n1 = min(FFT_N1, seq)
    n2 = seq // n1
    br = min(512, m)

    cc, sc = _dft_cos_sin(gd)
    chan = jnp.asarray(np.concatenate([cc, -sc], axis=1), F32)
    vr, vi = pl.pallas_call(
        functools.partial(_fft_chan_kernel, groups=groups),
        grid=(m // br,),
        in_specs=[pl.BlockSpec((br, width), lambda i: (i, col0 // width)),
                  pl.BlockSpec((gd, 2 * gd), lambda i: (0, 0))],
        out_specs=[pl.BlockSpec((br, width), lambda i: (i, 0))] * 2,
        out_shape=[jax.ShapeDtypeStruct((m, width), F32)] * 2,
        compiler_params=_cparams(("parallel",)),
        name="fft_channels",
    )(proj, chan)

    c1, s1 = _dft_cos_sin(n1)
    m1 = jnp.asarray(np.block([[c1, s1], [-s1, c1]]), F32)
    k1 = np.arange(n1)
    bb = np.arange(n2)
    tw = 2.0 * np.pi * ((bb[:, None] * k1[None, :]) % seq) / seq
    twc = jnp.broadcast_to(jnp.asarray(np.cos(tw), F32)[:, :, None], (n2, n1, LANES))
    tws = jnp.broadcast_to(jnp.asarray(np.sin(tw), F32)[:, :, None], (n2, n1, LANES))
    s1 = min(FFT_SUB1, n2)
    vspec = pl.BlockSpec((n1, s1, width), lambda bt, j: (bt, j, 0))
    tspec = pl.BlockSpec((s1, n1, LANES), lambda bt, j: (j, 0, 0))
    ur, ui = pl.pallas_call(
        functools.partial(_fft_stage1_kernel, n1=n1, reps=width // LANES, nsub=s1),
        grid=(batch, n2 // s1),
        in_specs=[vspec, vspec, pl.BlockSpec((2 * n1, 2 * n1), lambda bt, j: (0, 0)), tspec, tspec],
        out_specs=[vspec, vspec],
        out_shape=[jax.ShapeDtypeStruct((batch * n1, n2, width), F32)] * 2,
        compiler_params=_cparams(("parallel", "parallel")),
        name="fft_stage1",
    )(vr.reshape(batch * n1, n2, width), vi.reshape(batch * n1, n2, width), m1, twc, tws)

    c2, s2 = _dft_cos_sin(n2)
    m2 = jnp.asarray(np.concatenate([c2, s2], axis=1), F32)
    s2 = min(FFT_SUB2, n1)
    uspec = pl.BlockSpec((s2 * n2, width), lambda bt, j: (bt * (n1 // s2) + j, 0))
    out = pl.pallas_call(
        functools.partial(_fft_stage2_kernel, groups=groups, scale=1.0 / math.sqrt(seq * gd), n2=n2, nsub=s2),
        grid=(batch, n1 // s2),
        in_specs=[uspec, uspec, pl.BlockSpec((n2, 2 * n2), lambda bt, j: (0, 0)),
                  pl.BlockSpec((groups, gd, gd), lambda bt, j: (0, 0, 0)),
                  pl.BlockSpec((1, width), lambda bt, j: (0, 0))],
        out_specs=pl.BlockSpec((n2, s2, width), lambda bt, j: (bt, j, 0)),
        out_shape=jax.ShapeDtypeStruct((batch * n2, n1, width), BF16),
        scratch_shapes=[pltpu.VMEM((n2, s2, width), F32)],
        compiler_params=_cparams(("parallel", "parallel")),
        name="fft_stage2",
    )(ur.reshape(m, width), ui.reshape(m, width), m2, w, bias.reshape(1, width))
    return out.reshape(m, width)


def _xattn_kernel(q_ref, k_ref, v_ref, o_ref):
    s = _dot_nt(q_ref[...], k_ref[...]) * (XATTN_HEAD_DIM ** -0.5)
    p = jnp.exp(s - jnp.max(s, axis=-1, keepdims=True))
    denom = jnp.sum(p, axis=-1, keepdims=True)
    o_ref[...] = (_dot(p.astype(BF16), v_ref[...]) / denom).astype(o_ref.dtype)


def cross_attention_core(q, k, v, batch, seq, mem_tokens, ts=512):
    m, width = q.shape
    hd = XATTN_HEAD_DIM
    heads = width // hd
    ts = min(ts, seq)
    nt = seq // ts
    kv = pl.BlockSpec((mem_tokens, hd), lambda b, h, i: (b, h))
    qs = pl.BlockSpec((ts, hd), lambda b, h, i: (b * nt + i, h))
    return pl.pallas_call(
        _xattn_kernel,
        grid=(batch, heads, nt),
        in_specs=[qs, kv, kv],
        out_specs=qs,
        out_shape=jax.ShapeDtypeStruct((m, width), BF16),
        compiler_params=_cparams(("parallel", "parallel", "parallel")),
        name="xattn",
    )(q, k, v)


def kernel(x, mem, mem_norm_g, pre_mix_g, w_in, ret_norm_g, hgrn_lb_logits, hgrn_norm_g, fourier_w, fourier_b,
           w_out, post_mix_g, pre_xattn_g, xattn_wq, xattn_wk, xattn_wv, xattn_wo, post_xattn_g, pre_ffn_g,
           ffn_w_gate, ffn_w_up, ffn_w_down, post_ffn_g):
    batch, seq, d = x.shape
    depth = w_in.shape[0]
    mem_tokens = mem.shape[1]
    m = batch * seq
    ret_w = ret_norm_g.shape[1]
    hgrn_w = hgrn_norm_g.shape[1]

    h = x.reshape(m, d)
    mem_n = rmsnorm_rows(mem.reshape(batch * mem_tokens, d), mem_norm_g)
    a = rmsnorm_rows(h, pre_mix_g[0])
    for l in range(depth):
        proj = matmul([a], w_in, l, F32, 1024, 512)
        y_ret = retention_mixer(proj, ret_norm_g[l], batch, seq, 0)
        y_hgrn = hgrn2_mixer(proj, hgrn_lb_logits, hgrn_norm_g[l], l, batch, seq, 4 * ret_w)
        y_fft = fourier_mixer(proj, fourier_w[l], fourier_b[l], batch, seq, 4 * ret_w + 5 * hgrn_w)
        mixed = matmul([y_ret, y_hgrn, y_fft], w_out, l, F32, 1024, 512)
        h, c = add_norm(h, mixed, post_mix_g[l], pre_xattn_g[l])
        q = matmul([c], xattn_wq, l, BF16, 1024, 512)
        kx = matmul([mem_n], xattn_wk, l, BF16, 512, 512)
        vx = matmul([mem_n], xattn_wv, l, BF16, 512, 512)
        att = cross_attention_core(q, kx, vx, batch, seq, mem_tokens)
        xa = matmul([att], xattn_wo, l, F32, 1024, 1024)
        h, f = add_norm(h, xa, post_xattn_g[l], pre_ffn_g[l])
        hid = swiglu_matmul(f, ffn_w_gate, ffn_w_up, l, 1024, 256)
        ff = matmul_ktiled(hid, ffn_w_down, l, 1024, 1024, 2048)
        h, a = add_norm(h, ff, post_ffn_g[l], pre_mix_g[l + 1] if l + 1 < depth else None)
    return h.reshape(batch, seq, d)
```

```python
import functools
import math

import numpy as np
import jax
import jax.numpy as jnp
from jax import lax
from jax.experimental import pallas as pl
from jax.experimental.pallas import tpu as pltpu

F32 = jnp.float32
BF16 = jnp.bfloat16

RET_HEAD_DIM = 256
HGRN_HEAD_DIM = 128
FOURIER_GROUP_DIM = 128
XATTN_HEADS = 4
XATTN_HEAD_DIM = 256
ROPE_BASE = 10000.0
RMS_EPS = 1e-6
GN_EPS = 1e-5

LANES = 128
MXU_TILE = 256
V7X_VMEM_BYTES = 64 * 1024 * 1024
VMEM_LIMIT = V7X_VMEM_BYTES - 8 * 1024 * 1024

RET_CHUNK = MXU_TILE
RET_ROWS = 1024
HGRN_ROWS = 512
HGRN_SUB = 128
FFT_N1 = 64
FFT_SUB1 = 8
FFT_SUB2 = 16


def _cparams(sem):
    return pltpu.CompilerParams(dimension_semantics=sem, vmem_limit_bytes=VMEM_LIMIT)


def _tile(n, preferred):
    for t in range(min(preferred, n) // LANES * LANES, 0, -LANES):
        if n % t == 0:
            return t
    return n


def _dot(a, b):
    return jnp.dot(a, b, preferred_element_type=F32)


def _dot_nt(a, b):
    return lax.dot_general(a, b, (((1,), (1,)), ((), ())), preferred_element_type=F32)


def _dot_tn(a, b):
    return lax.dot_general(a, b, (((0,), (0,)), ((), ())), preferred_element_type=F32)


def _split2(x):
    hi = x.astype(BF16)
    return hi, (x - hi.astype(F32)).astype(BF16)


def _const3_lhs(mat):
    hi, lo = _split2(jnp.asarray(mat, F32))
    return jnp.concatenate([hi, hi, lo], axis=1)


def _const3_rhs(mat):
    hi, lo = _split2(jnp.asarray(mat, F32))
    return jnp.concatenate([hi, hi, lo], axis=0)


def _dot3_lhs(m3, x):
    hi, lo = _split2(x)
    return _dot(m3, jnp.concatenate([hi, lo, hi], axis=0))


def _dot3_rhs(x, m3):
    hi, lo = _split2(x)
    return _dot(jnp.concatenate([hi, lo, hi], axis=1), m3)


def _rms(x, g):
    ms = jnp.mean(x * x, axis=-1, keepdims=True)
    return x * lax.rsqrt(ms + RMS_EPS) * g


def _silu(x):
    return x * (1.0 / (1.0 + jnp.exp(-x)))


def _norm_kernel(x_ref, g_ref, o_ref):
    o_ref[...] = _rms(x_ref[...], g_ref[...]).astype(o_ref.dtype)


def rmsnorm_rows(x, g, br=256):
    m, d = x.shape
    out_dtype = BF16
    br = min(br, m)
    return pl.pallas_call(
        _norm_kernel,
        grid=(m // br,),
        in_specs=[pl.BlockSpec((br, d), lambda i: (i, 0)),
                  pl.BlockSpec((1, d), lambda i: (0, 0))],
        out_specs=pl.BlockSpec((br, d), lambda i: (i, 0)),
        out_shape=jax.ShapeDtypeStruct((m, d), out_dtype),
        compiler_params=_cparams(("parallel",)),
        name="rmsnorm_rows",
    )(x, g.reshape(1, d))


def _add_norm_kernel(h_ref, y_ref, gp_ref, gn_ref, ho_ref, a_ref):
    h = h_ref[...] + _rms(y_ref[...], gp_ref[...])
    ho_ref[...] = h
    a_ref[...] = _rms(h, gn_ref[...]).astype(a_ref.dtype)


def _add_norm_last_kernel(h_ref, y_ref, gp_ref, ho_ref):
    ho_ref[...] = h_ref[...] + _rms(y_ref[...], gp_ref[...])


def add_norm(h, y, g_post, g_next, br=256):
    m, d = h.shape
    row = pl.BlockSpec((br, d), lambda i: (i, 0))
    vec = pl.BlockSpec((1, d), lambda i: (0, 0))
    if g_next is None:
        return pl.pallas_call(
            _add_norm_last_kernel, grid=(m // br,),
            in_specs=[row, row, vec], out_specs=row,
            out_shape=jax.ShapeDtypeStruct((m, d), F32),
            compiler_params=_cparams(("parallel",)), name="add_norm_last",
        )(h, y, g_post.reshape(1, d)), None
    return pl.pallas_call(
        _add_norm_kernel, grid=(m // br,),
        in_specs=[row, row, vec, vec], out_specs=[row, row],
        out_shape=[jax.ShapeDtypeStruct((m, d), F32), jax.ShapeDtypeStruct((m, d), BF16)],
        compiler_params=_cparams(("parallel",)), name="add_norm",
    )(h, y, g_post.reshape(1, d), g_next.reshape(1, d))


def _mm_kernel(*refs, k_parts):
    a_refs, w_ref, o_ref = refs[:len(k_parts)], refs[len(k_parts)], refs[-1]
    acc, r0 = None, 0
    for a_ref, kp in zip(a_refs, k_parts):
        part = _dot(a_ref[...], w_ref[r0:r0 + kp, :].astype(BF16))
        acc = part if acc is None else acc + part
        r0 += kp
    o_ref[...] = acc.astype(o_ref.dtype)


def matmul(a_parts, w, layer, out_dtype, bm, bn):
    m = a_parts[0].shape[0]
    k_parts = tuple(a.shape[1] for a in a_parts)
    kdim, n = w.shape[1], w.shape[2]
    assert sum(k_parts) == kdim
    bm, bn = _tile(m, bm), _tile(n, bn)
    return pl.pallas_call(
        functools.partial(_mm_kernel, k_parts=k_parts),
        grid=(m // bm, n // bn),
        in_specs=[pl.BlockSpec((bm, kp), lambda i, j: (i, 0)) for kp in k_parts]
        + [pl.BlockSpec((None, kdim, bn), lambda i, j: (layer, 0, j))],
        out_specs=pl.BlockSpec((bm, bn), lambda i, j: (i, j)),
        out_shape=jax.ShapeDtypeStruct((m, n), out_dtype),
        compiler_params=_cparams(("parallel", "arbitrary")),
        name="matmul",
    )(*a_parts, w)


def _swiglu_kernel(a_ref, wg_ref, wu_ref, o_ref):
    a = a_ref[...]
    gate = _dot(a, wg_ref[...].astype(BF16))
    up = _dot(a, wu_ref[...].astype(BF16))
    o_ref[...] = (_silu(gate) * up).astype(o_ref.dtype)


def swiglu_matmul(a, wg, wu, layer, bm, bn):
    m, kdim = a.shape
    n = wg.shape[2]
    bm, bn = _tile(m, bm), _tile(n, bn)
    wspec = pl.BlockSpec((None, kdim, bn), lambda i, j: (layer, 0, j))
    return pl.pallas_call(
        _swiglu_kernel,
        grid=(m // bm, n // bn),
        in_specs=[pl.BlockSpec((bm, kdim), lambda i, j: (i, 0)), wspec, wspec],
        out_specs=pl.BlockSpec((bm, bn), lambda i, j: (i, j)),
        out_shape=jax.ShapeDtypeStruct((m, n), BF16),
        compiler_params=_cparams(("parallel", "arbitrary")),
        name="swiglu_matmul",
    )(a, wg, wu)


def _mm_ktiled_kernel(*refs, n_tail):
    a_ref, w_ref = refs[0], refs[1]
    t_refs, wt_ref, o_ref = refs[2:2 + n_tail], refs[2 + n_tail], refs[-1]
    k = pl.program_id(2)

    @pl.when(k == 0)
    def _():
        tail = jnp.concatenate([t[...] for t in t_refs], axis=1)
        o_ref[...] = _dot(a_ref[...], w_ref[...].astype(BF16)) + _dot(tail, wt_ref[...].astype(BF16))

    @pl.when(k > 0)
    def _():
        o_ref[...] += _dot(a_ref[...], w_ref[...].astype(BF16))


def matmul_ktiled(a, w, layer, bm, bn, bk):
    m, kdim = a.shape
    n = w.shape[2]
    bm, bn = _tile(m, bm), _tile(n, bn)
    nk = kdim // bk
    kmain = nk * bk
    ktail = kdim - kmain
    assert nk >= 1 and ktail > 0 and ktail % MXU_TILE == 0 and kmain % MXU_TILE == 0
    n_tail = ktail // MXU_TILE
    tb0 = kmain // MXU_TILE
    w_tail = w[layer, kmain:, :]
    tail_specs = [pl.BlockSpec((bm, MXU_TILE), functools.partial(lambda i, j, k, c: (i, c), c=tb0 + c))
                  for c in range(n_tail)]
    return pl.pallas_call(
        functools.partial(_mm_ktiled_kernel, n_tail=n_tail),
        grid=(m // bm, n // bn, nk),
        in_specs=[pl.BlockSpec((bm, bk), lambda i, j, k: (i, k)),
                  pl.BlockSpec((None, bk, bn), lambda i, j, k: (layer, k, j))]
        + tail_specs + [pl.BlockSpec((ktail, bn), lambda i, j, k: (0, j))],
        out_specs=pl.BlockSpec((bm, bn), lambda i, j, k: (i, j)),
        out_shape=jax.ShapeDtypeStruct((m, n), F32),
        compiler_params=_cparams(("parallel", "parallel", "arbitrary")),
        name="matmul_ktiled",
    )(a, w, *([a] * n_tail), w_tail)


def _ret_kernel(lg_ref, cos_ref, sin_ref, q_ref, k_ref, v_ref, g_ref, ng_ref, o_ref,
                sf_ref, sb_ref, ob_ref, qr_ref, kr_ref, dec_ref, vec_ref, *, nb, rows, c):
    t = pl.program_id(2)
    half = RET_HEAD_DIM // 2
    nch = rows // c

    @pl.when(t == 0)
    def _():
        lgf = lg_ref[0, 0:1, :]
        lgb = lg_ref[0, 1:2, :]
        i_row = lax.broadcasted_iota(jnp.int32, (c, half), 0).astype(F32)
        rel = (lax.broadcasted_iota(jnp.int32, (c, c), 0) - lax.broadcasted_iota(jnp.int32, (c, c), 1)).astype(F32)
        dec_ref[...] = jnp.exp(jnp.where(rel >= 0, rel * lgf[:, :1], -rel * lgb[:, :1]))
        vec_ref[0] = jnp.exp((i_row + 1.0) * lgf)
        vec_ref[1] = jnp.exp((c - 1.0 - i_row) * lgf)
        vec_ref[2] = jnp.exp((c - i_row) * lgb)
        vec_ref[3] = jnp.exp(i_row * lgb)
        vec_ref[4] = jnp.exp(c * lgf) + jnp.zeros((c, half), F32)
        vec_ref[5] = jnp.exp(c * lgb) + jnp.zeros((c, half), F32)
        sf_ref[...] = jnp.zeros_like(sf_ref)
        sb_ref[...] = jnp.zeros_like(sb_ref)

    def lanes2(d):
        return jnp.concatenate([d, d], axis=-1)

    @pl.when(t < nb)
    def _():
        base = (nb - 1 - t) * rows
        for ci in range(nch - 1, -1, -1):
            r0 = ci * c
            g0 = pl.multiple_of(base + r0, c)
            cos = cos_ref[r0:r0 + c, :]
            sin = sin_ref[r0:r0 + c, :]

            def rot(x):
                x1, x2 = x[:, :half], x[:, half:]
                return jnp.concatenate([x1 * cos - x2 * sin, x1 * sin + x2 * cos], axis=-1)

            q = rot(q_ref[r0:r0 + c, :]) * (RET_HEAD_DIM ** -0.5)
            k = rot(k_ref[r0:r0 + c, :])
            qr_ref[pl.ds(g0, c), :] = q
            kr_ref[pl.ds(g0, c), :] = k
            vb = v_ref[r0:r0 + c, :].astype(BF16)
            sb = sb_ref[...]
            ob_ref[pl.ds(g0, c), :] = _dot((q * lanes2(vec_ref[2])).astype(BF16), sb.astype(BF16))
            kd = k * lanes2(vec_ref[3])
            sb_ref[...] = vec_ref[5][0:1, 0:1] * sb + _dot_tn(kd.astype(BF16), vb)

    @pl.when(t >= nb)
    def _():
        base = (t - nb) * rows
        for ci in range(nch):
            r0 = ci * c
            g0 = pl.multiple_of(base + r0, c)
            q = qr_ref[pl.ds(g0, c), :]
            k = kr_ref[pl.ds(g0, c), :]
            vb = v_ref[r0:r0 + c, :].astype(BF16)
            sf = sf_ref[...]
            scores = _dot_nt(q.astype(BF16), k.astype(BF16)) * dec_ref[...]
            o = (_dot(scores.astype(BF16), vb) + _dot((q * lanes2(vec_ref[0])).astype(BF16), sf.astype(BF16))
                 + ob_ref[pl.ds(g0, c), :])
            kd = k * lanes2(vec_ref[1])
            sf_ref[...] = vec_ref[4][0:1, 0:1] * sf + _dot_tn(kd.astype(BF16), vb)
            mu = jnp.mean(o, axis=-1, keepdims=True)
            oc = o - mu
            var = jnp.mean(oc * oc, axis=-1, keepdims=True)
            y = oc * lax.rsqrt(var + GN_EPS) * ng_ref[...]
            o_ref[r0:r0 + c, :] = (_silu(g_ref[r0:r0 + c, :]) * y).astype(o_ref.dtype)


def retention_mixer(proj, norm_g, batch, seq, col0):
    m = proj.shape[0]
    width = norm_g.shape[0]
    heads = width // RET_HEAD_DIM
    c = min(RET_CHUNK, seq)
    rows = min(RET_ROWS, seq)
    nb = seq // rows
    half = RET_HEAD_DIM // 2
    cb0 = col0 // RET_HEAD_DIM

    hidx = jnp.arange(heads, dtype=F32)
    gamma = 1.0 - jnp.power(2.0, -5.0 - hidx)
    lg = jnp.log(gamma)
    lg2 = jnp.broadcast_to(jnp.stack([lg, lg[::-1]], axis=1)[:, :, None], (heads, 2, LANES))
    inv = jnp.power(ROPE_BASE, -jnp.arange(half, dtype=F32) / half)
    ang = jnp.arange(seq, dtype=F32)[:, None] * inv[None, :]
    cos, sin = jnp.cos(ang), jnp.sin(ang)

    def bblk(t):
        return jnp.maximum(nb - 1 - t, 0)

    def fblk(t):
        return jnp.maximum(t - nb, 0)

    def blk(t):
        return jnp.where(t < nb, nb - 1 - t, t - nb)

    def spec(j, which):
        return pl.BlockSpec((rows, RET_HEAD_DIM), lambda b, h, t: (b * nb + which(t), cb0 + j * heads + h))

    return pl.pallas_call(
        functools.partial(_ret_kernel, nb=nb, rows=rows, c=c),
        grid=(batch, heads, 2 * nb),
        in_specs=[pl.BlockSpec((1, 2, LANES), lambda b, h, t: (h, 0, 0)),
                  pl.BlockSpec((rows, half), lambda b, h, t: (bblk(t), 0)),
                  pl.BlockSpec((rows, half), lambda b, h, t: (bblk(t), 0)),
                  spec(0, bblk), spec(1, bblk), spec(2, blk), spec(3, fblk),
                  pl.BlockSpec((1, RET_HEAD_DIM), lambda b, h, t: (0, h))],
        out_specs=pl.BlockSpec((rows, RET_HEAD_DIM), lambda b, h, t: (b * nb + fblk(t), h)),
        out_shape=jax.ShapeDtypeStruct((m, width), BF16),
        scratch_shapes=[pltpu.VMEM((RET_HEAD_DIM, RET_HEAD_DIM), F32),
                        pltpu.VMEM((RET_HEAD_DIM, RET_HEAD_DIM), F32),
                        pltpu.VMEM((seq, RET_HEAD_DIM), F32),
                        pltpu.VMEM((seq, RET_HEAD_DIM), F32),
                        pltpu.VMEM((seq, RET_HEAD_DIM), F32),
                        pltpu.VMEM((c, c), F32),
                        pltpu.VMEM((6, c, half), F32)],
        compiler_params=_cparams(("parallel", "parallel", "arbitrary")),
        name="retention",
    )(lg2, cos, sin, proj, proj, proj, proj, norm_g.reshape(1, width))


def _hgrn_kernel(lbl_ref, lvl_ref, up_ref, sg_ref, q_ref, z_ref, v_ref, g_ref, ng_ref, o_ref,
                 st_ref, ob_ref, *, nb, rows, sub, layer, depth):
    t = pl.program_id(2)
    d = HGRN_HEAD_DIM
    nlev = sub.bit_length() - 1

    def lower_bound(dirn):
        x = [lbl_ref[2 * l + dirn: 2 * l + dirn + 1, :] for l in range(depth)]
        mx = functools.reduce(jnp.maximum, x)
        e = [jnp.exp(xi - mx) for xi in x]
        tot = functools.reduce(lambda a, b: a + b, e)
        p = [ei / tot for ei in e]
        return functools.reduce(lambda a, b: a + b, p[:layer + 1]) - p[0]

    row = lax.broadcasted_iota(jnp.int32, (sub, d), 0)
    r_i = lax.broadcasted_iota(jnp.int32, (sub, sub), 0)
    c_i = lax.broadcasted_iota(jnp.int32, (sub, sub), 1)

    def split3(x):
        hi = x.astype(BF16)
        r1 = x - hi.astype(F32)
        mid = r1.astype(BF16)
        lo = (r1 - mid.astype(F32)).astype(BF16)
        return jnp.concatenate([hi, mid, lo], axis=1)

    def group_rows(x, size, pick):
        parts = [jnp.broadcast_to(x[base + pick:base + pick + 1, :], (size, d)) for base in range(0, sub, size)]
        return parts[0] if len(parts) == 1 else jnp.concatenate(parts, axis=0)

    def run(r0, rev, lb):
        f = lb + (1.0 - lb) * (1.0 / (1.0 + jnp.exp(-z_ref[r0:r0 + sub, :])))
        kk = 1.0 - f
        lf = jnp.log(f)
        q = q_ref[r0:r0 + sub, :] * (d ** -0.5)
        vb = v_ref[r0:r0 + sub, :].astype(BF16)

        tri = ((c_i >= r_i) if rev else (c_i <= r_i)).astype(BF16)
        parts = _dot(tri, split3(lf))
        cum = parts[:, :d] + parts[:, d:2 * d] + parts[:, 2 * d:]
        edge = cum[0:1, :] if rev else cum[sub - 1:sub, :]

        st = st_ref[...]
        o = _dot_nt((q * jnp.exp(cum)).astype(BF16), st.astype(BF16))
        kd = kk * jnp.exp(edge - cum)
        st_ref[...] = st * jnp.exp(edge) + _dot_tn(vb, kd.astype(BF16))

        lvl = lvl_ref[1 if rev else 0]
        a = jnp.where(lvl == nlev, jnp.sum(q * kk, axis=-1, keepdims=True), 0.0)
        lf_up = pltpu.roll(lf, sub - 1, 0)
        lf_dn = pltpu.roll(lf, 1, 0)
        qk = q - kk
        for lev in range(nlev):
            m = 1 << lev
            if m <= 2:
                upper = (row & m) != 0
                target = jnp.logical_not(upper) if rev else upper
                if m == 1:
                    dlt = jnp.where(upper, 0.0, lf) if rev else jnp.where(upper, lf, 0.0)
                else:
                    r4 = row & 3
                    if rev:
                        dlt = jnp.where(r4 == 0, lf + lf_up, jnp.where(r4 == 1, lf, jnp.where(r4 == 2, 0.0, lf_dn)))
                    else:
                        dlt = jnp.where(r4 == 0, lf_up, jnp.where(r4 == 1, 0.0, jnp.where(r4 == 2, lf, lf + lf_dn)))
                y = jnp.where(target, q, kk) * jnp.exp(dlt)
            else:
                up = up_ref[lev]
                sg = sg_ref[lev]
                bound = group_rows(cum, 2 * m, m if rev else m - 1)
                if rev:
                    y = (q - up * qk) * jnp.exp2((bound - cum) * sg)
                else:
                    y = (kk + up * qk) * jnp.exp2((cum - bound) * sg)
            yb = y.astype(BF16)
            a = jnp.where(lvl == lev, _dot_nt(yb, yb), a)
        return o + _dot(a.astype(BF16), vb)

    nsb = rows // sub

    @pl.when(t == 0)
    def _():
        st_ref[...] = jnp.zeros_like(st_ref)

    @pl.when(t < nb)
    def _():
        lb = lower_bound(1)
        base = (nb - 1 - t) * rows
        for si in range(nsb - 1, -1, -1):
            ob_ref[pl.ds(pl.multiple_of(base + si * sub, sub), sub), :] = run(si * sub, True, lb)

    @pl.when(t == nb)
    def _():
        st_ref[...] = jnp.zeros_like(st_ref)

    @pl.when(t >= nb)
    def _():
        lb = lower_bound(0)
        base = (t - nb) * rows
        for si in range(nsb):
            r0 = si * sub
            o = run(r0, False, lb) + ob_ref[pl.ds(pl.multiple_of(base + r0, sub), sub), :]
            o = o * lax.rsqrt(jnp.mean(o * o, axis=-1, keepdims=True) + RMS_EPS)
            o_ref[r0:r0 + sub, :] = (_silu(g_ref[r0:r0 + sub, :]) * (o * ng_ref[...])).astype(o_ref.dtype)


def hgrn2_mixer(proj, lb_logits, norm_g, layer, batch, seq, col0):
    m = proj.shape[0]
    width = norm_g.shape[0]
    d = HGRN_HEAD_DIM
    heads = width // d
    depth = lb_logits.shape[0]
    rows = min(HGRN_ROWS, seq)
    nb = seq // rows
    cb0 = col0 // d

    def blk(t):
        return jnp.where(t < nb, nb - 1 - t, t - nb)

    def fblk(t):
        return jnp.maximum(t - nb, 0)

    def spec(j):
        return pl.BlockSpec((rows, d), lambda b, h, t: (b * nb + blk(t), cb0 + j * heads + h))

    zspec = pl.BlockSpec((rows, d),
                         lambda b, h, t: (b * nb + blk(t), cb0 + jnp.where(t < nb, 2, 1) * heads + h))
    sub = min(HGRN_SUB, rows)
    assert sub & (sub - 1) == 0 and rows % sub == 0
    nlev = sub.bit_length() - 1
    idx = np.arange(sub)
    x = idx[:, None] ^ idx[None, :]
    lev = np.where(x > 0, np.floor(np.log2(np.maximum(x, 1))), nlev).astype(np.int32)
    lvl = jnp.asarray(np.stack([np.where(idx[:, None] >= idx[None, :], lev, -1),
                                np.where(idx[:, None] <= idx[None, :], lev, -1)]), jnp.int32)
    upper = np.stack([((idx >> l) & 1).astype(np.float32) for l in range(nlev)])
    up = jnp.broadcast_to(jnp.asarray(upper)[:, :, None], (nlev, sub, d))
    sg = jnp.broadcast_to(jnp.asarray((2.0 * upper - 1.0) * np.float32(math.log2(math.e)))[:, :, None],
                          (nlev, sub, d))
    tab = pl.BlockSpec((nlev, sub, d), lambda b, h, t: (0, 0, 0))
    return pl.pallas_call(
        functools.partial(_hgrn_kernel, nb=nb, rows=rows, sub=sub, layer=layer, depth=depth),
        grid=(batch, heads, 2 * nb),
        in_specs=[pl.BlockSpec((2 * depth, d), lambda b, h, t: (0, h)),
                  pl.BlockSpec((2, sub, sub), lambda b, h, t: (0, 0, 0)), tab, tab,
                  spec(0), zspec, spec(3),
                  pl.BlockSpec((rows, d), lambda b, h, t: (b * nb + fblk(t), cb0 + 4 * heads + h)),
                  pl.BlockSpec((1, d), lambda b, h, t: (0, h))],
        out_specs=pl.BlockSpec((rows, d), lambda b, h, t: (b * nb + fblk(t), h)),
        out_shape=jax.ShapeDtypeStruct((m, width), BF16),
        scratch_shapes=[pltpu.VMEM((d, d), F32),
                        pltpu.VMEM((seq, d), F32)],
        compiler_params=_cparams(("parallel", "parallel", "arbitrary")),
        name="hgrn2",
    )(lb_logits.reshape(2 * depth, width), lvl, up, sg, proj, proj, proj, proj, norm_g.reshape(1, width))


def _dft_cos_sin(n):
    idx = np.arange(n)
    ang = 2.0 * np.pi * ((idx[:, None] * idx[None, :]) % n) / n
    return np.cos(ang), np.sin(ang)


def _fft_chan_kernel(z_ref, m_ref, vr_ref, vi_ref, *, groups):
    gd = FOURIER_GROUP_DIM
    mat = m_ref[...]
    for g in range(groups):
        pq = _dot3_rhs(z_ref[:, g * gd:(g + 1) * gd], mat)
        vr_ref[:, g * gd:(g + 1) * gd] = pq[:, :gd]
        vi_ref[:, g * gd:(g + 1) * gd] = pq[:, gd:]


def _fft_stage1_kernel(vr_ref, vi_ref, m_ref, twc_ref, tws_ref, ur_ref, ui_ref, *, n1, reps, nsub):
    for bi in range(nsub):
        x = jnp.concatenate([vr_ref[:, bi, :], vi_ref[:, bi, :]], axis=0)
        tt = _dot3_lhs(m_ref[...], x)
        tr, ti = tt[:n1], tt[n1:]
        cw = jnp.tile(twc_ref[bi], (1, reps))
        sw = jnp.tile(tws_ref[bi], (1, reps))
        ur_ref[:, bi, :] = tr * cw + ti * sw
        ui_ref[:, bi, :] = ti * cw - tr * sw


def _fft_stage2_kernel(ur_ref, ui_ref, m_ref, w_ref, b_ref, o_ref, y_ref, *, groups, scale, n2, nsub):
    gd = FOURIER_GROUP_DIM
    for ki in range(nsub):
        x = jnp.concatenate([ur_ref[ki * n2:(ki + 1) * n2, :], ui_ref[ki * n2:(ki + 1) * n2, :]], axis=0)
        spec = _dot3_lhs(m_ref[...], x) * scale
        for g in range(groups):
            y = _dot(spec[:, g * gd:(g + 1) * gd].astype(BF16), w_ref[g].astype(BF16))
            y_ref[:, ki, g * gd:(g + 1) * gd] = y + b_ref[:, g * gd:(g + 1) * gd]
    o_ref[...] = y_ref[...].astype(o_ref.dtype)


def fourier_mixer(proj, w, bias, batch, seq, col0):
    m = proj.shape[0]
    groups, gd, _ = w.shape
    width = groups * gd
    n1 = min(FFT_N1, seq)
    n2 = seq // n1
    br = min(512, m)

    cc, sc = _dft_cos_sin(gd)
    chan = _const3_rhs(np.concatenate([cc, -sc], axis=1))
    vr, vi = pl.pallas_call(
        functools.partial(_fft_chan_kernel, groups=groups),
        grid=(m // br,),
        in_specs=[pl.BlockSpec((br, width), lambda i: (i, col0 // width)),
                  pl.BlockSpec((3 * gd, 2 * gd), lambda i: (0, 0))],
        out_specs=[pl.BlockSpec((br, width), lambda i: (i, 0))] * 2,
        out_shape=[jax.ShapeDtypeStruct((m, width), F32)] * 2,
        compiler_params=_cparams(("parallel",)),
        name="fft_channels",
    )(proj, chan)

    c1, s1 = _dft_cos_sin(n1)
    m1 = _const3_lhs(np.block([[c1, s1], [-s1, c1]]))
    k1 = np.arange(n1)
    bb = np.arange(n2)
    tw = 2.0 * np.pi * ((bb[:, None] * k1[None, :]) % seq) / seq
    twc = jnp.broadcast_to(jnp.asarray(np.cos(tw), F32)[:, :, None], (n2, n1, LANES))
    tws = jnp.broadcast_to(jnp.asarray(np.sin(tw), F32)[:, :, None], (n2, n1, LANES))
    s1 = min(FFT_SUB1, n2)
    vspec = pl.BlockSpec((n1, s1, width), lambda bt, j: (bt, j, 0))
    tspec = pl.BlockSpec((s1, n1, LANES), lambda bt, j: (j, 0, 0))
    ur, ui = pl.pallas_call(
        functools.partial(_fft_stage1_kernel, n1=n1, reps=width // LANES, nsub=s1),
        grid=(batch, n2 // s1),
        in_specs=[vspec, vspec, pl.BlockSpec((2 * n1, 6 * n1), lambda bt, j: (0, 0)), tspec, tspec],
        out_specs=[vspec, vspec],
        out_shape=[jax.ShapeDtypeStruct((batch * n1, n2, width), F32)] * 2,
        compiler_params=_cparams(("parallel", "parallel")),
        name="fft_stage1",
    )(vr.reshape(batch * n1, n2, width), vi.reshape(batch * n1, n2, width), m1, twc, tws)

    c2, s2 = _dft_cos_sin(n2)
    m2 = _const3_lhs(np.concatenate([c2, s2], axis=1))
    s2 = min(FFT_SUB2, n1)
    uspec = pl.BlockSpec((s2 * n2, width), lambda bt, j: (bt * (n1 // s2) + j, 0))
    out = pl.pallas_call(
        functools.partial(_fft_stage2_kernel, groups=groups, scale=1.0 / math.sqrt(seq * gd), n2=n2, nsub=s2),
        grid=(batch, n1 // s2),
        in_specs=[uspec, uspec, pl.BlockSpec((n2, 6 * n2), lambda bt, j: (0, 0)),
                  pl.BlockSpec((groups, gd, gd), lambda bt, j: (0, 0, 0)),
                  pl.BlockSpec((1, width), lambda bt, j: (0, 0))],
        out_specs=pl.BlockSpec((n2, s2, width), lambda bt, j: (bt, j, 0)),
        out_shape=jax.ShapeDtypeStruct((batch * n2, n1, width), BF16),
        scratch_shapes=[pltpu.VMEM((n2, s2, width), F32)],
        compiler_params=_cparams(("parallel", "parallel")),
        name="fft_stage2",
    )(ur.reshape(m, width), ui.reshape(m, width), m2, w, bias.reshape(1, width))
    return out.reshape(m, width)


def _xattn_kernel(q_ref, k_ref, v_ref, o_ref):
    s = _dot_nt(q_ref[...], k_ref[...]) * (XATTN_HEAD_DIM ** -0.5)
    p = jnp.exp(s - jnp.max(s, axis=-1, keepdims=True))
    denom = jnp.sum(p, axis=-1, keepdims=True)
    o_ref[...] = (_dot(p.astype(BF16), v_ref[...]) / denom).astype(o_ref.dtype)


def cross_attention_core(q, k, v, batch, seq, mem_tokens, ts=512):
    m, width = q.shape
    hd = XATTN_HEAD_DIM
    heads = width // hd
    ts = min(ts, seq)
    nt = seq // ts
    kv = pl.BlockSpec((mem_tokens, hd), lambda b, h, i: (b, h))
    qs = pl.BlockSpec((ts, hd), lambda b, h, i: (b * nt + i, h))
    return pl.pallas_call(
        _xattn_kernel,
        grid=(batch, heads, nt),
        in_specs=[qs, kv, kv],
        out_specs=qs,
        out_shape=jax.ShapeDtypeStruct((m, width), BF16),
        compiler_params=_cparams(("parallel", "parallel", "parallel")),
        name="xattn",
    )(q, k, v)


def kernel(x, mem, mem_norm_g, pre_mix_g, w_in, ret_norm_g, hgrn_lb_logits, hgrn_norm_g, fourier_w, fourier_b,
           w_out, post_mix_g, pre_xattn_g, xattn_wq, xattn_wk, xattn_wv, xattn_wo, post_xattn_g, pre_ffn_g,
           ffn_w_gate, ffn_w_up, ffn_w_down, post_ffn_g):
    batch, seq, d = x.shape
    depth = w_in.shape[0]
    mem_tokens = mem.shape[1]
    m = batch * seq
    ret_w = ret_norm_g.shape[1]
    hgrn_w = hgrn_norm_g.shape[1]

    h = x.reshape(m, d)
    mem_n = rmsnorm_rows(mem.reshape(batch * mem_tokens, d), mem_norm_g)
    a = rmsnorm_rows(h, pre_mix_g[0])
    for l in range(depth):
        proj = matmul([a], w_in, l, F32, 1024, 512)
        y_ret = retention_mixer(proj, ret_norm_g[l], batch, seq, 0)
        y_hgrn = hgrn2_mixer(proj, hgrn_lb_logits, hgrn_norm_g[l], l, batch, seq, 4 * ret_w)
        y_fft = fourier_mixer(proj, fourier_w[l], fourier_b[l], batch, seq, 4 * ret_w + 5 * hgrn_w)
        mixed = matmul([y_ret, y_hgrn, y_fft], w_out, l, F32, 1024, 512)
        h, c = add_norm(h, mixed, post_mix_g[l], pre_xattn_g[l])
        q = matmul([c], xattn_wq, l, BF16, 1024, 512)
        kx = matmul([mem_n], xattn_wk, l, BF16, 512, 512)
        vx = matmul([mem_n], xattn_wv, l, BF16, 512, 512)
        att = cross_attention_core(q, kx, vx, batch, seq, mem_tokens)
        xa = matmul([att], xattn_wo, l, F32, 1024, 1024)
        h, f = add_norm(h, xa, post_xattn_g[l], pre_ffn_g[l])
        hid = swiglu_matmul(f, ffn_w_gate, ffn_w_up, l, 1024, 256)
        ff = matmul_ktiled(hid, ffn_w_down, l, 1024, 1024, 2048)
        h, a = add_norm(h, ff, post_ffn_g[l], pre_mix_g[l + 1] if l + 1 < depth else None)
    return h.reshape(batch, seq, d)
```

```python
import functools
import math

import numpy as np
import jax
import jax.numpy as jnp
from jax import lax
from jax.experimental import pallas as pl
from jax.experimental.pallas import tpu as pltpu

F32 = jnp.float32
BF16 = jnp.bfloat16

RET_HEAD_DIM = 256
HGRN_HEAD_DIM = 128
FOURIER_GROUP_DIM = 128
XATTN_HEADS = 4
XATTN_HEAD_DIM = 256
ROPE_BASE = 10000.0
RMS_EPS = 1e-6
GN_EPS = 1e-5

LANES = 128
MXU_TILE = 256
V7X_VMEM_BYTES = 64 * 1024 * 1024
VMEM_LIMIT = V7X_VMEM_BYTES - 8 * 1024 * 1024

RET_CHUNK = MXU_TILE
RET_ROWS = 1024
HGRN_ROWS = 512
HGRN_SUB = 128
FFT_N1 = 64
FFT_SUB1 = 8
FFT_SUB2 = 16


def _cparams(sem):
    return pltpu.CompilerParams(dimension_semantics=sem, vmem_limit_bytes=VMEM_LIMIT)


def _tile(n, preferred):
    for t in range(min(preferred, n) // LANES * LANES, 0, -LANES):
        if n % t == 0:
            return t
    return n


def _dot(a, b):
    return jnp.dot(a, b, preferred_element_type=F32)


def _dot_nt(a, b):
    return lax.dot_general(a, b, (((1,), (1,)), ((), ())), preferred_element_type=F32)


def _dot_tn(a, b):
    return lax.dot_general(a, b, (((0,), (0,)), ((), ())), preferred_element_type=F32)


def _split2(x):
    hi = x.astype(BF16)
    return hi, (x - hi.astype(F32)).astype(BF16)


def _const3_lhs(mat):
    hi, lo = _split2(jnp.asarray(mat, F32))
    return jnp.concatenate([hi, hi, lo], axis=1)


def _const3_rhs(mat):
    hi, lo = _split2(jnp.asarray(mat, F32))
    return jnp.concatenate([hi, hi, lo], axis=0)


def _dot3_lhs(m3, x):
    hi, lo = _split2(x)
    return _dot(m3, jnp.concatenate([hi, lo, hi], axis=0))


def _dot3_rhs(x, m3):
    hi, lo = _split2(x)
    return _dot(jnp.concatenate([hi, lo, hi], axis=1), m3)


def _rms(x, g):
    ms = jnp.mean(x * x, axis=-1, keepdims=True)
    return x * lax.rsqrt(ms + RMS_EPS) * g


def _silu(x):
    return x * (1.0 / (1.0 + jnp.exp(-x)))


def _norm_kernel(x_ref, g_ref, o_ref):
    o_ref[...] = _rms(x_ref[...], g_ref[...]).astype(o_ref.dtype)


def rmsnorm_rows(x, g, br=256):
    m, d = x.shape
    out_dtype = BF16
    br = min(br, m)
    return pl.pallas_call(
        _norm_kernel,
        grid=(m // br,),
        in_specs=[pl.BlockSpec((br, d), lambda i: (i, 0)),
                  pl.BlockSpec((1, d), lambda i: (0, 0))],
        out_specs=pl.BlockSpec((br, d), lambda i: (i, 0)),
        out_shape=jax.ShapeDtypeStruct((m, d), out_dtype),
        compiler_params=_cparams(("parallel",)),
        name="rmsnorm_rows",
    )(x, g.reshape(1, d))


def _add_norm_kernel(h_ref, y_ref, gp_ref, gn_ref, ho_ref, a_ref):
    h = h_ref[...] + _rms(y_ref[...], gp_ref[...])
    ho_ref[...] = h
    a_ref[...] = _rms(h, gn_ref[...]).astype(a_ref.dtype)


def _add_norm_last_kernel(h_ref, y_ref, gp_ref, ho_ref):
    ho_ref[...] = h_ref[...] + _rms(y_ref[...], gp_ref[...])


def add_norm(h, y, g_post, g_next, br=256):
    m, d = h.shape
    row = pl.BlockSpec((br, d), lambda i: (i, 0))
    vec = pl.BlockSpec((1, d), lambda i: (0, 0))
    if g_next is None:
        return pl.pallas_call(
            _add_norm_last_kernel, grid=(m // br,),
            in_specs=[row, row, vec], out_specs=row,
            out_shape=jax.ShapeDtypeStruct((m, d), F32),
            compiler_params=_cparams(("parallel",)), name="add_norm_last",
        )(h, y, g_post.reshape(1, d)), None
    return pl.pallas_call(
        _add_norm_kernel, grid=(m // br,),
        in_specs=[row, row, vec, vec], out_specs=[row, row],
        out_shape=[jax.ShapeDtypeStruct((m, d), F32), jax.ShapeDtypeStruct((m, d), BF16)],
        compiler_params=_cparams(("parallel",)), name="add_norm",
    )(h, y, g_post.reshape(1, d), g_next.reshape(1, d))


def _row_resident_spec(bm, kdim):
    return pl.BlockSpec((bm, kdim), lambda i, j: (i, 0), pipeline_mode=pl.Buffered(1))


def _mm_kernel(*refs, k_parts):
    a_refs, w_ref, o_ref = refs[:len(k_parts)], refs[len(k_parts)], refs[-1]
    acc, r0 = None, 0
    for a_ref, kp in zip(a_refs, k_parts):
        part = _dot(a_ref[...], w_ref[r0:r0 + kp, :].astype(BF16))
        acc = part if acc is None else acc + part
        r0 += kp
    o_ref[...] = acc.astype(o_ref.dtype)


def matmul(a_parts, w, layer, out_dtype, bm, bn):
    m = a_parts[0].shape[0]
    k_parts = tuple(a.shape[1] for a in a_parts)
    kdim, n = w.shape[1], w.shape[2]
    assert sum(k_parts) == kdim
    bm, bn = _tile(m, bm), _tile(n, bn)
    return pl.pallas_call(
        functools.partial(_mm_kernel, k_parts=k_parts),
        grid=(m // bm, n // bn),
        in_specs=[_row_resident_spec(bm, kp) for kp in k_parts]
        + [pl.BlockSpec((None, kdim, bn), lambda i, j: (layer, 0, j))],
        out_specs=pl.BlockSpec((bm, bn), lambda i, j: (i, j)),
        out_shape=jax.ShapeDtypeStruct((m, n), out_dtype),
        compiler_params=_cparams(("parallel", "arbitrary")),
        name="matmul",
    )(*a_parts, w)


def _swiglu_kernel(a_ref, wg_ref, wu_ref, o_ref):
    a = a_ref[...]
    gate = _dot(a, wg_ref[...].astype(BF16))
    up = _dot(a, wu_ref[...].astype(BF16))
    o_ref[...] = (_silu(gate) * up).astype(o_ref.dtype)


def swiglu_matmul(a, wg, wu, layer, bm, bn):
    m, kdim = a.shape
    n = wg.shape[2]
    bm, bn = _tile(m, bm), _tile(n, bn)
    wspec = pl.BlockSpec((None, kdim, bn), lambda i, j: (layer, 0, j))
    return pl.pallas_call(
        _swiglu_kernel,
        grid=(m // bm, n // bn),
        in_specs=[_row_resident_spec(bm, kdim), wspec, wspec],
        out_specs=pl.BlockSpec((bm, bn), lambda i, j: (i, j)),
        out_shape=jax.ShapeDtypeStruct((m, n), BF16),
        compiler_params=_cparams(("parallel", "arbitrary")),
        name="swiglu_matmul",
    )(a, wg, wu)


def _mm_ktiled_kernel(*refs, n_tail):
    a_ref, w_ref = refs[0], refs[1]
    t_refs, wt_ref, o_ref = refs[2:2 + n_tail], refs[2 + n_tail], refs[-1]
    k = pl.program_id(2)

    @pl.when(k == 0)
    def _():
        tail = jnp.concatenate([t[...] for t in t_refs], axis=1)
        o_ref[...] = _dot(a_ref[...], w_ref[...].astype(BF16)) + _dot(tail, wt_ref[...].astype(BF16))

    @pl.when(k > 0)
    def _():
        o_ref[...] += _dot(a_ref[...], w_ref[...].astype(BF16))


def matmul_ktiled(a, w, layer, bm, bn, bk):
    m, kdim = a.shape
    n = w.shape[2]
    bm, bn = _tile(m, bm), _tile(n, bn)
    nk = kdim // bk
    kmain = nk * bk
    ktail = kdim - kmain
    assert nk >= 1 and ktail > 0 and ktail % MXU_TILE == 0 and kmain % MXU_TILE == 0
    n_tail = ktail // MXU_TILE
    tb0 = kmain // MXU_TILE
    w_tail = w[layer, kmain:, :]
    tail_specs = [pl.BlockSpec((bm, MXU_TILE), functools.partial(lambda i, j, k, c: (i, c), c=tb0 + c))
                  for c in range(n_tail)]
    return pl.pallas_call(
        functools.partial(_mm_ktiled_kernel, n_tail=n_tail),
        grid=(m // bm, n // bn, nk),
        in_specs=[pl.BlockSpec((bm, bk), lambda i, j, k: (i, k)),
                  pl.BlockSpec((None, bk, bn), lambda i, j, k: (layer, k, j))]
        + tail_specs + [pl.BlockSpec((ktail, bn), lambda i, j, k: (0, j))],
        out_specs=pl.BlockSpec((bm, bn), lambda i, j, k: (i, j)),
        out_shape=jax.ShapeDtypeStruct((m, n), F32),
        compiler_params=_cparams(("parallel", "parallel", "arbitrary")),
        name="matmul_ktiled",
    )(a, w, *([a] * n_tail), w_tail)


def _ret_kernel(lg_ref, cos_ref, sin_ref, q_ref, k_ref, v_ref, g_ref, ng_ref, o_ref,
                sf_ref, sb_ref, ob_ref, qr_ref, kr_ref, dec_ref, vec_ref, *, nb, rows, c):
    t = pl.program_id(2)
    half = RET_HEAD_DIM // 2
    nch = rows // c

    @pl.when(t == 0)
    def _():
        lgf = lg_ref[0, 0:1, :]
        lgb = lg_ref[0, 1:2, :]
        i_row = lax.broadcasted_iota(jnp.int32, (c, half), 0).astype(F32)
        rel = (lax.broadcasted_iota(jnp.int32, (c, c), 0) - lax.broadcasted_iota(jnp.int32, (c, c), 1)).astype(F32)
        dec_ref[...] = jnp.exp(jnp.where(rel >= 0, rel * lgf[:, :1], -rel * lgb[:, :1]))
        vec_ref[0] = jnp.exp((i_row + 1.0) * lgf)
        vec_ref[1] = jnp.exp((c - 1.0 - i_row) * lgf)
        vec_ref[2] = jnp.exp((c - i_row) * lgb)
        vec_ref[3] = jnp.exp(i_row * lgb)
        vec_ref[4] = jnp.exp(c * lgf) + jnp.zeros((c, half), F32)
        vec_ref[5] = jnp.exp(c * lgb) + jnp.zeros((c, half), F32)
        sf_ref[...] = jnp.zeros_like(sf_ref)
        sb_ref[...] = jnp.zeros_like(sb_ref)

    def lanes2(d):
        return jnp.concatenate([d, d], axis=-1)

    @pl.when(t < nb)
    def _():
        base = (nb - 1 - t) * rows
        for ci in range(nch - 1, -1, -1):
            r0 = ci * c
            g0 = pl.multiple_of(base + r0, c)
            cos = cos_ref[r0:r0 + c, :]
            sin = sin_ref[r0:r0 + c, :]

            def rot(x):
                x1, x2 = x[:, :half], x[:, half:]
                return jnp.concatenate([x1 * cos - x2 * sin, x1 * sin + x2 * cos], axis=-1)

            q = rot(q_ref[r0:r0 + c, :]) * (RET_HEAD_DIM ** -0.5)
            k = rot(k_ref[r0:r0 + c, :])
            qr_ref[pl.ds(g0, c), :] = q
            kr_ref[pl.ds(g0, c), :] = k
            vb = v_ref[r0:r0 + c, :].astype(BF16)
            sb = sb_ref[...]
            ob_ref[pl.ds(g0, c), :] = _dot((q * lanes2(vec_ref[2])).astype(BF16), sb.astype(BF16))
            kd = k * lanes2(vec_ref[3])
            sb_ref[...] = vec_ref[5][0:1, 0:1] * sb + _dot_tn(kd.astype(BF16), vb)

    @pl.when(t >= nb)
    def _():
        base = (t - nb) * rows
        for ci in range(nch):
            r0 = ci * c
            g0 = pl.multiple_of(base + r0, c)
            q = qr_ref[pl.ds(g0, c), :]
            k = kr_ref[pl.ds(g0, c), :]
            vb = v_ref[r0:r0 + c, :].astype(BF16)
            sf = sf_ref[...]
            scores = _dot_nt(q.astype(BF16), k.astype(BF16)) * dec_ref[...]
            o = (_dot(scores.astype(BF16), vb) + _dot((q * lanes2(vec_ref[0])).astype(BF16), sf.astype(BF16))
                 + ob_ref[pl.ds(g0, c), :])
            kd = k * lanes2(vec_ref[1])
            sf_ref[...] = vec_ref[4][0:1, 0:1] * sf + _dot_tn(kd.astype(BF16), vb)
            mu = jnp.mean(o, axis=-1, keepdims=True)
            oc = o - mu
            var = jnp.mean(oc * oc, axis=-1, keepdims=True)
            y = oc * lax.rsqrt(var + GN_EPS) * ng_ref[...]
            o_ref[r0:r0 + c, :] = (_silu(g_ref[r0:r0 + c, :]) * y).astype(o_ref.dtype)


def retention_mixer(proj, norm_g, batch, seq, col0):
    m = proj.shape[0]
    width = norm_g.shape[0]
    heads = width // RET_HEAD_DIM
    c = min(RET_CHUNK, seq)
    rows = min(RET_ROWS, seq)
    nb = seq // rows
    half = RET_HEAD_DIM // 2
    cb0 = col0 // RET_HEAD_DIM

    hidx = jnp.arange(heads, dtype=F32)
    gamma = 1.0 - jnp.power(2.0, -5.0 - hidx)
    lg = jnp.log(gamma)
    lg2 = jnp.broadcast_to(jnp.stack([lg, lg[::-1]], axis=1)[:, :, None], (heads, 2, LANES))
    inv = jnp.power(ROPE_BASE, -jnp.arange(half, dtype=F32) / half)
    ang = jnp.arange(seq, dtype=F32)[:, None] * inv[None, :]
    cos, sin = jnp.cos(ang), jnp.sin(ang)

    def bblk(t):
        return jnp.maximum(nb - 1 - t, 0)

    def fblk(t):
        return jnp.maximum(t - nb, 0)

    def blk(t):
        return jnp.where(t < nb, nb - 1 - t, t - nb)

    def spec(j, which):
        return pl.BlockSpec((rows, RET_HEAD_DIM), lambda b, h, t: (b * nb + which(t), cb0 + j * heads + h))

    return pl.pallas_call(
        functools.partial(_ret_kernel, nb=nb, rows=rows, c=c),
        grid=(batch, heads, 2 * nb),
        in_specs=[pl.BlockSpec((1, 2, LANES), lambda b, h, t: (h, 0, 0)),
                  pl.BlockSpec((rows, half), lambda b, h, t: (bblk(t), 0)),
                  pl.BlockSpec((rows, half), lambda b, h, t: (bblk(t), 0)),
                  spec(0, bblk), spec(1, bblk), spec(2, blk), spec(3, fblk),
                  pl.BlockSpec((1, RET_HEAD_DIM), lambda b, h, t: (0, h))],
        out_specs=pl.BlockSpec((rows, RET_HEAD_DIM), lambda b, h, t: (b * nb + fblk(t), h)),
        out_shape=jax.ShapeDtypeStruct((m, width), BF16),
        scratch_shapes=[pltpu.VMEM((RET_HEAD_DIM, RET_HEAD_DIM), F32),
                        pltpu.VMEM((RET_HEAD_DIM, RET_HEAD_DIM), F32),
                        pltpu.VMEM((seq, RET_HEAD_DIM), F32),
                        pltpu.VMEM((seq, RET_HEAD_DIM), F32),
                        pltpu.VMEM((seq, RET_HEAD_DIM), F32),
                        pltpu.VMEM((c, c), F32),
                        pltpu.VMEM((6, c, half), F32)],
        compiler_params=_cparams(("parallel", "parallel", "arbitrary")),
        name="retention",
    )(lg2, cos, sin, proj, proj, proj, proj, norm_g.reshape(1, width))


def _hgrn_kernel(lbl_ref, lvl_ref, up_ref, sg_ref, q_ref, z_ref, v_ref, g_ref, ng_ref, o_ref,
                 st_ref, ob_ref, *, nb, rows, sub, layer, depth):
    t = pl.program_id(2)
    d = HGRN_HEAD_DIM
    nlev = sub.bit_length() - 1

    def lower_bound(dirn):
        x = [lbl_ref[2 * l + dirn: 2 * l + dirn + 1, :] for l in range(depth)]
        mx = functools.reduce(jnp.maximum, x)
        e = [jnp.exp(xi - mx) for xi in x]
        tot = functools.reduce(lambda a, b: a + b, e)
        p = [ei / tot for ei in e]
        return functools.reduce(lambda a, b: a + b, p[:layer + 1]) - p[0]

    row = lax.broadcasted_iota(jnp.int32, (sub, d), 0)
    r_i = lax.broadcasted_iota(jnp.int32, (sub, sub), 0)
    c_i = lax.broadcasted_iota(jnp.int32, (sub, sub), 1)

    def split3(x):
        hi = x.astype(BF16)
        r1 = x - hi.astype(F32)
        mid = r1.astype(BF16)
        lo = (r1 - mid.astype(F32)).astype(BF16)
        return jnp.concatenate([hi, mid, lo], axis=1)

    def group_rows(x, size, pick):
        parts = [jnp.broadcast_to(x[base + pick:base + pick + 1, :], (size, d)) for base in range(0, sub, size)]
        return parts[0] if len(parts) == 1 else jnp.concatenate(parts, axis=0)

    def run(r0, rev, lb):
        f = lb + (1.0 - lb) * (1.0 / (1.0 + jnp.exp(-z_ref[r0:r0 + sub, :])))
        kk = 1.0 - f
        lf = jnp.log(f)
        q = q_ref[r0:r0 + sub, :] * (d ** -0.5)
        vb = v_ref[r0:r0 + sub, :].astype(BF16)

        tri = ((c_i >= r_i) if rev else (c_i <= r_i)).astype(BF16)
        parts = _dot(tri, split3(lf))
        cum = parts[:, :d] + parts[:, d:2 * d] + parts[:, 2 * d:]
        edge = cum[0:1, :] if rev else cum[sub - 1:sub, :]

        st = st_ref[...]
        o = _dot_nt((q * jnp.exp(cum)).astype(BF16), st.astype(BF16))
        kd = kk * jnp.exp(edge - cum)
        st_ref[...] = st * jnp.exp(edge) + _dot_tn(vb, kd.astype(BF16))

        lvl = lvl_ref[1 if rev else 0]
        a = jnp.where(lvl == nlev, jnp.sum(q * kk, axis=-1, keepdims=True), 0.0)
        lf_up = pltpu.roll(lf, sub - 1, 0)
        lf_dn = pltpu.roll(lf, 1, 0)
        qk = q - kk
        for lev in range(nlev):
            m = 1 << lev
            if m <= 2:
                upper = (row & m) != 0
                target = jnp.logical_not(upper) if rev else upper
                if m == 1:
                    dlt = jnp.where(upper, 0.0, lf) if rev else jnp.where(upper, lf, 0.0)
                else:
                    r4 = row & 3
                    if rev:
                        dlt = jnp.where(r4 == 0, lf + lf_up, jnp.where(r4 == 1, lf, jnp.where(r4 == 2, 0.0, lf_dn)))
                    else:
                        dlt = jnp.where(r4 == 0, lf_up, jnp.where(r4 == 1, 0.0, jnp.where(r4 == 2, lf, lf + lf_dn)))
                y = jnp.where(target, q, kk) * jnp.exp(dlt)
            else:
                up = up_ref[lev]
                sg = sg_ref[lev]
                bound = group_rows(cum, 2 * m, m if rev else m - 1)
                if rev:
                    y = (q - up * qk) * jnp.exp2((bound - cum) * sg)
                else:
                    y = (kk + up * qk) * jnp.exp2((cum - bound) * sg)
            yb = y.astype(BF16)
            a = jnp.where(lvl == lev, _dot_nt(yb, yb), a)
        return o + _dot(a.astype(BF16), vb)

    nsb = rows // sub

    @pl.when(t == 0)
    def _():
        st_ref[...] = jnp.zeros_like(st_ref)

    @pl.when(t < nb)
    def _():
        lb = lower_bound(1)
        base = (nb - 1 - t) * rows
        for si in range(nsb - 1, -1, -1):
            ob_ref[pl.ds(pl.multiple_of(base + si * sub, sub), sub), :] = run(si * sub, True, lb)

    @pl.when(t == nb)
    def _():
        st_ref[...] = jnp.zeros_like(st_ref)

    @pl.when(t >= nb)
    def _():
        lb = lower_bound(0)
        base = (t - nb) * rows
        for si in range(nsb):
            r0 = si * sub
            o = run(r0, False, lb) + ob_ref[pl.ds(pl.multiple_of(base + r0, sub), sub), :]
            o = o * lax.rsqrt(jnp.mean(o * o, axis=-1, keepdims=True) + RMS_EPS)
            o_ref[r0:r0 + sub, :] = (_silu(g_ref[r0:r0 + sub, :]) * (o * ng_ref[...])).astype(o_ref.dtype)


def hgrn2_mixer(proj, lb_logits, norm_g, layer, batch, seq, col0):
    m = proj.shape[0]
    width = norm_g.shape[0]
    d = HGRN_HEAD_DIM
    heads = width // d
    depth = lb_logits.shape[0]
    rows = min(HGRN_ROWS, seq)
    nb = seq // rows
    cb0 = col0 // d

    def blk(t):
        return jnp.where(t < nb, nb - 1 - t, t - nb)

    def fblk(t):
        return jnp.maximum(t - nb, 0)

    def spec(j):
        return pl.BlockSpec((rows, d), lambda b, h, t: (b * nb + blk(t), cb0 + j * heads + h))

    zspec = pl.BlockSpec((rows, d),
                         lambda b, h, t: (b * nb + blk(t), cb0 + jnp.where(t < nb, 2, 1) * heads + h))
    sub = min(HGRN_SUB, rows)
    assert sub & (sub - 1) == 0 and rows % sub == 0
    nlev = sub.bit_length() - 1
    idx = np.arange(sub)
    x = idx[:, None] ^ idx[None, :]
    lev = np.where(x > 0, np.floor(np.log2(np.maximum(x, 1))), nlev).astype(np.int32)
    lvl = jnp.asarray(np.stack([np.where(idx[:, None] >= idx[None, :], lev, -1),
                                np.where(idx[:, None] <= idx[None, :], lev, -1)]), jnp.int32)
    upper = np.stack([((idx >> l) & 1).astype(np.float32) for l in range(nlev)])
    up = jnp.broadcast_to(jnp.asarray(upper)[:, :, None], (nlev, sub, d))
    sg = jnp.broadcast_to(jnp.asarray((2.0 * upper - 1.0) * np.float32(math.log2(math.e)))[:, :, None],
                          (nlev, sub, d))
    tab = pl.BlockSpec((nlev, sub, d), lambda b, h, t: (0, 0, 0))
    return pl.pallas_call(
        functools.partial(_hgrn_kernel, nb=nb, rows=rows, sub=sub, layer=layer, depth=depth),
        grid=(batch, heads, 2 * nb),
        in_specs=[pl.BlockSpec((2 * depth, d), lambda b, h, t: (0, h)),
                  pl.BlockSpec((2, sub, sub), lambda b, h, t: (0, 0, 0)), tab, tab,
                  spec(0), zspec, spec(3),
                  pl.BlockSpec((rows, d), lambda b, h, t: (b * nb + fblk(t), cb0 + 4 * heads + h)),
                  pl.BlockSpec((1, d), lambda b, h, t: (0, h))],
        out_specs=pl.BlockSpec((rows, d), lambda b, h, t: (b * nb + fblk(t), h)),
        out_shape=jax.ShapeDtypeStruct((m, width), BF16),
        scratch_shapes=[pltpu.VMEM((d, d), F32),
                        pltpu.VMEM((seq, d), F32)],
        compiler_params=_cparams(("parallel", "parallel", "arbitrary")),
        name="hgrn2",
    )(lb_logits.reshape(2 * depth, width), lvl, up, sg, proj, proj, proj, proj, norm_g.reshape(1, width))


def _dft_cos_sin(n):
    idx = np.arange(n)
    ang = 2.0 * np.pi * ((idx[:, None] * idx[None, :]) % n) / n
    return np.cos(ang), np.sin(ang)


def _fft_chan_kernel(z_ref, m_ref, vr_ref, vi_ref, *, groups):
    gd = FOURIER_GROUP_DIM
    mat = m_ref[...]
    for g in range(groups):
        pq = _dot3_rhs(z_ref[:, g * gd:(g + 1) * gd], mat)
        vr_ref[:, g * gd:(g + 1) * gd] = pq[:, :gd]
        vi_ref[:, g * gd:(g + 1) * gd] = pq[:, gd:]


def _fft_stage1_kernel(vr_ref, vi_ref, m_ref, twc_ref, tws_ref, ur_ref, ui_ref, *, n1, reps, nsub):
    for bi in range(nsub):
        x = jnp.concatenate([vr_ref[:, bi, :], vi_ref[:, bi, :]], axis=0)
        tt = _dot3_lhs(m_ref[...], x)
        tr, ti = tt[:n1], tt[n1:]
        cw = jnp.tile(twc_ref[bi], (1, reps))
        sw = jnp.tile(tws_ref[bi], (1, reps))
        ur_ref[:, bi, :] = tr * cw + ti * sw
        ui_ref[:, bi, :] = ti * cw - tr * sw


def _fft_stage2_kernel(ur_ref, ui_ref, m_ref, w_ref, b_ref, o_ref, y_ref, *, groups, scale, n2, nsub):
    gd = FOURIER_GROUP_DIM
    for ki in range(nsub):
        x = jnp.concatenate([ur_ref[ki * n2:(ki + 1) * n2, :], ui_ref[ki * n2:(ki + 1) * n2, :]], axis=0)
        spec = _dot3_lhs(m_ref[...], x) * scale
        for g in range(groups):
            y = _dot(spec[:, g * gd:(g + 1) * gd].astype(BF16), w_ref[g].astype(BF16))
            y_ref[:, ki, g * gd:(g + 1) * gd] = y + b_ref[:, g * gd:(g + 1) * gd]
    o_ref[...] = y_ref[...].astype(o_ref.dtype)


def fourier_mixer(proj, w, bias, batch, seq, col0):
    m = proj.shape[0]
    groups, gd, _ = w.shape
    width = groups * gd
    n1 = min(FFT_N1, seq)
    n2 = seq // n1
    br = min(512, m)

    cc, sc = _dft_cos_sin(gd)
    chan = _const3_rhs(np.concatenate([cc, -sc], axis=1))
    vr, vi = pl.pallas_call(
        functools.partial(_fft_chan_kernel, groups=groups),
        grid=(m // br,),
        in_specs=[pl.BlockSpec((br, width), lambda i: (i, col0 // width)),
                  pl.BlockSpec((3 * gd, 2 * gd), lambda i: (0, 0))],
        out_specs=[pl.BlockSpec((br, width), lambda i: (i, 0))] * 2,
        out_shape=[jax.ShapeDtypeStruct((m, width), F32)] * 2,
        compiler_params=_cparams(("parallel",)),
        name="fft_channels",
    )(proj, chan)

    c1, s1 = _dft_cos_sin(n1)
    m1 = _const3_lhs(np.block([[c1, s1], [-s1, c1]]))
    k1 = np.arange(n1)
    bb = np.arange(n2)
    tw = 2.0 * np.pi * ((bb[:, None] * k1[None, :]) % seq) / seq
    twc = jnp.broadcast_to(jnp.asarray(np.cos(tw), F32)[:, :, None], (n2, n1, LANES))
    tws = jnp.broadcast_to(jnp.asarray(np.sin(tw), F32)[:, :, None], (n2, n1, LANES))
    s1 = min(FFT_SUB1, n2)
    vspec = pl.BlockSpec((n1, s1, width), lambda bt, j: (bt, j, 0))
    tspec = pl.BlockSpec((s1, n1, LANES), lambda bt, j: (j, 0, 0))
    ur, ui = pl.pallas_call(
        functools.partial(_fft_stage1_kernel, n1=n1, reps=width // LANES, nsub=s1),
        grid=(batch, n2 // s1),
        in_specs=[vspec, vspec, pl.BlockSpec((2 * n1, 6 * n1), lambda bt, j: (0, 0)), tspec, tspec],
        out_specs=[vspec, vspec],
        out_shape=[jax.ShapeDtypeStruct((batch * n1, n2, width), F32)] * 2,
        compiler_params=_cparams(("parallel", "parallel")),
        name="fft_stage1",
    )(vr.reshape(batch * n1, n2, width), vi.reshape(batch * n1, n2, width), m1, twc, tws)

    c2, s2 = _dft_cos_sin(n2)
    m2 = _const3_lhs(np.concatenate([c2, s2], axis=1))
    s2 = min(FFT_SUB2, n1)
    uspec = pl.BlockSpec((s2 * n2, width), lambda bt, j: (bt * (n1 // s2) + j, 0))
    out = pl.pallas_call(
        functools.partial(_fft_stage2_kernel, groups=groups, scale=1.0 / math.sqrt(seq * gd), n2=n2, nsub=s2),
        grid=(batch, n1 // s2),
        in_specs=[uspec, uspec, pl.BlockSpec((n2, 6 * n2), lambda bt, j: (0, 0)),
                  pl.BlockSpec((groups, gd, gd), lambda bt, j: (0, 0, 0)),
                  pl.BlockSpec((1, width), lambda bt, j: (0, 0))],
        out_specs=pl.BlockSpec((n2, s2, width), lambda bt, j: (bt, j, 0)),
        out_shape=jax.ShapeDtypeStruct((batch * n2, n1, width), BF16),
        scratch_shapes=[pltpu.VMEM((n2, s2, width), F32)],
        compiler_params=_cparams(("parallel", "parallel")),
        name="fft_stage2",
    )(ur.reshape(m, width), ui.reshape(m, width), m2, w, bias.reshape(1, width))
    return out.reshape(m, width)


def _xattn_kernel(q_ref, k_ref, v_ref, o_ref):
    s = _dot_nt(q_ref[...], k_ref[...]) * (XATTN_HEAD_DIM ** -0.5)
    p = jnp.exp(s - jnp.max(s, axis=-1, keepdims=True))
    denom = jnp.sum(p, axis=-1, keepdims=True)
    o_ref[...] = (_dot(p.astype(BF16), v_ref[...]) / denom).astype(o_ref.dtype)


def cross_attention_core(q, k, v, batch, seq, mem_tokens, ts=512):
    m, width = q.shape
    hd = XATTN_HEAD_DIM
    heads = width // hd
    ts = min(ts, seq)
    nt = seq // ts
    kv = pl.BlockSpec((mem_tokens, hd), lambda b, h, i: (b, h))
    qs = pl.BlockSpec((ts, hd), lambda b, h, i: (b * nt + i, h))
    return pl.pallas_call(
        _xattn_kernel,
        grid=(batch, heads, nt),
        in_specs=[qs, kv, kv],
        out_specs=qs,
        out_shape=jax.ShapeDtypeStruct((m, width), BF16),
        compiler_params=_cparams(("parallel", "parallel", "parallel")),
        name="xattn",
    )(q, k, v)


def kernel(x, mem, mem_norm_g, pre_mix_g, w_in, ret_norm_g, hgrn_lb_logits, hgrn_norm_g, fourier_w, fourier_b,
           w_out, post_mix_g, pre_xattn_g, xattn_wq, xattn_wk, xattn_wv, xattn_wo, post_xattn_g, pre_ffn_g,
           ffn_w_gate, ffn_w_up, ffn_w_down, post_ffn_g):
    batch, seq, d = x.shape
    depth = w_in.shape[0]
    mem_tokens = mem.shape[1]
    m = batch * seq
    ret_w = ret_norm_g.shape[1]
    hgrn_w = hgrn_norm_g.shape[1]

    h = x.reshape(m, d)
    mem_n = rmsnorm_rows(mem.reshape(batch * mem_tokens, d), mem_norm_g)
    a = rmsnorm_rows(h, pre_mix_g[0])
    for l in range(depth):
        proj = matmul([a], w_in, l, F32, 2048, 512)
        y_ret = retention_mixer(proj, ret_norm_g[l], batch, seq, 0)
        y_hgrn = hgrn2_mixer(proj, hgrn_lb_logits, hgrn_norm_g[l], l, batch, seq, 4 * ret_w)
        y_fft = fourier_mixer(proj, fourier_w[l], fourier_b[l], batch, seq, 4 * ret_w + 5 * hgrn_w)
        mixed = matmul([y_ret, y_hgrn, y_fft], w_out, l, F32, 2048, 512)
        h, c = add_norm(h, mixed, post_mix_g[l], pre_xattn_g[l])
        q = matmul([c], xattn_wq, l, BF16, 2048, 512)
        kx = matmul([mem_n], xattn_wk, l, BF16, 512, 512)
        vx = matmul([mem_n], xattn_wv, l, BF16, 512, 512)
        att = cross_attention_core(q, kx, vx, batch, seq, mem_tokens)
        xa = matmul([att], xattn_wo, l, F32, 2048, 1024)
        h, f = add_norm(h, xa, post_xattn_g[l], pre_ffn_g[l])
        hid = swiglu_matmul(f, ffn_w_gate, ffn_w_up, l, 2048, 256)
        ff = matmul_ktiled(hid, ffn_w_down, l, 1024, 1024, 2048)
        h, a = add_norm(h, ff, post_ffn_g[l], pre_mix_g[l + 1] if l + 1 < depth else None)
    return h.reshape(batch, seq, d)
```

```python
import functools
import math

import numpy as np
import jax
import jax.numpy as jnp
from jax import lax
from jax.experimental import pallas as pl
from jax.experimental.pallas import tpu as pltpu

F32 = jnp.float32
BF16 = jnp.bfloat16

RET_HEAD_DIM = 256
HGRN_HEAD_DIM = 128
FOURIER_GROUP_DIM = 128
XATTN_HEADS = 4
XATTN_HEAD_DIM = 256
ROPE_BASE = 10000.0
RMS_EPS = 1e-6
GN_EPS = 1e-5

LANES = 128
MXU_TILE = 256
V7X_VMEM_BYTES = 64 * 1024 * 1024
VMEM_LIMIT = V7X_VMEM_BYTES - 8 * 1024 * 1024

RET_CHUNK = MXU_TILE
RET_ROWS = 1024
HGRN_ROWS = 512
HGRN_SUB = 128
XATTN_ROWS = 256
FFT_N1 = 64
FFT_SUB1 = 8
FFT_SUB2 = 16


def _cparams(sem):
    return pltpu.CompilerParams(dimension_semantics=sem, vmem_limit_bytes=VMEM_LIMIT)


def _tile(n, preferred):
    for t in range(min(preferred, n) // LANES * LANES, 0, -LANES):
        if n % t == 0:
            return t
    return n


def _dot(a, b):
    return jnp.dot(a, b, preferred_element_type=F32)


def _dot_nt(a, b):
    return lax.dot_general(a, b, (((1,), (1,)), ((), ())), preferred_element_type=F32)


def _dot_tn(a, b):
    return lax.dot_general(a, b, (((0,), (0,)), ((), ())), preferred_element_type=F32)


def _split2(x):
    hi = x.astype(BF16)
    return hi, (x - hi.astype(F32)).astype(BF16)


def _const3_lhs(mat):
    hi, lo = _split2(jnp.asarray(mat, F32))
    return jnp.concatenate([hi, hi, lo], axis=1)


def _const3_rhs(mat):
    hi, lo = _split2(jnp.asarray(mat, F32))
    return jnp.concatenate([hi, hi, lo], axis=0)


def _dot3_lhs(m3, x):
    hi, lo = _split2(x)
    return _dot(m3, jnp.concatenate([hi, lo, hi], axis=0))


def _dot3_rhs(x, m3):
    hi, lo = _split2(x)
    return _dot(jnp.concatenate([hi, lo, hi], axis=1), m3)


def _rms(x, g):
    ms = jnp.mean(x * x, axis=-1, keepdims=True)
    return x * lax.rsqrt(ms + RMS_EPS) * g


def _silu(x):
    return x * (1.0 / (1.0 + jnp.exp(-x)))


def _norm_kernel(x_ref, g_ref, o_ref):
    o_ref[...] = _rms(x_ref[...], g_ref[...]).astype(o_ref.dtype)


def rmsnorm_rows(x, g, br=256):
    m, d = x.shape
    out_dtype = BF16
    br = min(br, m)
    return pl.pallas_call(
        _norm_kernel,
        grid=(m // br,),
        in_specs=[pl.BlockSpec((br, d), lambda i: (i, 0)),
                  pl.BlockSpec((1, d), lambda i: (0, 0))],
        out_specs=pl.BlockSpec((br, d), lambda i: (i, 0)),
        out_shape=jax.ShapeDtypeStruct((m, d), out_dtype),
        compiler_params=_cparams(("parallel",)),
        name="rmsnorm_rows",
    )(x, g.reshape(1, d))


def _add_norm_kernel(h_ref, y_ref, gp_ref, gn_ref, ho_ref, a_ref):
    h = h_ref[...] + _rms(y_ref[...], gp_ref[...])
    ho_ref[...] = h
    a_ref[...] = _rms(h, gn_ref[...]).astype(a_ref.dtype)


def _add_norm_last_kernel(h_ref, y_ref, gp_ref, ho_ref):
    ho_ref[...] = h_ref[...] + _rms(y_ref[...], gp_ref[...])


def add_norm(h, y, g_post, g_next, br=256):
    m, d = h.shape
    row = pl.BlockSpec((br, d), lambda i: (i, 0))
    vec = pl.BlockSpec((1, d), lambda i: (0, 0))
    if g_next is None:
        return pl.pallas_call(
            _add_norm_last_kernel, grid=(m // br,),
            in_specs=[row, row, vec], out_specs=row,
            out_shape=jax.ShapeDtypeStruct((m, d), F32),
            compiler_params=_cparams(("parallel",)), name="add_norm_last",
        )(h, y, g_post.reshape(1, d)), None
    return pl.pallas_call(
        _add_norm_kernel, grid=(m // br,),
        in_specs=[row, row, vec, vec], out_specs=[row, row],
        out_shape=[jax.ShapeDtypeStruct((m, d), F32), jax.ShapeDtypeStruct((m, d), BF16)],
        compiler_params=_cparams(("parallel",)), name="add_norm",
    )(h, y, g_post.reshape(1, d), g_next.reshape(1, d))


def _row_resident_spec(bm, kdim):
    return pl.BlockSpec((bm, kdim), lambda i, j: (i, 0), pipeline_mode=pl.Buffered(1))


def _mm_kernel(*refs, k_parts):
    a_refs, w_ref, o_ref = refs[:len(k_parts)], refs[len(k_parts)], refs[-1]
    acc, r0 = None, 0
    for a_ref, kp in zip(a_refs, k_parts):
        part = _dot(a_ref[...], w_ref[r0:r0 + kp, :].astype(BF16))
        acc = part if acc is None else acc + part
        r0 += kp
    o_ref[...] = acc.astype(o_ref.dtype)


def matmul(a_parts, w, layer, out_dtype, bm, bn):
    m = a_parts[0].shape[0]
    k_parts = tuple(a.shape[1] for a in a_parts)
    kdim, n = w.shape[1], w.shape[2]
    assert sum(k_parts) == kdim
    bm, bn = _tile(m, bm), _tile(n, bn)
    return pl.pallas_call(
        functools.partial(_mm_kernel, k_parts=k_parts),
        grid=(m // bm, n // bn),
        in_specs=[_row_resident_spec(bm, kp) for kp in k_parts]
        + [pl.BlockSpec((None, kdim, bn), lambda i, j: (layer, 0, j))],
        out_specs=pl.BlockSpec((bm, bn), lambda i, j: (i, j)),
        out_shape=jax.ShapeDtypeStruct((m, n), out_dtype),
        compiler_params=_cparams(("parallel", "arbitrary")),
        name="matmul",
    )(*a_parts, w)


def _swiglu_kernel(a_ref, wg_ref, wu_ref, o_ref):
    a = a_ref[...]
    gate = _dot(a, wg_ref[...].astype(BF16))
    up = _dot(a, wu_ref[...].astype(BF16))
    o_ref[...] = (_silu(gate) * up).astype(o_ref.dtype)


def swiglu_matmul(a, wg, wu, layer, bm, bn):
    m, kdim = a.shape
    n = wg.shape[2]
    bm, bn = _tile(m, bm), _tile(n, bn)
    wspec = pl.BlockSpec((None, kdim, bn), lambda i, j: (layer, 0, j))
    return pl.pallas_call(
        _swiglu_kernel,
        grid=(m // bm, n // bn),
        in_specs=[_row_resident_spec(bm, kdim), wspec, wspec],
        out_specs=pl.BlockSpec((bm, bn), lambda i, j: (i, j)),
        out_shape=jax.ShapeDtypeStruct((m, n), BF16),
        compiler_params=_cparams(("parallel", "arbitrary")),
        name="swiglu_matmul",
    )(a, wg, wu)


def _mm_ktiled_kernel(*refs, n_tail):
    a_ref, w_ref = refs[0], refs[1]
    t_refs, wt_ref, o_ref = refs[2:2 + n_tail], refs[2 + n_tail], refs[-1]
    k = pl.program_id(2)

    @pl.when(k == 0)
    def _():
        tail = jnp.concatenate([t[...] for t in t_refs], axis=1)
        o_ref[...] = _dot(a_ref[...], w_ref[...].astype(BF16)) + _dot(tail, wt_ref[...].astype(BF16))

    @pl.when(k > 0)
    def _():
        o_ref[...] += _dot(a_ref[...], w_ref[...].astype(BF16))


def matmul_ktiled(a, w, layer, bm, bn, bk):
    m, kdim = a.shape
    n = w.shape[2]
    bm, bn = _tile(m, bm), _tile(n, bn)
    nk = kdim // bk
    kmain = nk * bk
    ktail = kdim - kmain
    assert nk >= 1 and ktail > 0 and ktail % MXU_TILE == 0 and kmain % MXU_TILE == 0
    n_tail = ktail // MXU_TILE
    tb0 = kmain // MXU_TILE
    w_tail = w[layer, kmain:, :]
    tail_specs = [pl.BlockSpec((bm, MXU_TILE), functools.partial(lambda i, j, k, c: (i, c), c=tb0 + c))
                  for c in range(n_tail)]
    return pl.pallas_call(
        functools.partial(_mm_ktiled_kernel, n_tail=n_tail),
        grid=(m // bm, n // bn, nk),
        in_specs=[pl.BlockSpec((bm, bk), lambda i, j, k: (i, k)),
                  pl.BlockSpec((None, bk, bn), lambda i, j, k: (layer, k, j))]
        + tail_specs + [pl.BlockSpec((ktail, bn), lambda i, j, k: (0, j))],
        out_specs=pl.BlockSpec((bm, bn), lambda i, j, k: (i, j)),
        out_shape=jax.ShapeDtypeStruct((m, n), F32),
        compiler_params=_cparams(("parallel", "parallel", "arbitrary")),
        name="matmul_ktiled",
    )(a, w, *([a] * n_tail), w_tail)


def _ret_kernel(lg_ref, cos_ref, sin_ref, q_ref, k_ref, v_ref, g_ref, ng_ref, o_ref,
                sf_ref, sb_ref, ob_ref, qr_ref, kr_ref, dec_ref, vec_ref, *, nb, rows, c):
    t = pl.program_id(2)
    half = RET_HEAD_DIM // 2
    nch = rows // c

    @pl.when(t == 0)
    def _():
        lgf = lg_ref[0, 0:1, :]
        lgb = lg_ref[0, 1:2, :]
        i_row = lax.broadcasted_iota(jnp.int32, (c, half), 0).astype(F32)
        rel = (lax.broadcasted_iota(jnp.int32, (c, c), 0) - lax.broadcasted_iota(jnp.int32, (c, c), 1)).astype(F32)
        dec_ref[...] = jnp.exp(jnp.where(rel >= 0, rel * lgf[:, :1], -rel * lgb[:, :1]))
        vec_ref[0] = jnp.exp((i_row + 1.0) * lgf)
        vec_ref[1] = jnp.exp((c - 1.0 - i_row) * lgf)
        vec_ref[2] = jnp.exp((c - i_row) * lgb)
        vec_ref[3] = jnp.exp(i_row * lgb)
        vec_ref[4] = jnp.exp(c * lgf) + jnp.zeros((c, half), F32)
        vec_ref[5] = jnp.exp(c * lgb) + jnp.zeros((c, half), F32)
        sf_ref[...] = jnp.zeros_like(sf_ref)
        sb_ref[...] = jnp.zeros_like(sb_ref)

    def lanes2(d):
        return jnp.concatenate([d, d], axis=-1)

    @pl.when(t < nb)
    def _():
        base = (nb - 1 - t) * rows
        for ci in range(nch - 1, -1, -1):
            r0 = ci * c
            g0 = pl.multiple_of(base + r0, c)
            cos = cos_ref[r0:r0 + c, :]
            sin = sin_ref[r0:r0 + c, :]

            def rot(x):
                x1, x2 = x[:, :half], x[:, half:]
                return jnp.concatenate([x1 * cos - x2 * sin, x1 * sin + x2 * cos], axis=-1)

            q = rot(q_ref[r0:r0 + c, :]) * (RET_HEAD_DIM ** -0.5)
            k = rot(k_ref[r0:r0 + c, :])
            qr_ref[pl.ds(g0, c), :] = q
            kr_ref[pl.ds(g0, c), :] = k
            vb = v_ref[r0:r0 + c, :].astype(BF16)
            sb = sb_ref[...]
            ob_ref[pl.ds(g0, c), :] = _dot((q * lanes2(vec_ref[2])).astype(BF16), sb.astype(BF16))
            kd = k * lanes2(vec_ref[3])
            sb_ref[...] = vec_ref[5][0:1, 0:1] * sb + _dot_tn(kd.astype(BF16), vb)

    @pl.when(t >= nb)
    def _():
        base = (t - nb) * rows
        for ci in range(nch):
            r0 = ci * c
            g0 = pl.multiple_of(base + r0, c)
            q = qr_ref[pl.ds(g0, c), :]
            k = kr_ref[pl.ds(g0, c), :]
            vb = v_ref[r0:r0 + c, :].astype(BF16)
            sf = sf_ref[...]
            scores = _dot_nt(q.astype(BF16), k.astype(BF16)) * dec_ref[...]
            o = (_dot(scores.astype(BF16), vb) + _dot((q * lanes2(vec_ref[0])).astype(BF16), sf.astype(BF16))
                 + ob_ref[pl.ds(g0, c), :])
            kd = k * lanes2(vec_ref[1])
            sf_ref[...] = vec_ref[4][0:1, 0:1] * sf + _dot_tn(kd.astype(BF16), vb)
            mu = jnp.mean(o, axis=-1, keepdims=True)
            oc = o - mu
            var = jnp.mean(oc * oc, axis=-1, keepdims=True)
            y = oc * lax.rsqrt(var + GN_EPS) * ng_ref[...]
            o_ref[r0:r0 + c, :] = (_silu(g_ref[r0:r0 + c, :]) * y).astype(o_ref.dtype)


def retention_mixer(proj, norm_g, batch, seq, col0):
    m = proj.shape[0]
    width = norm_g.shape[0]
    heads = width // RET_HEAD_DIM
    c = min(RET_CHUNK, seq)
    rows = min(RET_ROWS, seq)
    nb = seq // rows
    half = RET_HEAD_DIM // 2
    cb0 = col0 // RET_HEAD_DIM

    hidx = jnp.arange(heads, dtype=F32)
    gamma = 1.0 - jnp.power(2.0, -5.0 - hidx)
    lg = jnp.log(gamma)
    lg2 = jnp.broadcast_to(jnp.stack([lg, lg[::-1]], axis=1)[:, :, None], (heads, 2, LANES))
    inv = jnp.power(ROPE_BASE, -jnp.arange(half, dtype=F32) / half)
    ang = jnp.arange(seq, dtype=F32)[:, None] * inv[None, :]
    cos, sin = jnp.cos(ang), jnp.sin(ang)

    def bblk(t):
        return jnp.maximum(nb - 1 - t, 0)

    def fblk(t):
        return jnp.maximum(t - nb, 0)

    def blk(t):
        return jnp.where(t < nb, nb - 1 - t, t - nb)

    def spec(j, which):
        return pl.BlockSpec((rows, RET_HEAD_DIM), lambda b, h, t: (b * nb + which(t), cb0 + j * heads + h))

    return pl.pallas_call(
        functools.partial(_ret_kernel, nb=nb, rows=rows, c=c),
        grid=(batch, heads, 2 * nb),
        in_specs=[pl.BlockSpec((1, 2, LANES), lambda b, h, t: (h, 0, 0)),
                  pl.BlockSpec((rows, half), lambda b, h, t: (bblk(t), 0)),
                  pl.BlockSpec((rows, half), lambda b, h, t: (bblk(t), 0)),
                  spec(0, bblk), spec(1, bblk), spec(2, blk), spec(3, fblk),
                  pl.BlockSpec((1, RET_HEAD_DIM), lambda b, h, t: (0, h))],
        out_specs=pl.BlockSpec((rows, RET_HEAD_DIM), lambda b, h, t: (b * nb + fblk(t), h)),
        out_shape=jax.ShapeDtypeStruct((m, width), BF16),
        scratch_shapes=[pltpu.VMEM((RET_HEAD_DIM, RET_HEAD_DIM), F32),
                        pltpu.VMEM((RET_HEAD_DIM, RET_HEAD_DIM), F32),
                        pltpu.VMEM((seq, RET_HEAD_DIM), F32),
                        pltpu.VMEM((seq, RET_HEAD_DIM), F32),
                        pltpu.VMEM((seq, RET_HEAD_DIM), F32),
                        pltpu.VMEM((c, c), F32),
                        pltpu.VMEM((6, c, half), F32)],
        compiler_params=_cparams(("parallel", "parallel", "arbitrary")),
        name="retention",
    )(lg2, cos, sin, proj, proj, proj, proj, norm_g.reshape(1, width))


def _hgrn_kernel(lbl_ref, lvl_ref, up_ref, sg_ref, q_ref, z_ref, v_ref, g_ref, ng_ref, o_ref,
                 st_ref, ob_ref, *, nb, rows, sub, layer, depth):
    t = pl.program_id(2)
    d = HGRN_HEAD_DIM
    nlev = sub.bit_length() - 1

    def lower_bound(dirn):
        x = [lbl_ref[2 * l + dirn: 2 * l + dirn + 1, :] for l in range(depth)]
        mx = functools.reduce(jnp.maximum, x)
        e = [jnp.exp(xi - mx) for xi in x]
        tot = functools.reduce(lambda a, b: a + b, e)
        p = [ei / tot for ei in e]
        return functools.reduce(lambda a, b: a + b, p[:layer + 1]) - p[0]

    row = lax.broadcasted_iota(jnp.int32, (sub, d), 0)
    r_i = lax.broadcasted_iota(jnp.int32, (sub, sub), 0)
    c_i = lax.broadcasted_iota(jnp.int32, (sub, sub), 1)

    def split3(x):
        hi = x.astype(BF16)
        r1 = x - hi.astype(F32)
        mid = r1.astype(BF16)
        lo = (r1 - mid.astype(F32)).astype(BF16)
        return jnp.concatenate([hi, mid, lo], axis=1)

    def group_rows(x, size, pick):
        parts = [jnp.broadcast_to(x[base + pick:base + pick + 1, :], (size, d)) for base in range(0, sub, size)]
        return parts[0] if len(parts) == 1 else jnp.concatenate(parts, axis=0)

    def run(r0, rev, lb):
        f = lb + (1.0 - lb) * (1.0 / (1.0 + jnp.exp(-z_ref[r0:r0 + sub, :])))
        kk = 1.0 - f
        lf = jnp.log(f)
        q = q_ref[r0:r0 + sub, :] * (d ** -0.5)
        vb = v_ref[r0:r0 + sub, :].astype(BF16)

        tri = ((c_i >= r_i) if rev else (c_i <= r_i)).astype(BF16)
        parts = _dot(tri, split3(lf))
        cum = parts[:, :d] + parts[:, d:2 * d] + parts[:, 2 * d:]
        edge = cum[0:1, :] if rev else cum[sub - 1:sub, :]

        st = st_ref[...]
        o = _dot_nt((q * jnp.exp(cum)).astype(BF16), st.astype(BF16))
        kd = kk * jnp.exp(edge - cum)
        st_ref[...] = st * jnp.exp(edge) + _dot_tn(vb, kd.astype(BF16))

        lvl = lvl_ref[1 if rev else 0]
        a = jnp.where(lvl == nlev, jnp.sum(q * kk, axis=-1, keepdims=True), 0.0)
        lf_up = pltpu.roll(lf, sub - 1, 0)
        lf_dn = pltpu.roll(lf, 1, 0)
        qk = q - kk
        for lev in range(nlev):
            m = 1 << lev
            if m <= 2:
                upper = (row & m) != 0
                target = jnp.logical_not(upper) if rev else upper
                if m == 1:
                    dlt = jnp.where(upper, 0.0, lf) if rev else jnp.where(upper, lf, 0.0)
                else:
                    r4 = row & 3
                    if rev:
                        dlt = jnp.where(r4 == 0, lf + lf_up, jnp.where(r4 == 1, lf, jnp.where(r4 == 2, 0.0, lf_dn)))
                    else:
                        dlt = jnp.where(r4 == 0, lf_up, jnp.where(r4 == 1, 0.0, jnp.where(r4 == 2, lf, lf + lf_dn)))
                y = jnp.where(target, q, kk) * jnp.exp(dlt)
            else:
                up = up_ref[lev]
                sg = sg_ref[lev]
                bound = group_rows(cum, 2 * m, m if rev else m - 1)
                if rev:
                    y = (q - up * qk) * jnp.exp2((bound - cum) * sg)
                else:
                    y = (kk + up * qk) * jnp.exp2((cum - bound) * sg)
            yb = y.astype(BF16)
            a = jnp.where(lvl == lev, _dot_nt(yb, yb), a)
        return o + _dot(a.astype(BF16), vb)

    nsb = rows // sub

    @pl.when(t == 0)
    def _():
        st_ref[...] = jnp.zeros_like(st_ref)

    @pl.when(t < nb)
    def _():
        lb = lower_bound(1)
        base = (nb - 1 - t) * rows
        for si in range(nsb - 1, -1, -1):
            ob_ref[pl.ds(pl.multiple_of(base + si * sub, sub), sub), :] = run(si * sub, True, lb)

    @pl.when(t == nb)
    def _():
        st_ref[...] = jnp.zeros_like(st_ref)

    @pl.when(t >= nb)
    def _():
        lb = lower_bound(0)
        base = (t - nb) * rows
        for si in range(nsb):
            r0 = si * sub
            o = run(r0, False, lb) + ob_ref[pl.ds(pl.multiple_of(base + r0, sub), sub), :]
            o = o * lax.rsqrt(jnp.mean(o * o, axis=-1, keepdims=True) + RMS_EPS)
            o_ref[r0:r0 + sub, :] = (_silu(g_ref[r0:r0 + sub, :]) * (o * ng_ref[...])).astype(o_ref.dtype)


def hgrn2_mixer(proj, lb_logits, norm_g, layer, batch, seq, col0):
    m = proj.shape[0]
    width = norm_g.shape[0]
    d = HGRN_HEAD_DIM
    heads = width // d
    depth = lb_logits.shape[0]
    rows = min(HGRN_ROWS, seq)
    nb = seq // rows
    cb0 = col0 // d

    def blk(t):
        return jnp.where(t < nb, nb - 1 - t, t - nb)

    def fblk(t):
        return jnp.maximum(t - nb, 0)

    def spec(j):
        return pl.BlockSpec((rows, d), lambda b, h, t: (b * nb + blk(t), cb0 + j * heads + h))

    zspec = pl.BlockSpec((rows, d),
                         lambda b, h, t: (b * nb + blk(t), cb0 + jnp.where(t < nb, 2, 1) * heads + h))
    sub = min(HGRN_SUB, rows)
    assert sub & (sub - 1) == 0 and rows % sub == 0
    nlev = sub.bit_length() - 1
    idx = np.arange(sub)
    x = idx[:, None] ^ idx[None, :]
    lev = np.where(x > 0, np.floor(np.log2(np.maximum(x, 1))), nlev).astype(np.int32)
    lvl = jnp.asarray(np.stack([np.where(idx[:, None] >= idx[None, :], lev, -1),
                                np.where(idx[:, None] <= idx[None, :], lev, -1)]), jnp.int32)
    upper = np.stack([((idx >> l) & 1).astype(np.float32) for l in range(nlev)])
    up = jnp.broadcast_to(jnp.asarray(upper)[:, :, None], (nlev, sub, d))
    sg = jnp.broadcast_to(jnp.asarray((2.0 * upper - 1.0) * np.float32(math.log2(math.e)))[:, :, None],
                          (nlev, sub, d))
    tab = pl.BlockSpec((nlev, sub, d), lambda b, h, t: (0, 0, 0))
    return pl.pallas_call(
        functools.partial(_hgrn_kernel, nb=nb, rows=rows, sub=sub, layer=layer, depth=depth),
        grid=(batch, heads, 2 * nb),
        in_specs=[pl.BlockSpec((2 * depth, d), lambda b, h, t: (0, h)),
                  pl.BlockSpec((2, sub, sub), lambda b, h, t: (0, 0, 0)), tab, tab,
                  spec(0), zspec, spec(3),
                  pl.BlockSpec((rows, d), lambda b, h, t: (b * nb + fblk(t), cb0 + 4 * heads + h)),
                  pl.BlockSpec((1, d), lambda b, h, t: (0, h))],
        out_specs=pl.BlockSpec((rows, d), lambda b, h, t: (b * nb + fblk(t), h)),
        out_shape=jax.ShapeDtypeStruct((m, width), BF16),
        scratch_shapes=[pltpu.VMEM((d, d), F32),
                        pltpu.VMEM((seq, d), F32)],
        compiler_params=_cparams(("parallel", "parallel", "arbitrary")),
        name="hgrn2",
    )(lb_logits.reshape(2 * depth, width), lvl, up, sg, proj, proj, proj, proj, norm_g.reshape(1, width))


def _dft_cos_sin(n):
    idx = np.arange(n)
    ang = 2.0 * np.pi * ((idx[:, None] * idx[None, :]) % n) / n
    return np.cos(ang), np.sin(ang)


def _fft_chan_kernel(z_ref, m_ref, vr_ref, vi_ref, *, groups):
    gd = FOURIER_GROUP_DIM
    mat = m_ref[...]
    for g in range(groups):
        pq = _dot3_rhs(z_ref[:, g * gd:(g + 1) * gd], mat)
        vr_ref[:, g * gd:(g + 1) * gd] = pq[:, :gd]
        vi_ref[:, g * gd:(g + 1) * gd] = pq[:, gd:]


def _fft_stage1_kernel(vr_ref, vi_ref, m_ref, twc_ref, tws_ref, ur_ref, ui_ref, *, n1, reps, nsub):
    for bi in range(nsub):
        x = jnp.concatenate([vr_ref[:, bi, :], vi_ref[:, bi, :]], axis=0)
        tt = _dot3_lhs(m_ref[...], x)
        tr, ti = tt[:n1], tt[n1:]
        cw = jnp.tile(twc_ref[bi], (1, reps))
        sw = jnp.tile(tws_ref[bi], (1, reps))
        ur_ref[:, bi, :] = tr * cw + ti * sw
        ui_ref[:, bi, :] = ti * cw - tr * sw


def _fft_stage2_kernel(ur_ref, ui_ref, m_ref, w_ref, b_ref, o_ref, y_ref, *, groups, scale, n2, nsub):
    gd = FOURIER_GROUP_DIM
    for ki in range(nsub):
        x = jnp.concatenate([ur_ref[ki * n2:(ki + 1) * n2, :], ui_ref[ki * n2:(ki + 1) * n2, :]], axis=0)
        spec = _dot3_lhs(m_ref[...], x) * scale
        for g in range(groups):
            y = _dot(spec[:, g * gd:(g + 1) * gd].astype(BF16), w_ref[g].astype(BF16))
            y_ref[:, ki, g * gd:(g + 1) * gd] = y + b_ref[:, g * gd:(g + 1) * gd]
    o_ref[...] = y_ref[...].astype(o_ref.dtype)


def fourier_mixer(proj, w, bias, batch, seq, col0):
    m = proj.shape[0]
    groups, gd, _ = w.shape
    width = groups * gd
    n1 = min(FFT_N1, seq)
    n2 = seq // n1
    br = min(512, m)

    cc, sc = _dft_cos_sin(gd)
    chan = _const3_rhs(np.concatenate([cc, -sc], axis=1))
    vr, vi = pl.pallas_call(
        functools.partial(_fft_chan_kernel, groups=groups),
        grid=(m // br,),
        in_specs=[pl.BlockSpec((br, width), lambda i: (i, col0 // width)),
                  pl.BlockSpec((3 * gd, 2 * gd), lambda i: (0, 0))],
        out_specs=[pl.BlockSpec((br, width), lambda i: (i, 0))] * 2,
        out_shape=[jax.ShapeDtypeStruct((m, width), F32)] * 2,
        compiler_params=_cparams(("parallel",)),
        name="fft_channels",
    )(proj, chan)

    c1, s1 = _dft_cos_sin(n1)
    m1 = _const3_lhs(np.block([[c1, s1], [-s1, c1]]))
    k1 = np.arange(n1)
    bb = np.arange(n2)
    tw = 2.0 * np.pi * ((bb[:, None] * k1[None, :]) % seq) / seq
    twc = jnp.broadcast_to(jnp.asarray(np.cos(tw), F32)[:, :, None], (n2, n1, LANES))
    tws = jnp.broadcast_to(jnp.asarray(np.sin(tw), F32)[:, :, None], (n2, n1, LANES))
    s1 = min(FFT_SUB1, n2)
    vspec = pl.BlockSpec((n1, s1, width), lambda bt, j: (bt, j, 0))
    tspec = pl.BlockSpec((s1, n1, LANES), lambda bt, j: (j, 0, 0))
    ur, ui = pl.pallas_call(
        functools.partial(_fft_stage1_kernel, n1=n1, reps=width // LANES, nsub=s1),
        grid=(batch, n2 // s1),
        in_specs=[vspec, vspec, pl.BlockSpec((2 * n1, 6 * n1), lambda bt, j: (0, 0)), tspec, tspec],
        out_specs=[vspec, vspec],
        out_shape=[jax.ShapeDtypeStruct((batch * n1, n2, width), F32)] * 2,
        compiler_params=_cparams(("parallel", "parallel")),
        name="fft_stage1",
    )(vr.reshape(batch * n1, n2, width), vi.reshape(batch * n1, n2, width), m1, twc, tws)

    c2, s2 = _dft_cos_sin(n2)
    m2 = _const3_lhs(np.concatenate([c2, s2], axis=1))
    s2 = min(FFT_SUB2, n1)
    uspec = pl.BlockSpec((s2 * n2, width), lambda bt, j: (bt * (n1 // s2) + j, 0))
    out = pl.pallas_call(
        functools.partial(_fft_stage2_kernel, groups=groups, scale=1.0 / math.sqrt(seq * gd), n2=n2, nsub=s2),
        grid=(batch, n1 // s2),
        in_specs=[uspec, uspec, pl.BlockSpec((n2, 6 * n2), lambda bt, j: (0, 0)),
                  pl.BlockSpec((groups, gd, gd), lambda bt, j: (0, 0, 0)),
                  pl.BlockSpec((1, width), lambda bt, j: (0, 0))],
        out_specs=pl.BlockSpec((n2, s2, width), lambda bt, j: (bt, j, 0)),
        out_shape=jax.ShapeDtypeStruct((batch * n2, n1, width), BF16),
        scratch_shapes=[pltpu.VMEM((n2, s2, width), F32)],
        compiler_params=_cparams(("parallel", "parallel")),
        name="fft_stage2",
    )(ur.reshape(m, width), ui.reshape(m, width), m2, w, bias.reshape(1, width))
    return out.reshape(m, width)


def _xattn_block_kernel(h_ref, y_ref, g_ref, wq_ref, wo_ref, k_ref, v_ref, ho_ref, f_ref, *, heads):
    hd = XATTN_HEAD_DIM
    h1 = h_ref[...] + _rms(y_ref[...], g_ref[0:1, :])
    c = _rms(h1, g_ref[1:2, :]).astype(BF16)
    q = (_dot(c, wq_ref[...]) * (hd ** -0.5)).astype(BF16)
    outs = []
    for hh in range(heads):
        cols = slice(hh * hd, (hh + 1) * hd)
        s = _dot_nt(q[:, cols], k_ref[:, cols])
        p = jnp.exp(s - jnp.max(s, axis=-1, keepdims=True))
        denom = jnp.sum(p, axis=-1, keepdims=True)
        outs.append((_dot(p.astype(BF16), v_ref[:, cols]) / denom).astype(BF16))
    xa = _dot(jnp.concatenate(outs, axis=1), wo_ref[...])
    h2 = h1 + _rms(xa, g_ref[2:3, :])
    ho_ref[...] = h2
    f_ref[...] = _rms(h2, g_ref[3:4, :]).astype(f_ref.dtype)


def xattn_block(h, y, gains, wq, wo, kx, vx, seq, mem_tokens):
    m, d = h.shape
    width = wq.shape[1]
    bm = min(XATTN_ROWS, seq)
    nb = seq // bm
    row = pl.BlockSpec((bm, d), lambda i: (i, 0))

    def const(shape):
        return pl.BlockSpec(shape, lambda i: (0, 0), pipeline_mode=pl.Buffered(1))

    kv = pl.BlockSpec((mem_tokens, width), lambda i: (i // nb, 0))
    return pl.pallas_call(
        functools.partial(_xattn_block_kernel, heads=width // XATTN_HEAD_DIM),
        grid=(m // bm,),
        in_specs=[row, row, const((4, d)), const((d, width)), const((width, d)), kv, kv],
        out_specs=[row, row],
        out_shape=[jax.ShapeDtypeStruct((m, d), F32), jax.ShapeDtypeStruct((m, d), BF16)],
        compiler_params=_cparams(("arbitrary",)),
        name="xattn_block",
    )(h, y, gains, wq, wo, kx, vx)


def kernel(x, mem, mem_norm_g, pre_mix_g, w_in, ret_norm_g, hgrn_lb_logits, hgrn_norm_g, fourier_w, fourier_b,
           w_out, post_mix_g, pre_xattn_g, xattn_wq, xattn_wk, xattn_wv, xattn_wo, post_xattn_g, pre_ffn_g,
           ffn_w_gate, ffn_w_up, ffn_w_down, post_ffn_g):
    batch, seq, d = x.shape
    depth = w_in.shape[0]
    mem_tokens = mem.shape[1]
    m = batch * seq
    ret_w = ret_norm_g.shape[1]
    hgrn_w = hgrn_norm_g.shape[1]

    h = x.reshape(m, d)
    mem_n = rmsnorm_rows(mem.reshape(batch * mem_tokens, d), mem_norm_g)
    a = rmsnorm_rows(h, pre_mix_g[0])
    for l in range(depth):
        proj = matmul([a], w_in, l, F32, 2048, 512)
        y_ret = retention_mixer(proj, ret_norm_g[l], batch, seq, 0)
        y_hgrn = hgrn2_mixer(proj, hgrn_lb_logits, hgrn_norm_g[l], l, batch, seq, 4 * ret_w)
        y_fft = fourier_mixer(proj, fourier_w[l], fourier_b[l], batch, seq, 4 * ret_w + 5 * hgrn_w)
        mixed = matmul([y_ret, y_hgrn, y_fft], w_out, l, F32, 2048, 512)
        kx = matmul([mem_n], xattn_wk, l, BF16, 512, 512)
        vx = matmul([mem_n], xattn_wv, l, BF16, 512, 512)
        gains = jnp.stack([post_mix_g[l], pre_xattn_g[l], post_xattn_g[l], pre_ffn_g[l]])
        h, f = xattn_block(h, mixed, gains, xattn_wq[l].astype(BF16), xattn_wo[l].astype(BF16), kx, vx,
                           seq, mem_tokens)
        hid = swiglu_matmul(f, ffn_w_gate, ffn_w_up, l, 2048, 256)
        ff = matmul_ktiled(hid, ffn_w_down, l, 1024, 1024, 2048)
        h, a = add_norm(h, ff, post_ffn_g[l], pre_mix_g[l + 1] if l + 1 < depth else None)
    return h.reshape(batch, seq, d)
```

```python
import functools
import math

import numpy as np
import jax
import jax.numpy as jnp
from jax import lax
from jax.experimental import pallas as pl
from jax.experimental.pallas import tpu as pltpu

F32 = jnp.float32
BF16 = jnp.bfloat16

RET_HEAD_DIM = 256
HGRN_HEAD_DIM = 128
FOURIER_GROUP_DIM = 128
XATTN_HEADS = 4
XATTN_HEAD_DIM = 256
ROPE_BASE = 10000.0
RMS_EPS = 1e-6
GN_EPS = 1e-5

LANES = 128
MXU_TILE = 256
V7X_VMEM_BYTES = 64 * 1024 * 1024
VMEM_LIMIT = V7X_VMEM_BYTES - 8 * 1024 * 1024

RET_CHUNK = MXU_TILE
RET_ROWS = 2048
HGRN_ROWS = 1024
HGRN_SUB = 128
XATTN_ROWS = 256
FFT_N1 = 64
FFT_SUB1 = 8
FFT_SUB2 = 16


def _cparams(sem):
    return pltpu.CompilerParams(dimension_semantics=sem, vmem_limit_bytes=VMEM_LIMIT)


def _tile(n, preferred):
    for t in range(min(preferred, n) // LANES * LANES, 0, -LANES):
        if n % t == 0:
            return t
    return n


def _dot(a, b):
    return jnp.dot(a, b, preferred_element_type=F32)


def _dot_nt(a, b):
    return lax.dot_general(a, b, (((1,), (1,)), ((), ())), preferred_element_type=F32)


def _dot_tn(a, b):
    return lax.dot_general(a, b, (((0,), (0,)), ((), ())), preferred_element_type=F32)


def _split2(x):
    hi = x.astype(BF16)
    return hi, (x - hi.astype(F32)).astype(BF16)


def _const3_lhs(mat):
    hi, lo = _split2(jnp.asarray(mat, F32))
    return jnp.concatenate([hi, hi, lo], axis=1)


def _const3_rhs(mat):
    hi, lo = _split2(jnp.asarray(mat, F32))
    return jnp.concatenate([hi, hi, lo], axis=0)


def _dot3_lhs(m3, x):
    hi, lo = _split2(x)
    return _dot(m3, jnp.concatenate([hi, lo, hi], axis=0))


def _dot3_rhs(x, m3):
    hi, lo = _split2(x)
    return _dot(jnp.concatenate([hi, lo, hi], axis=1), m3)


def _rms(x, g):
    ms = jnp.mean(x * x, axis=-1, keepdims=True)
    return x * lax.rsqrt(ms + RMS_EPS) * g


def _silu(x):
    return x * (1.0 / (1.0 + jnp.exp(-x)))


def _norm_kernel(x_ref, g_ref, o_ref):
    o_ref[...] = _rms(x_ref[...], g_ref[...]).astype(o_ref.dtype)


def rmsnorm_rows(x, g, br=256):
    m, d = x.shape
    out_dtype = BF16
    br = min(br, m)
    return pl.pallas_call(
        _norm_kernel,
        grid=(m // br,),
        in_specs=[pl.BlockSpec((br, d), lambda i: (i, 0)),
                  pl.BlockSpec((1, d), lambda i: (0, 0))],
        out_specs=pl.BlockSpec((br, d), lambda i: (i, 0)),
        out_shape=jax.ShapeDtypeStruct((m, d), out_dtype),
        compiler_params=_cparams(("parallel",)),
        name="rmsnorm_rows",
    )(x, g.reshape(1, d))


def _add_norm_kernel(h_ref, y_ref, gp_ref, gn_ref, ho_ref, a_ref):
    h = h_ref[...] + _rms(y_ref[...], gp_ref[...])
    ho_ref[...] = h
    a_ref[...] = _rms(h, gn_ref[...]).astype(a_ref.dtype)


def _add_norm_last_kernel(h_ref, y_ref, gp_ref, ho_ref):
    ho_ref[...] = h_ref[...] + _rms(y_ref[...], gp_ref[...])


def add_norm(h, y, g_post, g_next, br=256):
    m, d = h.shape
    row = pl.BlockSpec((br, d), lambda i: (i, 0))
    vec = pl.BlockSpec((1, d), lambda i: (0, 0))
    if g_next is None:
        return pl.pallas_call(
            _add_norm_last_kernel, grid=(m // br,),
            in_specs=[row, row, vec], out_specs=row,
            out_shape=jax.ShapeDtypeStruct((m, d), F32),
            compiler_params=_cparams(("parallel",)), name="add_norm_last",
        )(h, y, g_post.reshape(1, d)), None
    return pl.pallas_call(
        _add_norm_kernel, grid=(m // br,),
        in_specs=[row, row, vec, vec], out_specs=[row, row],
        out_shape=[jax.ShapeDtypeStruct((m, d), F32), jax.ShapeDtypeStruct((m, d), BF16)],
        compiler_params=_cparams(("parallel",)), name="add_norm",
    )(h, y, g_post.reshape(1, d), g_next.reshape(1, d))


def _row_resident_spec(bm, kdim):
    return pl.BlockSpec((bm, kdim), lambda i, j: (i, 0), pipeline_mode=pl.Buffered(1))


def _mm_kernel(*refs, k_parts):
    a_refs, w_ref, o_ref = refs[:len(k_parts)], refs[len(k_parts)], refs[-1]
    acc, r0 = None, 0
    for a_ref, kp in zip(a_refs, k_parts):
        part = _dot(a_ref[...], w_ref[r0:r0 + kp, :].astype(BF16))
        acc = part if acc is None else acc + part
        r0 += kp
    o_ref[...] = acc.astype(o_ref.dtype)


def matmul(a_parts, w, layer, out_dtype, bm, bn):
    m = a_parts[0].shape[0]
    k_parts = tuple(a.shape[1] for a in a_parts)
    kdim, n = w.shape[1], w.shape[2]
    assert sum(k_parts) == kdim
    bm, bn = _tile(m, bm), _tile(n, bn)
    return pl.pallas_call(
        functools.partial(_mm_kernel, k_parts=k_parts),
        grid=(m // bm, n // bn),
        in_specs=[_row_resident_spec(bm, kp) for kp in k_parts]
        + [pl.BlockSpec((None, kdim, bn), lambda i, j: (layer, 0, j))],
        out_specs=pl.BlockSpec((bm, bn), lambda i, j: (i, j)),
        out_shape=jax.ShapeDtypeStruct((m, n), out_dtype),
        compiler_params=_cparams(("parallel", "arbitrary")),
        name="matmul",
    )(*a_parts, w)


def _swiglu_kernel(a_ref, wg_ref, wu_ref, o_ref):
    a = a_ref[...]
    gate = _dot(a, wg_ref[...].astype(BF16))
    up = _dot(a, wu_ref[...].astype(BF16))
    o_ref[...] = (_silu(gate) * up).astype(o_ref.dtype)


def swiglu_matmul(a, wg, wu, layer, bm, bn):
    m, kdim = a.shape
    n = wg.shape[2]
    bm, bn = _tile(m, bm), _tile(n, bn)
    wspec = pl.BlockSpec((None, kdim, bn), lambda i, j: (layer, 0, j))
    return pl.pallas_call(
        _swiglu_kernel,
        grid=(m // bm, n // bn),
        in_specs=[_row_resident_spec(bm, kdim), wspec, wspec],
        out_specs=pl.BlockSpec((bm, bn), lambda i, j: (i, j)),
        out_shape=jax.ShapeDtypeStruct((m, n), BF16),
        compiler_params=_cparams(("parallel", "arbitrary")),
        name="swiglu_matmul",
    )(a, wg, wu)


def _mm_ktiled_kernel(*refs, n_tail):
    a_ref, w_ref = refs[0], refs[1]
    t_refs, wt_ref, o_ref = refs[2:2 + n_tail], refs[2 + n_tail], refs[-1]
    k = pl.program_id(2)

    @pl.when(k == 0)
    def _():
        tail = jnp.concatenate([t[...] for t in t_refs], axis=1)
        o_ref[...] = _dot(a_ref[...], w_ref[...].astype(BF16)) + _dot(tail, wt_ref[...].astype(BF16))

    @pl.when(k > 0)
    def _():
        o_ref[...] += _dot(a_ref[...], w_ref[...].astype(BF16))


def matmul_ktiled(a, w, layer, bm, bn, bk):
    m, kdim = a.shape
    n = w.shape[2]
    bm, bn = _tile(m, bm), _tile(n, bn)
    nk = kdim // bk
    kmain = nk * bk
    ktail = kdim - kmain
    assert nk >= 1 and ktail > 0 and ktail % MXU_TILE == 0 and kmain % MXU_TILE == 0
    n_tail = ktail // MXU_TILE
    tb0 = kmain // MXU_TILE
    w_tail = w[layer, kmain:, :]
    tail_specs = [pl.BlockSpec((bm, MXU_TILE), functools.partial(lambda i, j, k, c: (i, c), c=tb0 + c))
                  for c in range(n_tail)]
    return pl.pallas_call(
        functools.partial(_mm_ktiled_kernel, n_tail=n_tail),
        grid=(m // bm, n // bn, nk),
        in_specs=[pl.BlockSpec((bm, bk), lambda i, j, k: (i, k)),
                  pl.BlockSpec((None, bk, bn), lambda i, j, k: (layer, k, j))]
        + tail_specs + [pl.BlockSpec((ktail, bn), lambda i, j, k: (0, j))],
        out_specs=pl.BlockSpec((bm, bn), lambda i, j, k: (i, j)),
        out_shape=jax.ShapeDtypeStruct((m, n), F32),
        compiler_params=_cparams(("parallel", "parallel", "arbitrary")),
        name="matmul_ktiled",
    )(a, w, *([a] * n_tail), w_tail)


def _ret_kernel(lg_ref, cos_ref, sin_ref, q_ref, k_ref, v_ref, g_ref, ng_ref, o_ref,
                sf_ref, sb_ref, ob_ref, qr_ref, kr_ref, dec_ref, vec_ref, *, nb, rows, c):
    t = pl.program_id(2)
    half = RET_HEAD_DIM // 2
    nch = rows // c

    @pl.when(t == 0)
    def _():
        lgf = lg_ref[0, 0:1, :]
        lgb = lg_ref[0, 1:2, :]
        i_row = lax.broadcasted_iota(jnp.int32, (c, half), 0).astype(F32)
        rel = (lax.broadcasted_iota(jnp.int32, (c, c), 0) - lax.broadcasted_iota(jnp.int32, (c, c), 1)).astype(F32)
        dec_ref[...] = jnp.exp(jnp.where(rel >= 0, rel * lgf[:, :1], -rel * lgb[:, :1]))
        vec_ref[0] = jnp.exp((i_row + 1.0) * lgf)
        vec_ref[1] = jnp.exp((c - 1.0 - i_row) * lgf)
        vec_ref[2] = jnp.exp((c - i_row) * lgb)
        vec_ref[3] = jnp.exp(i_row * lgb)
        vec_ref[4] = jnp.exp(c * lgf) + jnp.zeros((c, half), F32)
        vec_ref[5] = jnp.exp(c * lgb) + jnp.zeros((c, half), F32)
        sf_ref[...] = jnp.zeros_like(sf_ref)
        sb_ref[...] = jnp.zeros_like(sb_ref)

    def lanes2(d):
        return jnp.concatenate([d, d], axis=-1)

    @pl.when(t < nb)
    def _():
        base = (nb - 1 - t) * rows
        for ci in range(nch - 1, -1, -1):
            r0 = ci * c
            g0 = pl.multiple_of(base + r0, c)
            cos = cos_ref[r0:r0 + c, :]
            sin = sin_ref[r0:r0 + c, :]

            def rot(x):
                x1, x2 = x[:, :half], x[:, half:]
                return jnp.concatenate([x1 * cos - x2 * sin, x1 * sin + x2 * cos], axis=-1)

            q = rot(q_ref[r0:r0 + c, :]) * (RET_HEAD_DIM ** -0.5)
            k = rot(k_ref[r0:r0 + c, :])
            qr_ref[pl.ds(g0, c), :] = q
            kr_ref[pl.ds(g0, c), :] = k
            vb = v_ref[r0:r0 + c, :].astype(BF16)
            sb = sb_ref[...]
            ob_ref[pl.ds(g0, c), :] = _dot((q * lanes2(vec_ref[2])).astype(BF16), sb.astype(BF16))
            kd = k * lanes2(vec_ref[3])
            sb_ref[...] = vec_ref[5][0:1, 0:1] * sb + _dot_tn(kd.astype(BF16), vb)

    @pl.when(t >= nb)
    def _():
        base = (t - nb) * rows
        for ci in range(nch):
            r0 = ci * c
            g0 = pl.multiple_of(base + r0, c)
            q = qr_ref[pl.ds(g0, c), :]
            k = kr_ref[pl.ds(g0, c), :]
            vb = v_ref[r0:r0 + c, :].astype(BF16)
            sf = sf_ref[...]
            scores = _dot_nt(q.astype(BF16), k.astype(BF16)) * dec_ref[...]
            o = (_dot(scores.astype(BF16), vb) + _dot((q * lanes2(vec_ref[0])).astype(BF16), sf.astype(BF16))
                 + ob_ref[pl.ds(g0, c), :])
            kd = k * lanes2(vec_ref[1])
            sf_ref[...] = vec_ref[4][0:1, 0:1] * sf + _dot_tn(kd.astype(BF16), vb)
            mu = jnp.mean(o, axis=-1, keepdims=True)
            oc = o - mu
            var = jnp.mean(oc * oc, axis=-1, keepdims=True)
            y = oc * lax.rsqrt(var + GN_EPS) * ng_ref[...]
            o_ref[r0:r0 + c, :] = (_silu(g_ref[r0:r0 + c, :]) * y).astype(o_ref.dtype)


def retention_mixer(proj, norm_g, batch, seq, col0):
    m = proj.shape[0]
    width = norm_g.shape[0]
    heads = width // RET_HEAD_DIM
    c = min(RET_CHUNK, seq)
    rows = min(RET_ROWS, seq)
    nb = seq // rows
    half = RET_HEAD_DIM // 2
    cb0 = col0 // RET_HEAD_DIM

    hidx = jnp.arange(heads, dtype=F32)
    gamma = 1.0 - jnp.power(2.0, -5.0 - hidx)
    lg = jnp.log(gamma)
    lg2 = jnp.broadcast_to(jnp.stack([lg, lg[::-1]], axis=1)[:, :, None], (heads, 2, LANES))
    inv = jnp.power(ROPE_BASE, -jnp.arange(half, dtype=F32) / half)
    ang = jnp.arange(seq, dtype=F32)[:, None] * inv[None, :]
    cos, sin = jnp.cos(ang), jnp.sin(ang)

    def bblk(t):
        return jnp.maximum(nb - 1 - t, 0)

    def fblk(t):
        return jnp.maximum(t - nb, 0)

    def blk(t):
        return jnp.where(t < nb, nb - 1 - t, t - nb)

    def spec(j, which):
        return pl.BlockSpec((rows, RET_HEAD_DIM), lambda b, h, t: (b * nb + which(t), cb0 + j * heads + h))

    return pl.pallas_call(
        functools.partial(_ret_kernel, nb=nb, rows=rows, c=c),
        grid=(batch, heads, 2 * nb),
        in_specs=[pl.BlockSpec((1, 2, LANES), lambda b, h, t: (h, 0, 0)),
                  pl.BlockSpec((rows, half), lambda b, h, t: (bblk(t), 0)),
                  pl.BlockSpec((rows, half), lambda b, h, t: (bblk(t), 0)),
                  spec(0, bblk), spec(1, bblk), spec(2, blk), spec(3, fblk),
                  pl.BlockSpec((1, RET_HEAD_DIM), lambda b, h, t: (0, h))],
        out_specs=pl.BlockSpec((rows, RET_HEAD_DIM), lambda b, h, t: (b * nb + fblk(t), h)),
        out_shape=jax.ShapeDtypeStruct((m, width), BF16),
        scratch_shapes=[pltpu.VMEM((RET_HEAD_DIM, RET_HEAD_DIM), F32),
                        pltpu.VMEM((RET_HEAD_DIM, RET_HEAD_DIM), F32),
                        pltpu.VMEM((seq, RET_HEAD_DIM), F32),
                        pltpu.VMEM((seq, RET_HEAD_DIM), F32),
                        pltpu.VMEM((seq, RET_HEAD_DIM), F32),
                        pltpu.VMEM((c, c), F32),
                        pltpu.VMEM((6, c, half), F32)],
        compiler_params=_cparams(("parallel", "parallel", "arbitrary")),
        name="retention",
    )(lg2, cos, sin, proj, proj, proj, proj, norm_g.reshape(1, width))


def _hgrn_kernel(lbl_ref, lvl_ref, up_ref, sg_ref, cf_ref, q_ref, z_ref, v_ref, g_ref, ng_ref, o_ref,
                 st_ref, ob_ref, *, nb, rows, sub, layer, depth):
    t = pl.program_id(2)
    d = HGRN_HEAD_DIM
    nlev = sub.bit_length() - 1

    def lower_bound(dirn):
        x = [lbl_ref[2 * l + dirn: 2 * l + dirn + 1, :] for l in range(depth)]
        mx = functools.reduce(jnp.maximum, x)
        e = [jnp.exp(xi - mx) for xi in x]
        tot = functools.reduce(lambda a, b: a + b, e)
        p = [ei / tot for ei in e]
        return functools.reduce(lambda a, b: a + b, p[:layer + 1]) - p[0]

    r_i = lax.broadcasted_iota(jnp.int32, (sub, sub), 0)
    c_i = lax.broadcasted_iota(jnp.int32, (sub, sub), 1)

    def split3(x):
        hi = x.astype(BF16)
        r1 = x - hi.astype(F32)
        mid = r1.astype(BF16)
        lo = (r1 - mid.astype(F32)).astype(BF16)
        return jnp.concatenate([hi, mid, lo], axis=1)

    def group_rows(x, size, pick):
        parts = [jnp.broadcast_to(x[base + pick:base + pick + 1, :], (size, d)) for base in range(0, sub, size)]
        return parts[0] if len(parts) == 1 else jnp.concatenate(parts, axis=0)

    def run(r0, rev, lb):
        f = lb + (1.0 - lb) * (1.0 / (1.0 + jnp.exp(-z_ref[r0:r0 + sub, :])))
        kk = 1.0 - f
        lf = jnp.log(f)
        q = q_ref[r0:r0 + sub, :] * (d ** -0.5)
        vb = v_ref[r0:r0 + sub, :].astype(BF16)

        tri = ((c_i >= r_i) if rev else (c_i <= r_i)).astype(BF16)
        parts = _dot(tri, split3(lf))
        cum = parts[:, :d] + parts[:, d:2 * d] + parts[:, 2 * d:]
        edge = cum[0:1, :] if rev else cum[sub - 1:sub, :]

        st = st_ref[...]
        o = _dot_nt((q * jnp.exp(cum)).astype(BF16), st.astype(BF16))
        kd = kk * jnp.exp(edge - cum)
        st_ref[...] = st * jnp.exp(edge) + _dot_tn(vb, kd.astype(BF16))

        lvl = lvl_ref[1 if rev else 0]
        a = jnp.where(lvl == nlev, jnp.sum(q * kk, axis=-1, keepdims=True), 0.0)
        lf_up = pltpu.roll(lf, sub - 1, 0)
        lf_dn = pltpu.roll(lf, 1, 0)
        qk = q - kk
        for lev in range(nlev):
            m = 1 << lev
            di = 1 if rev else 0
            if m == 1:
                arg = lf * cf_ref[di, 0]
            elif m == 2:
                arg = lf_up * cf_ref[di, 1] + lf * cf_ref[di, 2] + lf_dn * cf_ref[di, 3]
            else:
                bound = group_rows(cum, 2 * m, m if rev else m - 1)
                arg = ((bound - cum) if rev else (cum - bound)) * sg_ref[lev]
            up = up_ref[lev]
            y = ((q - up * qk) if rev else (kk + up * qk)) * jnp.exp2(arg)
            yb = y.astype(BF16)
            a = jnp.where(lvl == lev, _dot_nt(yb, yb), a)
        return o + _dot(a.astype(BF16), vb)

    nsb = rows // sub

    @pl.when(t == 0)
    def _():
        st_ref[...] = jnp.zeros_like(st_ref)

    @pl.when(t < nb)
    def _():
        lb = lower_bound(1)
        base = (nb - 1 - t) * rows
        for si in range(nsb - 1, -1, -1):
            ob_ref[pl.ds(pl.multiple_of(base + si * sub, sub), sub), :] = run(si * sub, True, lb)

    @pl.when(t == nb)
    def _():
        st_ref[...] = jnp.zeros_like(st_ref)

    @pl.when(t >= nb)
    def _():
        lb = lower_bound(0)
        base = (t - nb) * rows
        for si in range(nsb):
            r0 = si * sub
            o = run(r0, False, lb) + ob_ref[pl.ds(pl.multiple_of(base + r0, sub), sub), :]
            o = o * lax.rsqrt(jnp.mean(o * o, axis=-1, keepdims=True) + RMS_EPS)
            o_ref[r0:r0 + sub, :] = (_silu(g_ref[r0:r0 + sub, :]) * (o * ng_ref[...])).astype(o_ref.dtype)


def hgrn2_mixer(proj, lb_logits, norm_g, layer, batch, seq, col0):
    m = proj.shape[0]
    width = norm_g.shape[0]
    d = HGRN_HEAD_DIM
    heads = width // d
    depth = lb_logits.shape[0]
    rows = min(HGRN_ROWS, seq)
    nb = seq // rows
    cb0 = col0 // d

    def blk(t):
        return jnp.where(t < nb, nb - 1 - t, t - nb)

    def fblk(t):
        return jnp.maximum(t - nb, 0)

    def spec(j):
        return pl.BlockSpec((rows, d), lambda b, h, t: (b * nb + blk(t), cb0 + j * heads + h))

    zspec = pl.BlockSpec((rows, d),
                         lambda b, h, t: (b * nb + blk(t), cb0 + jnp.where(t < nb, 2, 1) * heads + h))
    sub = min(HGRN_SUB, rows)
    assert sub & (sub - 1) == 0 and rows % sub == 0
    nlev = sub.bit_length() - 1
    idx = np.arange(sub)
    x = idx[:, None] ^ idx[None, :]
    lev = np.where(x > 0, np.floor(np.log2(np.maximum(x, 1))), nlev).astype(np.int32)
    lvl = jnp.asarray(np.stack([np.where(idx[:, None] >= idx[None, :], lev, -1),
                                np.where(idx[:, None] <= idx[None, :], lev, -1)]), jnp.int32)
    upper = np.stack([((idx >> l) & 1).astype(np.float32) for l in range(nlev)])
    up = jnp.broadcast_to(jnp.asarray(upper)[:, :, None], (nlev, sub, d))
    sg = jnp.broadcast_to(jnp.asarray((2.0 * upper - 1.0) * np.float32(math.log2(math.e)))[:, :, None],
                          (nlev, sub, d))
    tab = pl.BlockSpec((nlev, sub, d), lambda b, h, t: (0, 0, 0))
    r4 = idx & 3
    pat = np.array([[[0, 1, 0, 1], [1, 0, 0, 0], [0, 0, 1, 1], [0, 0, 0, 1]],
                    [[1, 0, 1, 0], [1, 0, 0, 0], [1, 1, 0, 0], [0, 0, 0, 1]]], np.float32)
    cf = jnp.broadcast_to(jnp.asarray(pat[:, :, r4] * np.float32(math.log2(math.e)))[:, :, :, None],
                          (2, 4, sub, d))
    return pl.pallas_call(
        functools.partial(_hgrn_kernel, nb=nb, rows=rows, sub=sub, layer=layer, depth=depth),
        grid=(batch, heads, 2 * nb),
        in_specs=[pl.BlockSpec((2 * depth, d), lambda b, h, t: (0, h)),
                  pl.BlockSpec((2, sub, sub), lambda b, h, t: (0, 0, 0)), tab, tab,
                  pl.BlockSpec((2, 4, sub, d), lambda b, h, t: (0, 0, 0, 0)),
                  spec(0), zspec, spec(3),
                  pl.BlockSpec((rows, d), lambda b, h, t: (b * nb + fblk(t), cb0 + 4 * heads + h)),
                  pl.BlockSpec((1, d), lambda b, h, t: (0, h))],
        out_specs=pl.BlockSpec((rows, d), lambda b, h, t: (b * nb + fblk(t), h)),
        out_shape=jax.ShapeDtypeStruct((m, width), BF16),
        scratch_shapes=[pltpu.VMEM((d, d), F32),
                        pltpu.VMEM((seq, d), F32)],
        compiler_params=_cparams(("parallel", "parallel", "arbitrary")),
        name="hgrn2",
    )(lb_logits.reshape(2 * depth, width), lvl, up, sg, cf, proj, proj, proj, proj, norm_g.reshape(1, width))


def _dft_cos_sin(n):
    idx = np.arange(n)
    ang = 2.0 * np.pi * ((idx[:, None] * idx[None, :]) % n) / n
    return np.cos(ang), np.sin(ang)


def _fft_chan_kernel(z_ref, m_ref, vr_ref, vi_ref, *, groups):
    gd = FOURIER_GROUP_DIM
    mat = m_ref[...]
    for g in range(groups):
        pq = _dot3_rhs(z_ref[:, g * gd:(g + 1) * gd], mat)
        vr_ref[:, g * gd:(g + 1) * gd] = pq[:, :gd]
        vi_ref[:, g * gd:(g + 1) * gd] = pq[:, gd:]


def _fft_stage1_kernel(vr_ref, vi_ref, m_ref, twc_ref, tws_ref, ur_ref, ui_ref, *, n1, reps, nsub):
    for bi in range(nsub):
        x = jnp.concatenate([vr_ref[:, bi, :], vi_ref[:, bi, :]], axis=0)
        tt = _dot3_lhs(m_ref[...], x)
        tr, ti = tt[:n1], tt[n1:]
        cw = jnp.tile(twc_ref[bi], (1, reps))
        sw = jnp.tile(tws_ref[bi], (1, reps))
        ur_ref[:, bi, :] = tr * cw + ti * sw
        ui_ref[:, bi, :] = ti * cw - tr * sw


def _fft_stage2_kernel(ur_ref, ui_ref, m_ref, w_ref, b_ref, o_ref, y_ref, *, groups, scale, n2, nsub):
    gd = FOURIER_GROUP_DIM
    for ki in range(nsub):
        x = jnp.concatenate([ur_ref[ki * n2:(ki + 1) * n2, :], ui_ref[ki * n2:(ki + 1) * n2, :]], axis=0)
        spec = _dot3_lhs(m_ref[...], x) * scale
        for g in range(groups):
            y = _dot(spec[:, g * gd:(g + 1) * gd].astype(BF16), w_ref[g].astype(BF16))
            y_ref[:, ki, g * gd:(g + 1) * gd] = y + b_ref[:, g * gd:(g + 1) * gd]
    o_ref[...] = y_ref[...].astype(o_ref.dtype)


def fourier_mixer(proj, w, bias, batch, seq, col0):
    m = proj.shape[0]
    groups, gd, _ = w.shape
    width = groups * gd
    n1 = min(FFT_N1, seq)
    n2 = seq // n1
    br = min(512, m)

    cc, sc = _dft_cos_sin(gd)
    chan = _const3_rhs(np.concatenate([cc, -sc], axis=1))
    vr, vi = pl.pallas_call(
        functools.partial(_fft_chan_kernel, groups=groups),
        grid=(m // br,),
        in_specs=[pl.BlockSpec((br, width), lambda i: (i, col0 // width)),
                  pl.BlockSpec((3 * gd, 2 * gd), lambda i: (0, 0))],
        out_specs=[pl.BlockSpec((br, width), lambda i: (i, 0))] * 2,
        out_shape=[jax.ShapeDtypeStruct((m, width), F32)] * 2,
        compiler_params=_cparams(("parallel",)),
        name="fft_channels",
    )(proj, chan)

    c1, s1 = _dft_cos_sin(n1)
    m1 = _const3_lhs(np.block([[c1, s1], [-s1, c1]]))
    k1 = np.arange(n1)
    bb = np.arange(n2)
    tw = 2.0 * np.pi * ((bb[:, None] * k1[None, :]) % seq) / seq
    twc = jnp.broadcast_to(jnp.asarray(np.cos(tw), F32)[:, :, None], (n2, n1, LANES))
    tws = jnp.broadcast_to(jnp.asarray(np.sin(tw), F32)[:, :, None], (n2, n1, LANES))
    s1 = min(FFT_SUB1, n2)
    vspec = pl.BlockSpec((n1, s1, width), lambda bt, j: (bt, j, 0))
    tspec = pl.BlockSpec((s1, n1, LANES), lambda bt, j: (j, 0, 0))
    ur, ui = pl.pallas_call(
        functools.partial(_fft_stage1_kernel, n1=n1, reps=width // LANES, nsub=s1),
        grid=(batch, n2 // s1),
        in_specs=[vspec, vspec, pl.BlockSpec((2 * n1, 6 * n1), lambda bt, j: (0, 0)), tspec, tspec],
        out_specs=[vspec, vspec],
        out_shape=[jax.ShapeDtypeStruct((batch * n1, n2, width), F32)] * 2,
        compiler_params=_cparams(("parallel", "parallel")),
        name="fft_stage1",
    )(vr.reshape(batch * n1, n2, width), vi.reshape(batch * n1, n2, width), m1, twc, tws)

    c2, s2 = _dft_cos_sin(n2)
    m2 = _const3_lhs(np.concatenate([c2, s2], axis=1))
    s2 = min(FFT_SUB2, n1)
    uspec = pl.BlockSpec((s2 * n2, width), lambda bt, j: (bt * (n1 // s2) + j, 0))
    out = pl.pallas_call(
        functools.partial(_fft_stage2_kernel, groups=groups, scale=1.0 / math.sqrt(seq * gd), n2=n2, nsub=s2),
        grid=(batch, n1 // s2),
        in_specs=[uspec, uspec, pl.BlockSpec((n2, 6 * n2), lambda bt, j: (0, 0)),
                  pl.BlockSpec((groups, gd, gd), lambda bt, j: (0, 0, 0)),
                  pl.BlockSpec((1, width), lambda bt, j: (0, 0))],
        out_specs=pl.BlockSpec((n2, s2, width), lambda bt, j: (bt, j, 0)),
        out_shape=jax.ShapeDtypeStruct((batch * n2, n1, width), BF16),
        scratch_shapes=[pltpu.VMEM((n2, s2, width), F32)],
        compiler_params=_cparams(("parallel", "parallel")),
        name="fft_stage2",
    )(ur.reshape(m, width), ui.reshape(m, width), m2, w, bias.reshape(1, width))
    return out.reshape(m, width)


def _xattn_block_kernel(h_ref, y_ref, g_ref, wq_ref, wo_ref, k_ref, v_ref, ho_ref, f_ref, *, heads):
    hd = XATTN_HEAD_DIM
    h1 = h_ref[...] + _rms(y_ref[...], g_ref[0:1, :])
    c = _rms(h1, g_ref[1:2, :]).astype(BF16)
    q = (_dot(c, wq_ref[...]) * (hd ** -0.5)).astype(BF16)
    outs = []
    for hh in range(heads):
        cols = slice(hh * hd, (hh + 1) * hd)
        s = _dot_nt(q[:, cols], k_ref[:, cols])
        p = jnp.exp(s - jnp.max(s, axis=-1, keepdims=True))
        denom = jnp.sum(p, axis=-1, keepdims=True)
        outs.append((_dot(p.astype(BF16), v_ref[:, cols]) / denom).astype(BF16))
    xa = _dot(jnp.concatenate(outs, axis=1), wo_ref[...])
    h2 = h1 + _rms(xa, g_ref[2:3, :])
    ho_ref[...] = h2
    f_ref[...] = _rms(h2, g_ref[3:4, :]).astype(f_ref.dtype)


def xattn_block(h, y, gains, wq, wo, kx, vx, seq, mem_tokens):
    m, d = h.shape
    width = wq.shape[1]
    bm = min(XATTN_ROWS, seq)
    nb = seq // bm
    row = pl.BlockSpec((bm, d), lambda i: (i, 0))

    def const(shape):
        return pl.BlockSpec(shape, lambda i: (0, 0), pipeline_mode=pl.Buffered(1))

    kv = pl.BlockSpec((mem_tokens, width), lambda i: (i // nb, 0))
    return pl.pallas_call(
        functools.partial(_xattn_block_kernel, heads=width // XATTN_HEAD_DIM),
        grid=(m // bm,),
        in_specs=[row, row, const((4, d)), const((d, width)), const((width, d)), kv, kv],
        out_specs=[row, row],
        out_shape=[jax.ShapeDtypeStruct((m, d), F32), jax.ShapeDtypeStruct((m, d), BF16)],
        compiler_params=_cparams(("arbitrary",)),
        name="xattn_block",
    )(h, y, gains, wq, wo, kx, vx)


def kernel(x, mem, mem_norm_g, pre_mix_g, w_in, ret_norm_g, hgrn_lb_logits, hgrn_norm_g, fourier_w, fourier_b,
           w_out, post_mix_g, pre_xattn_g, xattn_wq, xattn_wk, xattn_wv, xattn_wo, post_xattn_g, pre_ffn_g,
           ffn_w_gate, ffn_w_up, ffn_w_down, post_ffn_g):
    batch, seq, d = x.shape
    depth = w_in.shape[0]
    mem_tokens = mem.shape[1]
    m = batch * seq
    ret_w = ret_norm_g.shape[1]
    hgrn_w = hgrn_norm_g.shape[1]

    h = x.reshape(m, d)
    mem_n = rmsnorm_rows(mem.reshape(batch * mem_tokens, d), mem_norm_g)
    a = rmsnorm_rows(h, pre_mix_g[0])
    for l in range(depth):
        proj = matmul([a], w_in, l, F32, 2048, 512)
        y_ret = retention_mixer(proj, ret_norm_g[l], batch, seq, 0)
        y_hgrn = hgrn2_mixer(proj, hgrn_lb_logits, hgrn_norm_g[l], l, batch, seq, 4 * ret_w)
        y_fft = fourier_mixer(proj, fourier_w[l], fourier_b[l], batch, seq, 4 * ret_w + 5 * hgrn_w)
        mixed = matmul([y_ret, y_hgrn, y_fft], w_out, l, F32, 2048, 512)
        kx = matmul([mem_n], xattn_wk, l, BF16, 512, 512)
        vx = matmul([mem_n], xattn_wv, l, BF16, 512, 512)
        gains = jnp.stack([post_mix_g[l], pre_xattn_g[l], post_xattn_g[l], pre_ffn_g[l]])
        h, f = xattn_block(h, mixed, gains, xattn_wq[l].astype(BF16), xattn_wo[l].astype(BF16), kx, vx,
                           seq, mem_tokens)
        hid = swiglu_matmul(f, ffn_w_gate, ffn_w_up, l, 2048, 256)
        ff = matmul_ktiled(hid, ffn_w_down, l, 1024, 1024, 2048)
        h, a = add_norm(h, ff, post_ffn_g[l], pre_mix_g[l + 1] if l + 1 < depth else None)
    return h.reshape(batch, seq, d)
```

```python
import functools
import math

import numpy as np
import jax
import jax.numpy as jnp
from jax import lax
from jax.experimental import pallas as pl
from jax.experimental.pallas import tpu as pltpu

F32 = jnp.float32
BF16 = jnp.bfloat16

RET_HEAD_DIM = 256
HGRN_HEAD_DIM = 128
FOURIER_GROUP_DIM = 128
XATTN_HEADS = 4
XATTN_HEAD_DIM = 256
ROPE_BASE = 10000.0
RMS_EPS = 1e-6
GN_EPS = 1e-5

LANES = 128
MXU_TILE = 256
V7X_VMEM_BYTES = 64 * 1024 * 1024
VMEM_LIMIT = V7X_VMEM_BYTES - 8 * 1024 * 1024

RET_CHUNK = MXU_TILE
RET_ROWS = 2048
HGRN_ROWS = 1024
SIDE_BN = 512
HGRN_SUB = 128
XATTN_ROWS = 256
FFT_N1 = 64
FFT_SUB1 = 8
FFT_SUB2 = 16


def _cparams(sem):
    return pltpu.CompilerParams(dimension_semantics=sem, vmem_limit_bytes=VMEM_LIMIT)


def _tile(n, preferred):
    for t in range(min(preferred, n) // LANES * LANES, 0, -LANES):
        if n % t == 0:
            return t
    return n


def _dot(a, b):
    return jnp.dot(a, b, preferred_element_type=F32)


def _dot_nt(a, b):
    return lax.dot_general(a, b, (((1,), (1,)), ((), ())), preferred_element_type=F32)


def _dot_tn(a, b):
    return lax.dot_general(a, b, (((0,), (0,)), ((), ())), preferred_element_type=F32)


def _split2(x):
    hi = x.astype(BF16)
    return hi, (x - hi.astype(F32)).astype(BF16)


def _const3_lhs(mat):
    hi, lo = _split2(jnp.asarray(mat, F32))
    return jnp.concatenate([hi, hi, lo], axis=1)


def _const3_rhs(mat):
    hi, lo = _split2(jnp.asarray(mat, F32))
    return jnp.concatenate([hi, hi, lo], axis=0)


def _dot3_lhs(m3, x):
    hi, lo = _split2(x)
    return _dot(m3, jnp.concatenate([hi, lo, hi], axis=0))


def _dot3_rhs(x, m3):
    hi, lo = _split2(x)
    return _dot(jnp.concatenate([hi, lo, hi], axis=1), m3)


def _rms(x, g):
    ms = jnp.mean(x * x, axis=-1, keepdims=True)
    return x * lax.rsqrt(ms + RMS_EPS) * g


def _silu(x):
    return x * (1.0 / (1.0 + jnp.exp(-x)))


def _norm_kernel(x_ref, g_ref, o_ref):
    o_ref[...] = _rms(x_ref[...], g_ref[...]).astype(o_ref.dtype)


def rmsnorm_rows(x, g, br=256):
    m, d = x.shape
    out_dtype = BF16
    br = min(br, m)
    return pl.pallas_call(
        _norm_kernel,
        grid=(m // br,),
        in_specs=[pl.BlockSpec((br, d), lambda i: (i, 0)),
                  pl.BlockSpec((1, d), lambda i: (0, 0))],
        out_specs=pl.BlockSpec((br, d), lambda i: (i, 0)),
        out_shape=jax.ShapeDtypeStruct((m, d), out_dtype),
        compiler_params=_cparams(("parallel",)),
        name="rmsnorm_rows",
    )(x, g.reshape(1, d))


def _add_norm_kernel(h_ref, y_ref, gp_ref, gn_ref, ho_ref, a_ref):
    h = h_ref[...] + _rms(y_ref[...], gp_ref[...])
    ho_ref[...] = h
    a_ref[...] = _rms(h, gn_ref[...]).astype(a_ref.dtype)


def _add_norm_last_kernel(h_ref, y_ref, gp_ref, ho_ref):
    ho_ref[...] = h_ref[...] + _rms(y_ref[...], gp_ref[...])


def add_norm(h, y, g_post, g_next, br=256):
    m, d = h.shape
    row = pl.BlockSpec((br, d), lambda i: (i, 0))
    vec = pl.BlockSpec((1, d), lambda i: (0, 0))
    if g_next is None:
        return pl.pallas_call(
            _add_norm_last_kernel, grid=(m // br,),
            in_specs=[row, row, vec], out_specs=row,
            out_shape=jax.ShapeDtypeStruct((m, d), F32),
            compiler_params=_cparams(("parallel",)), name="add_norm_last",
        )(h, y, g_post.reshape(1, d)), None
    return pl.pallas_call(
        _add_norm_kernel, grid=(m // br,),
        in_specs=[row, row, vec, vec], out_specs=[row, row],
        out_shape=[jax.ShapeDtypeStruct((m, d), F32), jax.ShapeDtypeStruct((m, d), BF16)],
        compiler_params=_cparams(("parallel",)), name="add_norm",
    )(h, y, g_post.reshape(1, d), g_next.reshape(1, d))


def _row_resident_spec(bm, kdim):
    return pl.BlockSpec((bm, kdim), lambda i, j: (i, 0), pipeline_mode=pl.Buffered(1))


def _mm_kernel(*refs, k_parts):
    a_refs, w_ref, o_ref = refs[:len(k_parts)], refs[len(k_parts)], refs[-1]
    acc, r0 = None, 0
    for a_ref, kp in zip(a_refs, k_parts):
        part = _dot(a_ref[...], w_ref[r0:r0 + kp, :].astype(BF16))
        acc = part if acc is None else acc + part
        r0 += kp
    o_ref[...] = acc.astype(o_ref.dtype)


def matmul(a_parts, w, layer, out_dtype, bm, bn, col0=0, n_cols=None):
    m = a_parts[0].shape[0]
    k_parts = tuple(a.shape[1] for a in a_parts)
    kdim = w.shape[1]
    n = w.shape[2] - col0 if n_cols is None else n_cols
    assert sum(k_parts) == kdim
    bm, bn = _tile(m, bm), _tile(math.gcd(n, col0) if col0 else n, bn)
    cb0 = col0 // bn
    return pl.pallas_call(
        functools.partial(_mm_kernel, k_parts=k_parts),
        grid=(m // bm, n // bn),
        in_specs=[_row_resident_spec(bm, kp) for kp in k_parts]
        + [pl.BlockSpec((None, kdim, bn), lambda i, j: (layer, 0, cb0 + j))],
        out_specs=pl.BlockSpec((bm, bn), lambda i, j: (i, j)),
        out_shape=jax.ShapeDtypeStruct((m, n), out_dtype),
        compiler_params=_cparams(("parallel", "arbitrary")),
        name="matmul",
    )(*a_parts, w)


def _swiglu_kernel(a_ref, wg_ref, wu_ref, o_ref):
    a = a_ref[...]
    gate = _dot(a, wg_ref[...].astype(BF16))
    up = _dot(a, wu_ref[...].astype(BF16))
    o_ref[...] = (_silu(gate) * up).astype(o_ref.dtype)


def swiglu_matmul(a, wg, wu, layer, bm, bn):
    m, kdim = a.shape
    n = wg.shape[2]
    bm, bn = _tile(m, bm), _tile(n, bn)
    wspec = pl.BlockSpec((None, kdim, bn), lambda i, j: (layer, 0, j))
    return pl.pallas_call(
        _swiglu_kernel,
        grid=(m // bm, n // bn),
        in_specs=[_row_resident_spec(bm, kdim), wspec, wspec],
        out_specs=pl.BlockSpec((bm, bn), lambda i, j: (i, j)),
        out_shape=jax.ShapeDtypeStruct((m, n), BF16),
        compiler_params=_cparams(("parallel", "arbitrary")),
        name="swiglu_matmul",
    )(a, wg, wu)


def _mm_ktiled_kernel(*refs, n_tail):
    a_ref, w_ref = refs[0], refs[1]
    t_refs, wt_ref, o_ref = refs[2:2 + n_tail], refs[2 + n_tail], refs[-1]
    k = pl.program_id(2)

    @pl.when(k == 0)
    def _():
        tail = jnp.concatenate([t[...] for t in t_refs], axis=1)
        o_ref[...] = _dot(a_ref[...], w_ref[...].astype(BF16)) + _dot(tail, wt_ref[...].astype(BF16))

    @pl.when(k > 0)
    def _():
        o_ref[...] += _dot(a_ref[...], w_ref[...].astype(BF16))


def matmul_ktiled(a, w, layer, bm, bn, bk):
    m, kdim = a.shape
    n = w.shape[2]
    bm, bn = _tile(m, bm), _tile(n, bn)
    nk = kdim // bk
    kmain = nk * bk
    ktail = kdim - kmain
    assert nk >= 1 and ktail > 0 and ktail % MXU_TILE == 0 and kmain % MXU_TILE == 0
    n_tail = ktail // MXU_TILE
    tb0 = kmain // MXU_TILE
    w_tail = w[layer, kmain:, :]
    tail_specs = [pl.BlockSpec((bm, MXU_TILE), functools.partial(lambda i, j, k, c: (i, c), c=tb0 + c))
                  for c in range(n_tail)]
    return pl.pallas_call(
        functools.partial(_mm_ktiled_kernel, n_tail=n_tail),
        grid=(m // bm, n // bn, nk),
        in_specs=[pl.BlockSpec((bm, bk), lambda i, j, k: (i, k)),
                  pl.BlockSpec((None, bk, bn), lambda i, j, k: (layer, k, j))]
        + tail_specs + [pl.BlockSpec((ktail, bn), lambda i, j, k: (0, j))],
        out_specs=pl.BlockSpec((bm, bn), lambda i, j, k: (i, j)),
        out_shape=jax.ShapeDtypeStruct((m, n), F32),
        compiler_params=_cparams(("parallel", "parallel", "arbitrary")),
        name="matmul_ktiled",
    )(a, w, *([a] * n_tail), w_tail)


def _ret_kernel(lg_ref, cos_ref, sin_ref, q_ref, k_ref, v_ref, g_ref, ng_ref, o_ref,
                sf_ref, sb_ref, ob_ref, qr_ref, kr_ref, dec_ref, vec_ref, *, nb, rows, c):
    t = pl.program_id(2)
    half = RET_HEAD_DIM // 2
    nch = rows // c

    @pl.when(t == 0)
    def _():
        lgf = lg_ref[0, 0:1, :]
        lgb = lg_ref[0, 1:2, :]
        i_row = lax.broadcasted_iota(jnp.int32, (c, half), 0).astype(F32)
        rel = (lax.broadcasted_iota(jnp.int32, (c, c), 0) - lax.broadcasted_iota(jnp.int32, (c, c), 1)).astype(F32)
        dec_ref[...] = jnp.exp(jnp.where(rel >= 0, rel * lgf[:, :1], -rel * lgb[:, :1]))
        vec_ref[0] = jnp.exp((i_row + 1.0) * lgf)
        vec_ref[1] = jnp.exp((c - 1.0 - i_row) * lgf)
        vec_ref[2] = jnp.exp((c - i_row) * lgb)
        vec_ref[3] = jnp.exp(i_row * lgb)
        vec_ref[4] = jnp.exp(c * lgf) + jnp.zeros((c, half), F32)
        vec_ref[5] = jnp.exp(c * lgb) + jnp.zeros((c, half), F32)
        sf_ref[...] = jnp.zeros_like(sf_ref)
        sb_ref[...] = jnp.zeros_like(sb_ref)

    def lanes2(d):
        return jnp.concatenate([d, d], axis=-1)

    @pl.when(t < nb)
    def _():
        base = (nb - 1 - t) * rows
        for ci in range(nch - 1, -1, -1):
            r0 = ci * c
            g0 = pl.multiple_of(base + r0, c)
            cos = cos_ref[r0:r0 + c, :]
            sin = sin_ref[r0:r0 + c, :]

            def rot(x):
                x1, x2 = x[:, :half], x[:, half:]
                return jnp.concatenate([x1 * cos - x2 * sin, x1 * sin + x2 * cos], axis=-1)

            q = rot(q_ref[r0:r0 + c, :]) * (RET_HEAD_DIM ** -0.5)
            k = rot(k_ref[r0:r0 + c, :])
            qr_ref[pl.ds(g0, c), :] = q
            kr_ref[pl.ds(g0, c), :] = k
            vb = v_ref[r0:r0 + c, :].astype(BF16)
            sb = sb_ref[...]
            ob_ref[pl.ds(g0, c), :] = _dot((q * lanes2(vec_ref[2])).astype(BF16), sb.astype(BF16))
            kd = k * lanes2(vec_ref[3])
            sb_ref[...] = vec_ref[5][0:1, 0:1] * sb + _dot_tn(kd.astype(BF16), vb)

    @pl.when(t >= nb)
    def _():
        base = (t - nb) * rows
        for ci in range(nch):
            r0 = ci * c
            g0 = pl.multiple_of(base + r0, c)
            q = qr_ref[pl.ds(g0, c), :]
            k = kr_ref[pl.ds(g0, c), :]
            vb = v_ref[r0:r0 + c, :].astype(BF16)
            sf = sf_ref[...]
            scores = _dot_nt(q.astype(BF16), k.astype(BF16)) * dec_ref[...]
            o = (_dot(scores.astype(BF16), vb) + _dot((q * lanes2(vec_ref[0])).astype(BF16), sf.astype(BF16))
                 + ob_ref[pl.ds(g0, c), :])
            kd = k * lanes2(vec_ref[1])
            sf_ref[...] = vec_ref[4][0:1, 0:1] * sf + _dot_tn(kd.astype(BF16), vb)
            mu = jnp.mean(o, axis=-1, keepdims=True)
            oc = o - mu
            var = jnp.mean(oc * oc, axis=-1, keepdims=True)
            y = oc * lax.rsqrt(var + GN_EPS) * ng_ref[...]
            o_ref[r0:r0 + c, :] = (_silu(g_ref[r0:r0 + c, :]) * y).astype(o_ref.dtype)


def retention_mixer(proj, norm_g, batch, seq, col0):
    m = proj.shape[0]
    width = norm_g.shape[0]
    heads = width // RET_HEAD_DIM
    c = min(RET_CHUNK, seq)
    rows = min(RET_ROWS, seq)
    nb = seq // rows
    half = RET_HEAD_DIM // 2
    cb0 = col0 // RET_HEAD_DIM

    hidx = jnp.arange(heads, dtype=F32)
    gamma = 1.0 - jnp.power(2.0, -5.0 - hidx)
    lg = jnp.log(gamma)
    lg2 = jnp.broadcast_to(jnp.stack([lg, lg[::-1]], axis=1)[:, :, None], (heads, 2, LANES))
    inv = jnp.power(ROPE_BASE, -jnp.arange(half, dtype=F32) / half)
    ang = jnp.arange(seq, dtype=F32)[:, None] * inv[None, :]
    cos, sin = jnp.cos(ang), jnp.sin(ang)

    def bblk(t):
        return jnp.maximum(nb - 1 - t, 0)

    def fblk(t):
        return jnp.maximum(t - nb, 0)

    def blk(t):
        return jnp.where(t < nb, nb - 1 - t, t - nb)

    def spec(j, which):
        return pl.BlockSpec((rows, RET_HEAD_DIM), lambda b, h, t: (b * nb + which(t), cb0 + j * heads + h))

    return pl.pallas_call(
        functools.partial(_ret_kernel, nb=nb, rows=rows, c=c),
        grid=(batch, heads, 2 * nb),
        in_specs=[pl.BlockSpec((1, 2, LANES), lambda b, h, t: (h, 0, 0)),
                  pl.BlockSpec((rows, half), lambda b, h, t: (bblk(t), 0)),
                  pl.BlockSpec((rows, half), lambda b, h, t: (bblk(t), 0)),
                  spec(0, bblk), spec(1, bblk), spec(2, blk), spec(3, fblk),
                  pl.BlockSpec((1, RET_HEAD_DIM), lambda b, h, t: (0, h))],
        out_specs=pl.BlockSpec((rows, RET_HEAD_DIM), lambda b, h, t: (b * nb + fblk(t), h)),
        out_shape=jax.ShapeDtypeStruct((m, width), BF16),
        scratch_shapes=[pltpu.VMEM((RET_HEAD_DIM, RET_HEAD_DIM), F32),
                        pltpu.VMEM((RET_HEAD_DIM, RET_HEAD_DIM), F32),
                        pltpu.VMEM((seq, RET_HEAD_DIM), F32),
                        pltpu.VMEM((seq, RET_HEAD_DIM), F32),
                        pltpu.VMEM((seq, RET_HEAD_DIM), F32),
                        pltpu.VMEM((c, c), F32),
                        pltpu.VMEM((6, c, half), F32)],
        compiler_params=_cparams(("parallel", "parallel", "arbitrary")),
        name="retention",
    )(lg2, cos, sin, proj, proj, proj, proj, norm_g.reshape(1, width))


def _hgrn_kernel(lbl_ref, lvl_ref, up_ref, sg_ref, cf_ref, q_ref, z_ref, v_ref, g_ref, ng_ref, a_ref, w_ref,
                 o_ref, po_ref, st_ref, ob_ref, *, nb, rows, sub, layer, depth):
    t = pl.program_id(2)
    d = HGRN_HEAD_DIM
    nlev = sub.bit_length() - 1

    def lower_bound(dirn):
        x = [lbl_ref[2 * l + dirn: 2 * l + dirn + 1, :] for l in range(depth)]
        mx = functools.reduce(jnp.maximum, x)
        e = [jnp.exp(xi - mx) for xi in x]
        tot = functools.reduce(lambda a, b: a + b, e)
        p = [ei / tot for ei in e]
        return functools.reduce(lambda a, b: a + b, p[:layer + 1]) - p[0]

    r_i = lax.broadcasted_iota(jnp.int32, (sub, sub), 0)
    c_i = lax.broadcasted_iota(jnp.int32, (sub, sub), 1)

    def split3(x):
        hi = x.astype(BF16)
        r1 = x - hi.astype(F32)
        mid = r1.astype(BF16)
        lo = (r1 - mid.astype(F32)).astype(BF16)
        return jnp.concatenate([hi, mid, lo], axis=1)

    def group_rows(x, size, pick):
        parts = [jnp.broadcast_to(x[base + pick:base + pick + 1, :], (size, d)) for base in range(0, sub, size)]
        return parts[0] if len(parts) == 1 else jnp.concatenate(parts, axis=0)

    def run(r0, rev, lb):
        f = lb + (1.0 - lb) * (1.0 / (1.0 + jnp.exp(-z_ref[r0:r0 + sub, :])))
        kk = 1.0 - f
        lf = jnp.log(f)
        q = q_ref[r0:r0 + sub, :] * (d ** -0.5)
        vb = v_ref[r0:r0 + sub, :].astype(BF16)

        tri = ((c_i >= r_i) if rev else (c_i <= r_i)).astype(BF16)
        parts = _dot(tri, split3(lf))
        cum = parts[:, :d] + parts[:, d:2 * d] + parts[:, 2 * d:]
        edge = cum[0:1, :] if rev else cum[sub - 1:sub, :]

        st = st_ref[...]
        o = _dot_nt((q * jnp.exp(cum)).astype(BF16), st.astype(BF16))
        kd = kk * jnp.exp(edge - cum)
        st_ref[...] = st * jnp.exp(edge) + _dot_tn(vb, kd.astype(BF16))

        lvl = lvl_ref[1 if rev else 0]
        a = jnp.where(lvl == nlev, jnp.sum(q * kk, axis=-1, keepdims=True), 0.0)
        lf_up = pltpu.roll(lf, sub - 1, 0)
        lf_dn = pltpu.roll(lf, 1, 0)
        qk = q - kk
        for lev in range(nlev):
            m = 1 << lev
            di = 1 if rev else 0
            if m == 1:
                arg = lf * cf_ref[di, 0]
            elif m == 2:
                arg = lf_up * cf_ref[di, 1] + lf * cf_ref[di, 2] + lf_dn * cf_ref[di, 3]
            else:
                bound = group_rows(cum, 2 * m, m if rev else m - 1)
                arg = ((bound - cum) if rev else (cum - bound)) * sg_ref[lev]
            up = up_ref[lev]
            y = ((q - up * qk) if rev else (kk + up * qk)) * jnp.exp2(arg)
            yb = y.astype(BF16)
            a = jnp.where(lvl == lev, _dot_nt(yb, yb), a)
        return o + _dot(a.astype(BF16), vb)

    nsb = rows // sub

    @pl.when(t == 0)
    def _():
        st_ref[...] = jnp.zeros_like(st_ref)

    srows = a_ref.shape[0] // nsb

    def side_matmul(si, wb):
        po_ref[si * srows:(si + 1) * srows, :] = _dot(a_ref[si * srows:(si + 1) * srows, :], wb)

    @pl.when(t < nb)
    def _():
        wb = w_ref[...].astype(BF16)
        lb = lower_bound(1)
        base = (nb - 1 - t) * rows
        for si in range(nsb - 1, -1, -1):
            side_matmul(si, wb)
            ob_ref[pl.ds(pl.multiple_of(base + si * sub, sub), sub), :] = run(si * sub, True, lb)

    @pl.when(t == nb)
    def _():
        st_ref[...] = jnp.zeros_like(st_ref)

    @pl.when(t >= nb)
    def _():
        wb = w_ref[...].astype(BF16)
        lb = lower_bound(0)
        base = (t - nb) * rows
        for si in range(nsb):
            side_matmul(si, wb)
            r0 = si * sub
            o = run(r0, False, lb) + ob_ref[pl.ds(pl.multiple_of(base + r0, sub), sub), :]
            o = o * lax.rsqrt(jnp.mean(o * o, axis=-1, keepdims=True) + RMS_EPS)
            o_ref[r0:r0 + sub, :] = (_silu(g_ref[r0:r0 + sub, :]) * (o * ng_ref[...])).astype(o_ref.dtype)


def hgrn2_mixer(proj, lb_logits, norm_g, layer, batch, seq, col0, side_a, side_w, side_cols):
    m = proj.shape[0]
    width = norm_g.shape[0]
    d = HGRN_HEAD_DIM
    heads = width // d
    depth = lb_logits.shape[0]
    rows = min(HGRN_ROWS, seq)
    nb = seq // rows
    cb0 = col0 // d

    def blk(t):
        return jnp.where(t < nb, nb - 1 - t, t - nb)

    def fblk(t):
        return jnp.maximum(t - nb, 0)

    def spec(j):
        return pl.BlockSpec((rows, d), lambda b, h, t: (b * nb + blk(t), cb0 + j * heads + h))

    zspec = pl.BlockSpec((rows, d),
                         lambda b, h, t: (b * nb + blk(t), cb0 + jnp.where(t < nb, 2, 1) * heads + h))
    sub = min(HGRN_SUB, rows)
    assert sub & (sub - 1) == 0 and rows % sub == 0
    nlev = sub.bit_length() - 1
    idx = np.arange(sub)
    x = idx[:, None] ^ idx[None, :]
    lev = np.where(x > 0, np.floor(np.log2(np.maximum(x, 1))), nlev).astype(np.int32)
    lvl = jnp.asarray(np.stack([np.where(idx[:, None] >= idx[None, :], lev, -1),
                                np.where(idx[:, None] <= idx[None, :], lev, -1)]), jnp.int32)
    upper = np.stack([((idx >> l) & 1).astype(np.float32) for l in range(nlev)])
    up = jnp.broadcast_to(jnp.asarray(upper)[:, :, None], (nlev, sub, d))
    sg = jnp.broadcast_to(jnp.asarray((2.0 * upper - 1.0) * np.float32(math.log2(math.e)))[:, :, None],
                          (nlev, sub, d))
    tab = pl.BlockSpec((nlev, sub, d), lambda b, h, t: (0, 0, 0))
    r4 = idx & 3
    pat = np.array([[[0, 1, 0, 1], [1, 0, 0, 0], [0, 0, 1, 1], [0, 0, 0, 1]],
                    [[1, 0, 1, 0], [1, 0, 0, 0], [1, 1, 0, 0], [0, 0, 0, 1]]], np.float32)
    cf = jnp.broadcast_to(jnp.asarray(pat[:, :, r4] * np.float32(math.log2(math.e)))[:, :, :, None],
                          (2, 4, sub, d))
    steps = batch * heads * 2 * nb
    kdim = side_a.shape[1]
    sbn = _tile(side_cols, SIDE_BN)
    col_tiles = side_cols // sbn
    assert steps % col_tiles == 0 and m % (steps // col_tiles) == 0
    sbm = m // (steps // col_tiles)

    def step(b, h, t):
        return (b * heads + h) * (2 * nb) + t

    return pl.pallas_call(
        functools.partial(_hgrn_kernel, nb=nb, rows=rows, sub=sub, layer=layer, depth=depth),
        grid=(batch, heads, 2 * nb),
        in_specs=[pl.BlockSpec((2 * depth, d), lambda b, h, t: (0, h)),
                  pl.BlockSpec((2, sub, sub), lambda b, h, t: (0, 0, 0)), tab, tab,
                  pl.BlockSpec((2, 4, sub, d), lambda b, h, t: (0, 0, 0, 0)),
                  spec(0), zspec, spec(3),
                  pl.BlockSpec((rows, d), lambda b, h, t: (b * nb + fblk(t), cb0 + 4 * heads + h)),
                  pl.BlockSpec((1, d), lambda b, h, t: (0, h)),
                  pl.BlockSpec((sbm, kdim), lambda b, h, t: (step(b, h, t) // col_tiles, 0)),
                  pl.BlockSpec((None, kdim, sbn), lambda b, h, t: (layer, 0, step(b, h, t) % col_tiles))],
        out_specs=[pl.BlockSpec((rows, d), lambda b, h, t: (b * nb + fblk(t), h)),
                   pl.BlockSpec((sbm, sbn), lambda b, h, t: (step(b, h, t) // col_tiles, step(b, h, t) % col_tiles))],
        out_shape=[jax.ShapeDtypeStruct((m, width), BF16), jax.ShapeDtypeStruct((m, side_cols), F32)],
        scratch_shapes=[pltpu.VMEM((d, d), F32),
                        pltpu.VMEM((seq, d), F32)],
        compiler_params=_cparams(("arbitrary", "arbitrary", "arbitrary")),
        name="hgrn2",
    )(lb_logits.reshape(2 * depth, width), lvl, up, sg, cf, proj, proj, proj, proj, norm_g.reshape(1, width),
      side_a, side_w)


def _dft_cos_sin(n):
    idx = np.arange(n)
    ang = 2.0 * np.pi * ((idx[:, None] * idx[None, :]) % n) / n
    return np.cos(ang), np.sin(ang)


def _fft_chan_kernel(z_ref, m_ref, vr_ref, vi_ref, *, groups):
    gd = FOURIER_GROUP_DIM
    mat = m_ref[...]
    for g in range(groups):
        pq = _dot3_rhs(z_ref[:, g * gd:(g + 1) * gd], mat)
        vr_ref[:, g * gd:(g + 1) * gd] = pq[:, :gd]
        vi_ref[:, g * gd:(g + 1) * gd] = pq[:, gd:]


def _fft_stage1_kernel(vr_ref, vi_ref, m_ref, twc_ref, tws_ref, ur_ref, ui_ref, *, n1, reps, nsub):
    for bi in range(nsub):
        x = jnp.concatenate([vr_ref[:, bi, :], vi_ref[:, bi, :]], axis=0)
        tt = _dot3_lhs(m_ref[...], x)
        tr, ti = tt[:n1], tt[n1:]
        cw = jnp.tile(twc_ref[bi], (1, reps))
        sw = jnp.tile(tws_ref[bi], (1, reps))
        ur_ref[:, bi, :] = tr * cw + ti * sw
        ui_ref[:, bi, :] = ti * cw - tr * sw


def _fft_stage2_kernel(ur_ref, ui_ref, m_ref, w_ref, b_ref, o_ref, y_ref, *, groups, scale, n2, nsub):
    gd = FOURIER_GROUP_DIM
    for ki in range(nsub):
        x = jnp.concatenate([ur_ref[ki * n2:(ki + 1) * n2, :], ui_ref[ki * n2:(ki + 1) * n2, :]], axis=0)
        spec = _dot3_lhs(m_ref[...], x) * scale
        for g in range(groups):
            y = _dot(spec[:, g * gd:(g + 1) * gd].astype(BF16), w_ref[g].astype(BF16))
            y_ref[:, ki, g * gd:(g + 1) * gd] = y + b_ref[:, g * gd:(g + 1) * gd]
    o_ref[...] = y_ref[...].astype(o_ref.dtype)


def fourier_mixer(proj, w, bias, batch, seq, col0):
    m = proj.shape[0]
    groups, gd, _ = w.shape
    width = groups * gd
    n1 = min(FFT_N1, seq)
    n2 = seq // n1
    br = min(512, m)

    cc, sc = _dft_cos_sin(gd)
    chan = _const3_rhs(np.concatenate([cc, -sc], axis=1))
    vr, vi = pl.pallas_call(
        functools.partial(_fft_chan_kernel, groups=groups),
        grid=(m // br,),
        in_specs=[pl.BlockSpec((br, width), lambda i: (i, col0 // width)),
                  pl.BlockSpec((3 * gd, 2 * gd), lambda i: (0, 0))],
        out_specs=[pl.BlockSpec((br, width), lambda i: (i, 0))] * 2,
        out_shape=[jax.ShapeDtypeStruct((m, width), F32)] * 2,
        compiler_params=_cparams(("parallel",)),
        name="fft_channels",
    )(proj, chan)

    c1, s1 = _dft_cos_sin(n1)
    m1 = _const3_lhs(np.block([[c1, s1], [-s1, c1]]))
    k1 = np.arange(n1)
    bb = np.arange(n2)
    tw = 2.0 * np.pi * ((bb[:, None] * k1[None, :]) % seq) / seq
    twc = jnp.broadcast_to(jnp.asarray(np.cos(tw), F32)[:, :, None], (n2, n1, LANES))
    tws = jnp.broadcast_to(jnp.asarray(np.sin(tw), F32)[:, :, None], (n2, n1, LANES))
    s1 = min(FFT_SUB1, n2)
    vspec = pl.BlockSpec((n1, s1, width), lambda bt, j: (bt, j, 0))
    tspec = pl.BlockSpec((s1, n1, LANES), lambda bt, j: (j, 0, 0))
    ur, ui = pl.pallas_call(
        functools.partial(_fft_stage1_kernel, n1=n1, reps=width // LANES, nsub=s1),
        grid=(batch, n2 // s1),
        in_specs=[vspec, vspec, pl.BlockSpec((2 * n1, 6 * n1), lambda bt, j: (0, 0)), tspec, tspec],
        out_specs=[vspec, vspec],
        out_shape=[jax.ShapeDtypeStruct((batch * n1, n2, width), F32)] * 2,
        compiler_params=_cparams(("parallel", "parallel")),
        name="fft_stage1",
    )(vr.reshape(batch * n1, n2, width), vi.reshape(batch * n1, n2, width), m1, twc, tws)

    c2, s2 = _dft_cos_sin(n2)
    m2 = _const3_lhs(np.concatenate([c2, s2], axis=1))
    s2 = min(FFT_SUB2, n1)
    uspec = pl.BlockSpec((s2 * n2, width), lambda bt, j: (bt * (n1 // s2) + j, 0))
    out = pl.pallas_call(
        functools.partial(_fft_stage2_kernel, groups=groups, scale=1.0 / math.sqrt(seq * gd), n2=n2, nsub=s2),
        grid=(batch, n1 // s2),
        in_specs=[uspec, uspec, pl.BlockSpec((n2, 6 * n2), lambda bt, j: (0, 0)),
                  pl.BlockSpec((groups, gd, gd), lambda bt, j: (0, 0, 0)),
                  pl.BlockSpec((1, width), lambda bt, j: (0, 0))],
        out_specs=pl.BlockSpec((n2, s2, width), lambda bt, j: (bt, j, 0)),
        out_shape=jax.ShapeDtypeStruct((batch * n2, n1, width), BF16),
        scratch_shapes=[pltpu.VMEM((n2, s2, width), F32)],
        compiler_params=_cparams(("parallel", "parallel")),
        name="fft_stage2",
    )(ur.reshape(m, width), ui.reshape(m, width), m2, w, bias.reshape(1, width))
    return out.reshape(m, width)


def _xattn_block_kernel(h_ref, y_ref, g_ref, wq_ref, wo_ref, k_ref, v_ref, ho_ref, f_ref, *, heads):
    hd = XATTN_HEAD_DIM
    h1 = h_ref[...] + _rms(y_ref[...], g_ref[0:1, :])
    c = _rms(h1, g_ref[1:2, :]).astype(BF16)
    q = (_dot(c, wq_ref[...]) * (hd ** -0.5)).astype(BF16)
    outs = []
    for hh in range(heads):
        cols = slice(hh * hd, (hh + 1) * hd)
        s = _dot_nt(q[:, cols], k_ref[:, cols])
        p = jnp.exp(s - jnp.max(s, axis=-1, keepdims=True))
        denom = jnp.sum(p, axis=-1, keepdims=True)
        outs.append((_dot(p.astype(BF16), v_ref[:, cols]) / denom).astype(BF16))
    xa = _dot(jnp.concatenate(outs, axis=1), wo_ref[...])
    h2 = h1 + _rms(xa, g_ref[2:3, :])
    ho_ref[...] = h2
    f_ref[...] = _rms(h2, g_ref[3:4, :]).astype(f_ref.dtype)


def xattn_block(h, y, gains, wq, wo, kx, vx, seq, mem_tokens):
    m, d = h.shape
    width = wq.shape[1]
    bm = min(XATTN_ROWS, seq)
    nb = seq // bm
    row = pl.BlockSpec((bm, d), lambda i: (i, 0))

    def const(shape):
        return pl.BlockSpec(shape, lambda i: (0, 0), pipeline_mode=pl.Buffered(1))

    kv = pl.BlockSpec((mem_tokens, width), lambda i: (i // nb, 0))
    return pl.pallas_call(
        functools.partial(_xattn_block_kernel, heads=width // XATTN_HEAD_DIM),
        grid=(m // bm,),
        in_specs=[row, row, const((4, d)), const((d, width)), const((width, d)), kv, kv],
        out_specs=[row, row],
        out_shape=[jax.ShapeDtypeStruct((m, d), F32), jax.ShapeDtypeStruct((m, d), BF16)],
        compiler_params=_cparams(("arbitrary",)),
        name="xattn_block",
    )(h, y, gains, wq, wo, kx, vx)


def kernel(x, mem, mem_norm_g, pre_mix_g, w_in, ret_norm_g, hgrn_lb_logits, hgrn_norm_g, fourier_w, fourier_b,
           w_out, post_mix_g, pre_xattn_g, xattn_wq, xattn_wk, xattn_wv, xattn_wo, post_xattn_g, pre_ffn_g,
           ffn_w_gate, ffn_w_up, ffn_w_down, post_ffn_g):
    batch, seq, d = x.shape
    depth = w_in.shape[0]
    mem_tokens = mem.shape[1]
    m = batch * seq
    ret_w = ret_norm_g.shape[1]
    hgrn_w = hgrn_norm_g.shape[1]

    h = x.reshape(m, d)
    mem_n = rmsnorm_rows(mem.reshape(batch * mem_tokens, d), mem_norm_g)
    a = rmsnorm_rows(h, pre_mix_g[0])
    for l in range(depth):
        proj_hf = matmul([a], w_in, l, F32, 2048, 512, col0=4 * ret_w)
        y_hgrn, proj_r = hgrn2_mixer(proj_hf, hgrn_lb_logits, hgrn_norm_g[l], l, batch, seq, 0, a, w_in, 4 * ret_w)
        y_ret = retention_mixer(proj_r, ret_norm_g[l], batch, seq, 0)
        y_fft = fourier_mixer(proj_hf, fourier_w[l], fourier_b[l], batch, seq, 5 * hgrn_w)
        mixed = matmul([y_ret, y_hgrn, y_fft], w_out, l, F32, 2048, 512)
        kx = matmul([mem_n], xattn_wk, l, BF16, 512, 512)
        vx = matmul([mem_n], xattn_wv, l, BF16, 512, 512)
        gains = jnp.stack([post_mix_g[l], pre_xattn_g[l], post_xattn_g[l], pre_ffn_g[l]])
        h, f = xattn_block(h, mixed, gains, xattn_wq[l].astype(BF16), xattn_wo[l].astype(BF16), kx, vx,
                           seq, mem_tokens)
        hid = swiglu_matmul(f, ffn_w_gate, ffn_w_up, l, 2048, 256)
        ff = matmul_ktiled(hid, ffn_w_down, l, 2048, 1024, 1024)
        h, a = add_norm(h, ff, post_ffn_g[l], pre_mix_g[l + 1] if l + 1 < depth else None)
    return h.reshape(batch, seq, d)
```

```python
import functools
import math

import numpy as np
import jax
import jax.numpy as jnp
from jax import lax
from jax.experimental import pallas as pl
from jax.experimental.pallas import tpu as pltpu

F32 = jnp.float32
BF16 = jnp.bfloat16

RET_HEAD_DIM = 256
HGRN_HEAD_DIM = 128
FOURIER_GROUP_DIM = 128
XATTN_HEADS = 4
XATTN_HEAD_DIM = 256
ROPE_BASE = 10000.0
RMS_EPS = 1e-6
GN_EPS = 1e-5

LANES = 128
MXU_TILE = 256
V7X_VMEM_BYTES = 64 * 1024 * 1024
VMEM_LIMIT = V7X_VMEM_BYTES - 8 * 1024 * 1024

RET_CHUNK = MXU_TILE
RET_ROWS = 2048
HGRN_ROWS = 1024
HGRN_SUB = 128
XATTN_ROWS = 256
FFT_N1 = 64
FFT_SUB1 = 8
FFT_SUB2 = 16


def _cparams(sem):
    return pltpu.CompilerParams(dimension_semantics=sem, vmem_limit_bytes=VMEM_LIMIT)


def _tile(n, preferred):
    for t in range(min(preferred, n) // LANES * LANES, 0, -LANES):
        if n % t == 0:
            return t
    return n


def _dot(a, b):
    return jnp.dot(a, b, preferred_element_type=F32)


def _dot_nt(a, b):
    return lax.dot_general(a, b, (((1,), (1,)), ((), ())), preferred_element_type=F32)


def _dot_tn(a, b):
    return lax.dot_general(a, b, (((0,), (0,)), ((), ())), preferred_element_type=F32)


def _split2(x):
    hi = x.astype(BF16)
    return hi, (x - hi.astype(F32)).astype(BF16)


def _const3_lhs(mat):
    hi, lo = _split2(jnp.asarray(mat, F32))
    return jnp.concatenate([hi, hi, lo], axis=1)


def _const3_rhs(mat):
    hi, lo = _split2(jnp.asarray(mat, F32))
    return jnp.concatenate([hi, hi, lo], axis=0)


def _dot3_lhs(m3, x):
    hi, lo = _split2(x)
    return _dot(m3, jnp.concatenate([hi, lo, hi], axis=0))


def _dot3_rhs(x, m3):
    hi, lo = _split2(x)
    return _dot(jnp.concatenate([hi, lo, hi], axis=1), m3)


def _rms(x, g):
    ms = jnp.mean(x * x, axis=-1, keepdims=True)
    return x * lax.rsqrt(ms + RMS_EPS) * g


def _silu(x):
    return x * (1.0 / (1.0 + jnp.exp(-x)))


def _norm_kernel(x_ref, g_ref, o_ref):
    o_ref[...] = _rms(x_ref[...], g_ref[...]).astype(o_ref.dtype)


def rmsnorm_rows(x, g, br=256):
    m, d = x.shape
    out_dtype = BF16
    br = min(br, m)
    return pl.pallas_call(
        _norm_kernel,
        grid=(m // br,),
        in_specs=[pl.BlockSpec((br, d), lambda i: (i, 0)),
                  pl.BlockSpec((1, d), lambda i: (0, 0))],
        out_specs=pl.BlockSpec((br, d), lambda i: (i, 0)),
        out_shape=jax.ShapeDtypeStruct((m, d), out_dtype),
        compiler_params=_cparams(("parallel",)),
        name="rmsnorm_rows",
    )(x, g.reshape(1, d))


def _add_norm_kernel(h_ref, y_ref, gp_ref, gn_ref, ho_ref, a_ref):
    h = h_ref[...] + _rms(y_ref[...], gp_ref[...])
    ho_ref[...] = h
    a_ref[...] = _rms(h, gn_ref[...]).astype(a_ref.dtype)


def _add_norm_last_kernel(h_ref, y_ref, gp_ref, ho_ref):
    ho_ref[...] = h_ref[...] + _rms(y_ref[...], gp_ref[...])


def add_norm(h, y, g_post, g_next, br=256):
    m, d = h.shape
    row = pl.BlockSpec((br, d), lambda i: (i, 0))
    vec = pl.BlockSpec((1, d), lambda i: (0, 0))
    if g_next is None:
        return pl.pallas_call(
            _add_norm_last_kernel, grid=(m // br,),
            in_specs=[row, row, vec], out_specs=row,
            out_shape=jax.ShapeDtypeStruct((m, d), F32),
            compiler_params=_cparams(("parallel",)), name="add_norm_last",
        )(h, y, g_post.reshape(1, d)), None
    return pl.pallas_call(
        _add_norm_kernel, grid=(m // br,),
        in_specs=[row, row, vec, vec], out_specs=[row, row],
        out_shape=[jax.ShapeDtypeStruct((m, d), F32), jax.ShapeDtypeStruct((m, d), BF16)],
        compiler_params=_cparams(("parallel",)), name="add_norm",
    )(h, y, g_post.reshape(1, d), g_next.reshape(1, d))


def _row_resident_spec(bm, kdim):
    return pl.BlockSpec((bm, kdim), lambda i, j: (i, 0), pipeline_mode=pl.Buffered(1))


def _mm_kernel(*refs, k_parts):
    a_refs, w_ref, o_ref = refs[:len(k_parts)], refs[len(k_parts)], refs[-1]
    acc, r0 = None, 0
    for a_ref, kp in zip(a_refs, k_parts):
        part = _dot(a_ref[...], w_ref[r0:r0 + kp, :].astype(BF16))
        acc = part if acc is None else acc + part
        r0 += kp
    o_ref[...] = acc.astype(o_ref.dtype)


def matmul(a_parts, w, layer, out_dtype, bm, bn, col0=0, n_cols=None):
    m = a_parts[0].shape[0]
    k_parts = tuple(a.shape[1] for a in a_parts)
    kdim = w.shape[1]
    n = w.shape[2] - col0 if n_cols is None else n_cols
    assert sum(k_parts) == kdim
    bm, bn = _tile(m, bm), _tile(math.gcd(n, col0) if col0 else n, bn)
    cb0 = col0 // bn
    return pl.pallas_call(
        functools.partial(_mm_kernel, k_parts=k_parts),
        grid=(m // bm, n // bn),
        in_specs=[_row_resident_spec(bm, kp) for kp in k_parts]
        + [pl.BlockSpec((None, kdim, bn), lambda i, j: (layer, 0, cb0 + j))],
        out_specs=pl.BlockSpec((bm, bn), lambda i, j: (i, j)),
        out_shape=jax.ShapeDtypeStruct((m, n), out_dtype),
        compiler_params=_cparams(("parallel", "arbitrary")),
        name="matmul",
    )(*a_parts, w)


def _swiglu_kernel(a_ref, wg_ref, wu_ref, o_ref):
    a = a_ref[...]
    gate = _dot(a, wg_ref[...].astype(BF16))
    up = _dot(a, wu_ref[...].astype(BF16))
    o_ref[...] = (_silu(gate) * up).astype(o_ref.dtype)


def swiglu_matmul(a, wg, wu, layer, bm, bn):
    m, kdim = a.shape
    n = wg.shape[2]
    bm, bn = _tile(m, bm), _tile(n, bn)
    wspec = pl.BlockSpec((None, kdim, bn), lambda i, j: (layer, 0, j))
    return pl.pallas_call(
        _swiglu_kernel,
        grid=(m // bm, n // bn),
        in_specs=[_row_resident_spec(bm, kdim), wspec, wspec],
        out_specs=pl.BlockSpec((bm, bn), lambda i, j: (i, j)),
        out_shape=jax.ShapeDtypeStruct((m, n), BF16),
        compiler_params=_cparams(("parallel", "arbitrary")),
        name="swiglu_matmul",
    )(a, wg, wu)


def _mm_ktiled_kernel(*refs, n_tail):
    a_ref, w_ref = refs[0], refs[1]
    t_refs, wt_ref, o_ref = refs[2:2 + n_tail], refs[2 + n_tail], refs[-1]
    k = pl.program_id(2)

    @pl.when(k == 0)
    def _():
        tail = jnp.concatenate([t[...] for t in t_refs], axis=1)
        o_ref[...] = _dot(a_ref[...], w_ref[...].astype(BF16)) + _dot(tail, wt_ref[...].astype(BF16))

    @pl.when(k > 0)
    def _():
        o_ref[...] += _dot(a_ref[...], w_ref[...].astype(BF16))


def matmul_ktiled(a, w, layer, bm, bn, bk):
    m, kdim = a.shape
    n = w.shape[2]
    bm, bn = _tile(m, bm), _tile(n, bn)
    nk = kdim // bk
    kmain = nk * bk
    ktail = kdim - kmain
    assert nk >= 1 and ktail > 0 and ktail % MXU_TILE == 0 and kmain % MXU_TILE == 0
    n_tail = ktail // MXU_TILE
    tb0 = kmain // MXU_TILE
    w_tail = w[layer, kmain:, :]
    tail_specs = [pl.BlockSpec((bm, MXU_TILE), functools.partial(lambda i, j, k, c: (i, c), c=tb0 + c))
                  for c in range(n_tail)]
    return pl.pallas_call(
        functools.partial(_mm_ktiled_kernel, n_tail=n_tail),
        grid=(m // bm, n // bn, nk),
        in_specs=[pl.BlockSpec((bm, bk), lambda i, j, k: (i, k)),
                  pl.BlockSpec((None, bk, bn), lambda i, j, k: (layer, k, j))]
        + tail_specs + [pl.BlockSpec((ktail, bn), lambda i, j, k: (0, j))],
        out_specs=pl.BlockSpec((bm, bn), lambda i, j, k: (i, j)),
        out_shape=jax.ShapeDtypeStruct((m, n), F32),
        compiler_params=_cparams(("parallel", "parallel", "arbitrary")),
        name="matmul_ktiled",
    )(a, w, *([a] * n_tail), w_tail)


def _ret_kernel(lg_ref, cos_ref, sin_ref, q_ref, k_ref, v_ref, g_ref, ng_ref, o_ref,
                sf_ref, sb_ref, ob_ref, qr_ref, kr_ref, dec_ref, vec_ref, *, nb, rows, c):
    t = pl.program_id(2)
    half = RET_HEAD_DIM // 2
    nch = rows // c

    @pl.when(t == 0)
    def _():
        lgf = lg_ref[0, 0:1, :]
        lgb = lg_ref[0, 1:2, :]
        i_row = lax.broadcasted_iota(jnp.int32, (c, half), 0).astype(F32)
        rel = (lax.broadcasted_iota(jnp.int32, (c, c), 0) - lax.broadcasted_iota(jnp.int32, (c, c), 1)).astype(F32)
        dec_ref[...] = jnp.exp(jnp.where(rel >= 0, rel * lgf[:, :1], -rel * lgb[:, :1]))
        vec_ref[0] = jnp.exp((i_row + 1.0) * lgf)
        vec_ref[1] = jnp.exp((c - 1.0 - i_row) * lgf)
        vec_ref[2] = jnp.exp((c - i_row) * lgb)
        vec_ref[3] = jnp.exp(i_row * lgb)
        vec_ref[4] = jnp.exp(c * lgf) + jnp.zeros((c, half), F32)
        vec_ref[5] = jnp.exp(c * lgb) + jnp.zeros((c, half), F32)
        sf_ref[...] = jnp.zeros_like(sf_ref)
        sb_ref[...] = jnp.zeros_like(sb_ref)

    def lanes2(d):
        return jnp.concatenate([d, d], axis=-1)

    @pl.when(t < nb)
    def _():
        base = (nb - 1 - t) * rows
        for ci in range(nch - 1, -1, -1):
            r0 = ci * c
            g0 = pl.multiple_of(base + r0, c)
            cos = cos_ref[r0:r0 + c, :]
            sin = sin_ref[r0:r0 + c, :]

            def rot(x):
                x1, x2 = x[:, :half], x[:, half:]
                return jnp.concatenate([x1 * cos - x2 * sin, x1 * sin + x2 * cos], axis=-1)

            q = rot(q_ref[r0:r0 + c, :]) * (RET_HEAD_DIM ** -0.5)
            k = rot(k_ref[r0:r0 + c, :])
            qr_ref[pl.ds(g0, c), :] = q
            kr_ref[pl.ds(g0, c), :] = k
            vb = v_ref[r0:r0 + c, :].astype(BF16)
            sb = sb_ref[...]
            ob_ref[pl.ds(g0, c), :] = _dot((q * lanes2(vec_ref[2])).astype(BF16), sb.astype(BF16))
            kd = k * lanes2(vec_ref[3])
            sb_ref[...] = vec_ref[5][0:1, 0:1] * sb + _dot_tn(kd.astype(BF16), vb)

    @pl.when(t >= nb)
    def _():
        base = (t - nb) * rows
        for ci in range(nch):
            r0 = ci * c
            g0 = pl.multiple_of(base + r0, c)
            q = qr_ref[pl.ds(g0, c), :]
            k = kr_ref[pl.ds(g0, c), :]
            vb = v_ref[r0:r0 + c, :].astype(BF16)
            sf = sf_ref[...]
            scores = _dot_nt(q.astype(BF16), k.astype(BF16)) * dec_ref[...]
            o = (_dot(scores.astype(BF16), vb) + _dot((q * lanes2(vec_ref[0])).astype(BF16), sf.astype(BF16))
                 + ob_ref[pl.ds(g0, c), :])
            kd = k * lanes2(vec_ref[1])
            sf_ref[...] = vec_ref[4][0:1, 0:1] * sf + _dot_tn(kd.astype(BF16), vb)
            mu = jnp.mean(o, axis=-1, keepdims=True)
            oc = o - mu
            var = jnp.mean(oc * oc, axis=-1, keepdims=True)
            y = oc * lax.rsqrt(var + GN_EPS) * ng_ref[...]
            o_ref[r0:r0 + c, :] = (_silu(g_ref[r0:r0 + c, :]) * y).astype(o_ref.dtype)


def retention_mixer(proj, norm_g, batch, seq, col0):
    m = proj.shape[0]
    width = norm_g.shape[0]
    heads = width // RET_HEAD_DIM
    c = min(RET_CHUNK, seq)
    rows = min(RET_ROWS, seq)
    nb = seq // rows
    half = RET_HEAD_DIM // 2
    cb0 = col0 // RET_HEAD_DIM

    hidx = jnp.arange(heads, dtype=F32)
    gamma = 1.0 - jnp.power(2.0, -5.0 - hidx)
    lg = jnp.log(gamma)
    lg2 = jnp.broadcast_to(jnp.stack([lg, lg[::-1]], axis=1)[:, :, None], (heads, 2, LANES))
    inv = jnp.power(ROPE_BASE, -jnp.arange(half, dtype=F32) / half)
    ang = jnp.arange(seq, dtype=F32)[:, None] * inv[None, :]
    cos, sin = jnp.cos(ang), jnp.sin(ang)

    def bblk(t):
        return jnp.maximum(nb - 1 - t, 0)

    def fblk(t):
        return jnp.maximum(t - nb, 0)

    def blk(t):
        return jnp.where(t < nb, nb - 1 - t, t - nb)

    def spec(j, which):
        return pl.BlockSpec((rows, RET_HEAD_DIM), lambda b, h, t: (b * nb + which(t), cb0 + j * heads + h))

    return pl.pallas_call(
        functools.partial(_ret_kernel, nb=nb, rows=rows, c=c),
        grid=(batch, heads, 2 * nb),
        in_specs=[pl.BlockSpec((1, 2, LANES), lambda b, h, t: (h, 0, 0)),
                  pl.BlockSpec((rows, half), lambda b, h, t: (bblk(t), 0)),
                  pl.BlockSpec((rows, half), lambda b, h, t: (bblk(t), 0)),
                  spec(0, bblk), spec(1, bblk), spec(2, blk), spec(3, fblk),
                  pl.BlockSpec((1, RET_HEAD_DIM), lambda b, h, t: (0, h))],
        out_specs=pl.BlockSpec((rows, RET_HEAD_DIM), lambda b, h, t: (b * nb + fblk(t), h)),
        out_shape=jax.ShapeDtypeStruct((m, width), BF16),
        scratch_shapes=[pltpu.VMEM((RET_HEAD_DIM, RET_HEAD_DIM), F32),
                        pltpu.VMEM((RET_HEAD_DIM, RET_HEAD_DIM), F32),
                        pltpu.VMEM((seq, RET_HEAD_DIM), F32),
                        pltpu.VMEM((seq, RET_HEAD_DIM), F32),
                        pltpu.VMEM((seq, RET_HEAD_DIM), F32),
                        pltpu.VMEM((c, c), F32),
                        pltpu.VMEM((6, c, half), F32)],
        compiler_params=_cparams(("parallel", "parallel", "arbitrary")),
        name="retention",
    )(lg2, cos, sin, proj, proj, proj, proj, norm_g.reshape(1, width))


def _hgrn_kernel(lbl_ref, lvl_ref, up_ref, sg_ref, cf_ref, q_ref, z_ref, v_ref, g_ref, ng_ref, o_ref,
                 st_ref, ob_ref, *, nb, rows, sub, layer, depth):
    t = pl.program_id(2)
    d = HGRN_HEAD_DIM
    nlev = sub.bit_length() - 1

    def lower_bound(dirn):
        x = [lbl_ref[2 * l + dirn: 2 * l + dirn + 1, :] for l in range(depth)]
        mx = functools.reduce(jnp.maximum, x)
        e = [jnp.exp(xi - mx) for xi in x]
        tot = functools.reduce(lambda a, b: a + b, e)
        p = [ei / tot for ei in e]
        return functools.reduce(lambda a, b: a + b, p[:layer + 1]) - p[0]

    r_i = lax.broadcasted_iota(jnp.int32, (sub, sub), 0)
    c_i = lax.broadcasted_iota(jnp.int32, (sub, sub), 1)

    def split3(x):
        hi = x.astype(BF16)
        r1 = x - hi.astype(F32)
        mid = r1.astype(BF16)
        lo = (r1 - mid.astype(F32)).astype(BF16)
        return jnp.concatenate([hi, mid, lo], axis=1)

    def group_rows(x, size, pick):
        parts = [jnp.broadcast_to(x[base + pick:base + pick + 1, :], (size, d)) for base in range(0, sub, size)]
        return parts[0] if len(parts) == 1 else jnp.concatenate(parts, axis=0)

    def run(r0, rev, lb):
        f = lb + (1.0 - lb) * (1.0 / (1.0 + jnp.exp(-z_ref[r0:r0 + sub, :])))
        kk = 1.0 - f
        lf = jnp.log(f)
        q = q_ref[r0:r0 + sub, :] * (d ** -0.5)
        vb = v_ref[r0:r0 + sub, :].astype(BF16)

        tri = ((c_i >= r_i) if rev else (c_i <= r_i)).astype(BF16)
        parts = _dot(tri, split3(lf))
        cum = parts[:, :d] + parts[:, d:2 * d] + parts[:, 2 * d:]
        edge = cum[0:1, :] if rev else cum[sub - 1:sub, :]

        st = st_ref[...]
        o = _dot_nt((q * jnp.exp(cum)).astype(BF16), st.astype(BF16))
        kd = kk * jnp.exp(edge - cum)
        st_ref[...] = st * jnp.exp(edge) + _dot_tn(vb, kd.astype(BF16))

        lvl = lvl_ref[1 if rev else 0]
        a = jnp.where(lvl == nlev, jnp.sum(q * kk, axis=-1, keepdims=True), 0.0)
        lf_up = pltpu.roll(lf, sub - 1, 0)
        lf_dn = pltpu.roll(lf, 1, 0)
        qk = q - kk
        for lev in range(nlev):
            m = 1 << lev
            di = 1 if rev else 0
            if m == 1:
                arg = lf * cf_ref[di, 0]
            elif m == 2:
                arg = lf_up * cf_ref[di, 1] + lf * cf_ref[di, 2] + lf_dn * cf_ref[di, 3]
            else:
                bound = group_rows(cum, 2 * m, m if rev else m - 1)
                arg = ((bound - cum) if rev else (cum - bound)) * sg_ref[lev]
            up = up_ref[lev]
            y = ((q - up * qk) if rev else (kk + up * qk)) * jnp.exp2(arg)
            yb = y.astype(BF16)
            a = jnp.where(lvl == lev, _dot_nt(yb, yb), a)
        return o + _dot(a.astype(BF16), vb)

    nsb = rows // sub

    @pl.when(t == 0)
    def _():
        st_ref[...] = jnp.zeros_like(st_ref)

    @pl.when(t < nb)
    def _():
        lb = lower_bound(1)
        base = (nb - 1 - t) * rows
        for si in range(nsb - 1, -1, -1):
            ob_ref[pl.ds(pl.multiple_of(base + si * sub, sub), sub), :] = run(si * sub, True, lb)

    @pl.when(t == nb)
    def _():
        st_ref[...] = jnp.zeros_like(st_ref)

    @pl.when(t >= nb)
    def _():
        lb = lower_bound(0)
        base = (t - nb) * rows
        for si in range(nsb):
            r0 = si * sub
            o = run(r0, False, lb) + ob_ref[pl.ds(pl.multiple_of(base + r0, sub), sub), :]
            o = o * lax.rsqrt(jnp.mean(o * o, axis=-1, keepdims=True) + RMS_EPS)
            o_ref[r0:r0 + sub, :] = (_silu(g_ref[r0:r0 + sub, :]) * (o * ng_ref[...])).astype(o_ref.dtype)


def hgrn2_mixer(proj, lb_logits, norm_g, layer, batch, seq, col0):
    m = proj.shape[0]
    width = norm_g.shape[0]
    d = HGRN_HEAD_DIM
    heads = width // d
    depth = lb_logits.shape[0]
    rows = min(HGRN_ROWS, seq)
    nb = seq // rows
    cb0 = col0 // d

    def blk(t):
        return jnp.where(t < nb, nb - 1 - t, t - nb)

    def fblk(t):
        return jnp.maximum(t - nb, 0)

    def spec(j):
        return pl.BlockSpec((rows, d), lambda b, h, t: (b * nb + blk(t), cb0 + j * heads + h))

    zspec = pl.BlockSpec((rows, d),
                         lambda b, h, t: (b * nb + blk(t), cb0 + jnp.where(t < nb, 2, 1) * heads + h))
    sub = min(HGRN_SUB, rows)
    assert sub & (sub - 1) == 0 and rows % sub == 0
    nlev = sub.bit_length() - 1
    idx = np.arange(sub)
    x = idx[:, None] ^ idx[None, :]
    lev = np.where(x > 0, np.floor(np.log2(np.maximum(x, 1))), nlev).astype(np.int32)
    lvl = jnp.asarray(np.stack([np.where(idx[:, None] >= idx[None, :], lev, -1),
                                np.where(idx[:, None] <= idx[None, :], lev, -1)]), jnp.int32)
    upper = np.stack([((idx >> l) & 1).astype(np.float32) for l in range(nlev)])
    up = jnp.broadcast_to(jnp.asarray(upper)[:, :, None], (nlev, sub, d))
    sg = jnp.broadcast_to(jnp.asarray((2.0 * upper - 1.0) * np.float32(math.log2(math.e)))[:, :, None],
                          (nlev, sub, d))
    tab = pl.BlockSpec((nlev, sub, d), lambda b, h, t: (0, 0, 0))
    r4 = idx & 3
    pat = np.array([[[0, 1, 0, 1], [1, 0, 0, 0], [0, 0, 1, 1], [0, 0, 0, 1]],
                    [[1, 0, 1, 0], [1, 0, 0, 0], [1, 1, 0, 0], [0, 0, 0, 1]]], np.float32)
    cf = jnp.broadcast_to(jnp.asarray(pat[:, :, r4] * np.float32(math.log2(math.e)))[:, :, :, None],
                          (2, 4, sub, d))
    return pl.pallas_call(
        functools.partial(_hgrn_kernel, nb=nb, rows=rows, sub=sub, layer=layer, depth=depth),
        grid=(batch, heads, 2 * nb),
        in_specs=[pl.BlockSpec((2 * depth, d), lambda b, h, t: (0, h)),
                  pl.BlockSpec((2, sub, sub), lambda b, h, t: (0, 0, 0)), tab, tab,
                  pl.BlockSpec((2, 4, sub, d), lambda b, h, t: (0, 0, 0, 0)),
                  spec(0), zspec, spec(3),
                  pl.BlockSpec((rows, d), lambda b, h, t: (b * nb + fblk(t), cb0 + 4 * heads + h)),
                  pl.BlockSpec((1, d), lambda b, h, t: (0, h))],
        out_specs=pl.BlockSpec((rows, d), lambda b, h, t: (b * nb + fblk(t), h)),
        out_shape=jax.ShapeDtypeStruct((m, width), BF16),
        scratch_shapes=[pltpu.VMEM((d, d), F32),
                        pltpu.VMEM((seq, d), F32)],
        compiler_params=_cparams(("parallel", "parallel", "arbitrary")),
        name="hgrn2",
    )(lb_logits.reshape(2 * depth, width), lvl, up, sg, cf, proj, proj, proj, proj, norm_g.reshape(1, width))


def _dft_cos_sin(n):
    idx = np.arange(n)
    ang = 2.0 * np.pi * ((idx[:, None] * idx[None, :]) % n) / n
    return np.cos(ang), np.sin(ang)


def _fft_chan_kernel(z_ref, m_ref, vr_ref, vi_ref, *, groups):
    gd = FOURIER_GROUP_DIM
    mat = m_ref[...]
    for g in range(groups):
        pq = _dot3_rhs(z_ref[:, g * gd:(g + 1) * gd], mat)
        vr_ref[:, g * gd:(g + 1) * gd] = pq[:, :gd]
        vi_ref[:, g * gd:(g + 1) * gd] = pq[:, gd:]


def _fft_stage1_kernel(vr_ref, vi_ref, m_ref, twc_ref, tws_ref, ur_ref, ui_ref, *, n1, reps, nsub):
    for bi in range(nsub):
        x = jnp.concatenate([vr_ref[:, bi, :], vi_ref[:, bi, :]], axis=0)
        tt = _dot3_lhs(m_ref[...], x)
        tr, ti = tt[:n1], tt[n1:]
        cw = jnp.tile(twc_ref[bi], (1, reps))
        sw = jnp.tile(tws_ref[bi], (1, reps))
        ur_ref[:, bi, :] = tr * cw + ti * sw
        ui_ref[:, bi, :] = ti * cw - tr * sw


def _fft_stage2_kernel(ur_ref, ui_ref, m_ref, w_ref, b_ref, o_ref, y_ref, *, groups, scale, n2, nsub):
    gd = FOURIER_GROUP_DIM
    for ki in range(nsub):
        x = jnp.concatenate([ur_ref[ki * n2:(ki + 1) * n2, :], ui_ref[ki * n2:(ki + 1) * n2, :]], axis=0)
        spec = _dot3_lhs(m_ref[...], x) * scale
        for g in range(groups):
            y = _dot(spec[:, g * gd:(g + 1) * gd].astype(BF16), w_ref[g].astype(BF16))
            y_ref[:, ki, g * gd:(g + 1) * gd] = y + b_ref[:, g * gd:(g + 1) * gd]
    o_ref[...] = y_ref[...].astype(o_ref.dtype)


def fourier_mixer(proj, w, bias, batch, seq, col0):
    m = proj.shape[0]
    groups, gd, _ = w.shape
    width = groups * gd
    n1 = min(FFT_N1, seq)
    n2 = seq // n1
    br = min(512, m)

    cc, sc = _dft_cos_sin(gd)
    chan = _const3_rhs(np.concatenate([cc, -sc], axis=1))
    vr, vi = pl.pallas_call(
        functools.partial(_fft_chan_kernel, groups=groups),
        grid=(m // br,),
        in_specs=[pl.BlockSpec((br, width), lambda i: (i, col0 // width)),
                  pl.BlockSpec((3 * gd, 2 * gd), lambda i: (0, 0))],
        out_specs=[pl.BlockSpec((br, width), lambda i: (i, 0))] * 2,
        out_shape=[jax.ShapeDtypeStruct((m, width), F32)] * 2,
        compiler_params=_cparams(("parallel",)),
        name="fft_channels",
    )(proj, chan)

    c1, s1 = _dft_cos_sin(n1)
    m1 = _const3_lhs(np.block([[c1, s1], [-s1, c1]]))
    k1 = np.arange(n1)
    bb = np.arange(n2)
    tw = 2.0 * np.pi * ((bb[:, None] * k1[None, :]) % seq) / seq
    twc = jnp.broadcast_to(jnp.asarray(np.cos(tw), F32)[:, :, None], (n2, n1, LANES))
    tws = jnp.broadcast_to(jnp.asarray(np.sin(tw), F32)[:, :, None], (n2, n1, LANES))
    s1 = min(FFT_SUB1, n2)
    vspec = pl.BlockSpec((n1, s1, width), lambda bt, j: (bt, j, 0))
    tspec = pl.BlockSpec((s1, n1, LANES), lambda bt, j: (j, 0, 0))
    ur, ui = pl.pallas_call(
        functools.partial(_fft_stage1_kernel, n1=n1, reps=width // LANES, nsub=s1),
        grid=(batch, n2 // s1),
        in_specs=[vspec, vspec, pl.BlockSpec((2 * n1, 6 * n1), lambda bt, j: (0, 0)), tspec, tspec],
        out_specs=[vspec, vspec],
        out_shape=[jax.ShapeDtypeStruct((batch * n1, n2, width), F32)] * 2,
        compiler_params=_cparams(("parallel", "parallel")),
        name="fft_stage1",
    )(vr.reshape(batch * n1, n2, width), vi.reshape(batch * n1, n2, width), m1, twc, tws)

    c2, s2 = _dft_cos_sin(n2)
    m2 = _const3_lhs(np.concatenate([c2, s2], axis=1))
    s2 = min(FFT_SUB2, n1)
    uspec = pl.BlockSpec((s2 * n2, width), lambda bt, j: (bt * (n1 // s2) + j, 0))
    out = pl.pallas_call(
        functools.partial(_fft_stage2_kernel, groups=groups, scale=1.0 / math.sqrt(seq * gd), n2=n2, nsub=s2),
        grid=(batch, n1 // s2),
        in_specs=[uspec, uspec, pl.BlockSpec((n2, 6 * n2), lambda bt, j: (0, 0)),
                  pl.BlockSpec((groups, gd, gd), lambda bt, j: (0, 0, 0)),
                  pl.BlockSpec((1, width), lambda bt, j: (0, 0))],
        out_specs=pl.BlockSpec((n2, s2, width), lambda bt, j: (bt, j, 0)),
        out_shape=jax.ShapeDtypeStruct((batch * n2, n1, width), BF16),
        scratch_shapes=[pltpu.VMEM((n2, s2, width), F32)],
        compiler_params=_cparams(("parallel", "parallel")),
        name="fft_stage2",
    )(ur.reshape(m, width), ui.reshape(m, width), m2, w, bias.reshape(1, width))
    return out.reshape(m, width)


def _xattn_block_kernel(h_ref, y_ref, g_ref, wq_ref, wo_ref, k_ref, v_ref, ho_ref, f_ref, *, heads):
    hd = XATTN_HEAD_DIM
    h1 = h_ref[...] + _rms(y_ref[...], g_ref[0:1, :])
    c = _rms(h1, g_ref[1:2, :]).astype(BF16)
    q = (_dot(c, wq_ref[...]) * (hd ** -0.5)).astype(BF16)
    outs = []
    for hh in range(heads):
        cols = slice(hh * hd, (hh + 1) * hd)
        s = _dot_nt(q[:, cols], k_ref[:, cols])
        p = jnp.exp(s - jnp.max(s, axis=-1, keepdims=True))
        denom = jnp.sum(p, axis=-1, keepdims=True)
        outs.append((_dot(p.astype(BF16), v_ref[:, cols]) / denom).astype(BF16))
    xa = _dot(jnp.concatenate(outs, axis=1), wo_ref[...])
    h2 = h1 + _rms(xa, g_ref[2:3, :])
    ho_ref[...] = h2
    f_ref[...] = _rms(h2, g_ref[3:4, :]).astype(f_ref.dtype)


def xattn_block(h, y, gains, wq, wo, kx, vx, seq, mem_tokens):
    m, d = h.shape
    width = wq.shape[1]
    bm = min(XATTN_ROWS, seq)
    nb = seq // bm
    row = pl.BlockSpec((bm, d), lambda i: (i, 0))

    def const(shape):
        return pl.BlockSpec(shape, lambda i: (0, 0), pipeline_mode=pl.Buffered(1))

    kv = pl.BlockSpec((mem_tokens, width), lambda i: (i // nb, 0))
    return pl.pallas_call(
        functools.partial(_xattn_block_kernel, heads=width // XATTN_HEAD_DIM),
        grid=(m // bm,),
        in_specs=[row, row, const((4, d)), const((d, width)), const((width, d)), kv, kv],
        out_specs=[row, row],
        out_shape=[jax.ShapeDtypeStruct((m, d), F32), jax.ShapeDtypeStruct((m, d), BF16)],
        compiler_params=_cparams(("arbitrary",)),
        name="xattn_block",
    )(h, y, gains, wq, wo, kx, vx)


def kernel(x, mem, mem_norm_g, pre_mix_g, w_in, ret_norm_g, hgrn_lb_logits, hgrn_norm_g, fourier_w, fourier_b,
           w_out, post_mix_g, pre_xattn_g, xattn_wq, xattn_wk, xattn_wv, xattn_wo, post_xattn_g, pre_ffn_g,
           ffn_w_gate, ffn_w_up, ffn_w_down, post_ffn_g):
    batch, seq, d = x.shape
    depth = w_in.shape[0]
    mem_tokens = mem.shape[1]
    m = batch * seq
    ret_w = ret_norm_g.shape[1]
    hgrn_w = hgrn_norm_g.shape[1]

    h = x.reshape(m, d)
    mem_n = rmsnorm_rows(mem.reshape(batch * mem_tokens, d), mem_norm_g)
    a = rmsnorm_rows(h, pre_mix_g[0])
    for l in range(depth):
        proj = matmul([a], w_in, l, F32, 2048, 512)
        y_ret = retention_mixer(proj, ret_norm_g[l], batch, seq, 0)
        y_hgrn = hgrn2_mixer(proj, hgrn_lb_logits, hgrn_norm_g[l], l, batch, seq, 4 * ret_w)
        y_fft = fourier_mixer(proj, fourier_w[l], fourier_b[l], batch, seq, 4 * ret_w + 5 * hgrn_w)
        mixed = matmul([y_ret, y_hgrn, y_fft], w_out, l, F32, 2048, 512)
        kx = matmul([mem_n], xattn_wk, l, BF16, 512, 512)
        vx = matmul([mem_n], xattn_wv, l, BF16, 512, 512)
        gains = jnp.stack([post_mix_g[l], pre_xattn_g[l], post_xattn_g[l], pre_ffn_g[l]])
        h, f = xattn_block(h, mixed, gains, xattn_wq[l].astype(BF16), xattn_wo[l].astype(BF16), kx, vx,
                           seq, mem_tokens)
        hid = swiglu_matmul(f, ffn_w_gate, ffn_w_up, l, 2048, 256)
        ff = matmul_ktiled(hid, ffn_w_down, l, 2048, 1024, 1024)
        h, a = add_norm(h, ff, post_ffn_g[l], pre_mix_g[l + 1] if l + 1 < depth else None)
    return h.reshape(batch, seq, d)
```

```python
import functools
import math

import numpy as np
import jax
import jax.numpy as jnp
from jax import lax
from jax.experimental import pallas as pl
from jax.experimental.pallas import tpu as pltpu

F32 = jnp.float32
BF16 = jnp.bfloat16

RET_HEAD_DIM = 256
HGRN_HEAD_DIM = 128
FOURIER_GROUP_DIM = 128
XATTN_HEAD_DIM = 256
ROPE_BASE = 10000.0
RMS_EPS = 1e-6
GN_EPS = 1e-5

LANES = 128
MXU_TILE = 256
V7X_VMEM_BYTES = 64 * 1024 * 1024
VMEM_LIMIT = V7X_VMEM_BYTES - 8 * 1024 * 1024

MM_TILE = (2048, 512)
MM_TILE_SWIGLU = (2048, 256)
MM_TILE_MEM = (512, 512)
MM_TILE_DOWN = (2048, 1024, 1024)
RET_CHUNK = MXU_TILE
RET_ROWS = 2048
HGRN_ROWS = 1024
HGRN_SUB = 128
XATTN_ROWS = 256
FFT_N1 = 64
FFT_SUB1 = 8
FFT_SUB2 = 16


def _cparams(sem):
    return pltpu.CompilerParams(dimension_semantics=sem, vmem_limit_bytes=VMEM_LIMIT)


def _tile(n, preferred):
    for t in range(min(preferred, n) // LANES * LANES, 0, -LANES):
        if n % t == 0:
            return t
    return n


def _dot(a, b):
    return jnp.dot(a, b, preferred_element_type=F32)


def _dot_nt(a, b):
    return lax.dot_general(a, b, (((1,), (1,)), ((), ())), preferred_element_type=F32)


def _dot_tn(a, b):
    return lax.dot_general(a, b, (((0,), (0,)), ((), ())), preferred_element_type=F32)


def _split2(x):
    hi = x.astype(BF16)
    return hi, (x - hi.astype(F32)).astype(BF16)


def _const3_lhs(mat):
    hi, lo = _split2(jnp.asarray(mat, F32))
    return jnp.concatenate([hi, hi, lo], axis=1)


def _const3_rhs(mat):
    hi, lo = _split2(jnp.asarray(mat, F32))
    return jnp.concatenate([hi, hi, lo], axis=0)


def _dot3_lhs(m3, x):
    hi, lo = _split2(x)
    return _dot(m3, jnp.concatenate([hi, lo, hi], axis=0))


def _dot3_rhs(x, m3):
    hi, lo = _split2(x)
    return _dot(jnp.concatenate([hi, lo, hi], axis=1), m3)


def _rms(x, g):
    ms = jnp.mean(x * x, axis=-1, keepdims=True)
    return x * lax.rsqrt(ms + RMS_EPS) * g


def _silu(x):
    return x * (1.0 / (1.0 + jnp.exp(-x)))


def _norm_kernel(x_ref, g_ref, o_ref):
    o_ref[...] = _rms(x_ref[...], g_ref[...]).astype(o_ref.dtype)


def rmsnorm_rows(x, g, br=256):
    m, d = x.shape
    out_dtype = BF16
    br = min(br, m)
    return pl.pallas_call(
        _norm_kernel,
        grid=(m // br,),
        in_specs=[pl.BlockSpec((br, d), lambda i: (i, 0)),
                  pl.BlockSpec((1, d), lambda i: (0, 0))],
        out_specs=pl.BlockSpec((br, d), lambda i: (i, 0)),
        out_shape=jax.ShapeDtypeStruct((m, d), out_dtype),
        compiler_params=_cparams(("parallel",)),
        name="rmsnorm_rows",
    )(x, g.reshape(1, d))


def _add_norm_kernel(h_ref, y_ref, gp_ref, gn_ref, ho_ref, a_ref):
    h = h_ref[...] + _rms(y_ref[...], gp_ref[...])
    ho_ref[...] = h
    a_ref[...] = _rms(h, gn_ref[...]).astype(a_ref.dtype)


def _add_norm_last_kernel(h_ref, y_ref, gp_ref, ho_ref):
    ho_ref[...] = h_ref[...] + _rms(y_ref[...], gp_ref[...])


def add_norm(h, y, g_post, g_next, br=256):
    m, d = h.shape
    row = pl.BlockSpec((br, d), lambda i: (i, 0))
    vec = pl.BlockSpec((1, d), lambda i: (0, 0))
    if g_next is None:
        return pl.pallas_call(
            _add_norm_last_kernel, grid=(m // br,),
            in_specs=[row, row, vec], out_specs=row,
            out_shape=jax.ShapeDtypeStruct((m, d), F32),
            compiler_params=_cparams(("parallel",)), name="add_norm_last",
        )(h, y, g_post.reshape(1, d)), None
    return pl.pallas_call(
        _add_norm_kernel, grid=(m // br,),
        in_specs=[row, row, vec, vec], out_specs=[row, row],
        out_shape=[jax.ShapeDtypeStruct((m, d), F32), jax.ShapeDtypeStruct((m, d), BF16)],
        compiler_params=_cparams(("parallel",)), name="add_norm",
    )(h, y, g_post.reshape(1, d), g_next.reshape(1, d))


def _row_resident_spec(bm, kdim):
    return pl.BlockSpec((bm, kdim), lambda i, j: (i, 0), pipeline_mode=pl.Buffered(1))


def _mm_kernel(*refs, k_parts):
    a_refs, w_ref, o_ref = refs[:len(k_parts)], refs[len(k_parts)], refs[-1]
    acc, r0 = None, 0
    for a_ref, kp in zip(a_refs, k_parts):
        part = _dot(a_ref[...], w_ref[r0:r0 + kp, :].astype(BF16))
        acc = part if acc is None else acc + part
        r0 += kp
    o_ref[...] = acc.astype(o_ref.dtype)


def matmul(a_parts, w, layer, out_dtype, bm, bn):
    m = a_parts[0].shape[0]
    k_parts = tuple(a.shape[1] for a in a_parts)
    kdim, n = w.shape[1], w.shape[2]
    assert sum(k_parts) == kdim
    bm, bn = _tile(m, bm), _tile(n, bn)
    return pl.pallas_call(
        functools.partial(_mm_kernel, k_parts=k_parts),
        grid=(m // bm, n // bn),
        in_specs=[_row_resident_spec(bm, kp) for kp in k_parts]
        + [pl.BlockSpec((None, kdim, bn), lambda i, j: (layer, 0, j))],
        out_specs=pl.BlockSpec((bm, bn), lambda i, j: (i, j)),
        out_shape=jax.ShapeDtypeStruct((m, n), out_dtype),
        compiler_params=_cparams(("parallel", "arbitrary")),
        name="matmul",
    )(*a_parts, w)


def _swiglu_kernel(a_ref, wg_ref, wu_ref, o_ref):
    a = a_ref[...]
    gate = _dot(a, wg_ref[...].astype(BF16))
    up = _dot(a, wu_ref[...].astype(BF16))
    o_ref[...] = (_silu(gate) * up).astype(o_ref.dtype)


def swiglu_matmul(a, wg, wu, layer, bm, bn):
    m, kdim = a.shape
    n = wg.shape[2]
    bm, bn = _tile(m, bm), _tile(n, bn)
    wspec = pl.BlockSpec((None, kdim, bn), lambda i, j: (layer, 0, j))
    return pl.pallas_call(
        _swiglu_kernel,
        grid=(m // bm, n // bn),
        in_specs=[_row_resident_spec(bm, kdim), wspec, wspec],
        out_specs=pl.BlockSpec((bm, bn), lambda i, j: (i, j)),
        out_shape=jax.ShapeDtypeStruct((m, n), BF16),
        compiler_params=_cparams(("parallel", "arbitrary")),
        name="swiglu_matmul",
    )(a, wg, wu)


def _mm_ktiled_kernel(*refs, n_tail):
    a_ref, w_ref = refs[0], refs[1]
    t_refs, wt_ref, o_ref = refs[2:2 + n_tail], refs[2 + n_tail], refs[-1]
    k = pl.program_id(2)

    @pl.when(k == 0)
    def _():
        tail = jnp.concatenate([t[...] for t in t_refs], axis=1)
        o_ref[...] = _dot(a_ref[...], w_ref[...].astype(BF16)) + _dot(tail, wt_ref[...].astype(BF16))

    @pl.when(k > 0)
    def _():
        o_ref[...] += _dot(a_ref[...], w_ref[...].astype(BF16))


def matmul_ktiled(a, w, layer, bm, bn, bk):
    m, kdim = a.shape
    n = w.shape[2]
    bm, bn = _tile(m, bm), _tile(n, bn)
    nk = kdim // bk
    kmain = nk * bk
    ktail = kdim - kmain
    assert nk >= 1 and ktail > 0 and ktail % MXU_TILE == 0 and kmain % MXU_TILE == 0
    n_tail = ktail // MXU_TILE
    tb0 = kmain // MXU_TILE
    w_tail = w[layer, kmain:, :]
    tail_specs = [pl.BlockSpec((bm, MXU_TILE), functools.partial(lambda i, j, k, c: (i, c), c=tb0 + c))
                  for c in range(n_tail)]
    return pl.pallas_call(
        functools.partial(_mm_ktiled_kernel, n_tail=n_tail),
        grid=(m // bm, n // bn, nk),
        in_specs=[pl.BlockSpec((bm, bk), lambda i, j, k: (i, k)),
                  pl.BlockSpec((None, bk, bn), lambda i, j, k: (layer, k, j))]
        + tail_specs + [pl.BlockSpec((ktail, bn), lambda i, j, k: (0, j))],
        out_specs=pl.BlockSpec((bm, bn), lambda i, j, k: (i, j)),
        out_shape=jax.ShapeDtypeStruct((m, n), F32),
        compiler_params=_cparams(("parallel", "parallel", "arbitrary")),
        name="matmul_ktiled",
    )(a, w, *([a] * n_tail), w_tail)


def _ret_kernel(lg_ref, cos_ref, sin_ref, q_ref, k_ref, v_ref, g_ref, ng_ref, o_ref,
                sf_ref, sb_ref, ob_ref, qr_ref, kr_ref, vb_ref, dec_ref, vec_ref, *, nb, rows, c):
    t = pl.program_id(2)
    half = RET_HEAD_DIM // 2
    nch = rows // c

    @pl.when(t == 0)
    def _():
        lgf = lg_ref[0, 0:1, :]
        lgb = lg_ref[0, 1:2, :]
        i_row = lax.broadcasted_iota(jnp.int32, (c, half), 0).astype(F32)
        rel = (lax.broadcasted_iota(jnp.int32, (c, c), 0) - lax.broadcasted_iota(jnp.int32, (c, c), 1)).astype(F32)
        dec_ref[...] = jnp.exp(jnp.where(rel >= 0, rel * lgf[:, :1], -rel * lgb[:, :1]))
        vec_ref[0] = jnp.exp((i_row + 1.0) * lgf)
        vec_ref[1] = jnp.exp((c - 1.0 - i_row) * lgf)
        vec_ref[2] = jnp.exp((c - i_row) * lgb)
        vec_ref[3] = jnp.exp(i_row * lgb)
        vec_ref[4] = jnp.exp(c * lgf) + jnp.zeros((c, half), F32)
        vec_ref[5] = jnp.exp(c * lgb) + jnp.zeros((c, half), F32)
        sf_ref[...] = jnp.zeros_like(sf_ref)
        sb_ref[...] = jnp.zeros_like(sb_ref)

    def lanes2(d):
        return jnp.concatenate([d, d], axis=-1)

    @pl.when(t < nb)
    def _():
        base = (nb - 1 - t) * rows
        for ci in range(nch - 1, -1, -1):
            r0 = ci * c
            g0 = pl.multiple_of(base + r0, c)
            cos = cos_ref[r0:r0 + c, :]
            sin = sin_ref[r0:r0 + c, :]

            def rot(x):
                x1, x2 = x[:, :half], x[:, half:]
                return jnp.concatenate([x1 * cos - x2 * sin, x1 * sin + x2 * cos], axis=-1)

            q = rot(q_ref[r0:r0 + c, :]) * (RET_HEAD_DIM ** -0.5)
            k = rot(k_ref[r0:r0 + c, :])
            qr_ref[pl.ds(g0, c), :] = q
            kr_ref[pl.ds(g0, c), :] = k
            vb = v_ref[r0:r0 + c, :].astype(BF16)
            vb_ref[pl.ds(g0, c), :] = vb
            sb = sb_ref[...]
            ob_ref[pl.ds(g0, c), :] = _dot((q * lanes2(vec_ref[2])).astype(BF16), sb.astype(BF16))
            kd = k * lanes2(vec_ref[3])
            sb_ref[...] = vec_ref[5][0:1, 0:1] * sb + _dot_tn(kd.astype(BF16), vb)

    @pl.when(t >= nb)
    def _():
        base = (t - nb) * rows
        for ci in range(nch):
            r0 = ci * c
            g0 = pl.multiple_of(base + r0, c)
            q = qr_ref[pl.ds(g0, c), :]
            k = kr_ref[pl.ds(g0, c), :]
            vb = vb_ref[pl.ds(g0, c), :]
            sf = sf_ref[...]
            scores = _dot_nt(q.astype(BF16), k.astype(BF16)) * dec_ref[...]
            o = (_dot(scores.astype(BF16), vb) + _dot((q * lanes2(vec_ref[0])).astype(BF16), sf.astype(BF16))
                 + ob_ref[pl.ds(g0, c), :])
            kd = k * lanes2(vec_ref[1])
            sf_ref[...] = vec_ref[4][0:1, 0:1] * sf + _dot_tn(kd.astype(BF16), vb)
            mu = jnp.mean(o, axis=-1, keepdims=True)
            oc = o - mu
            var = jnp.mean(oc * oc, axis=-1, keepdims=True)
            y = oc * lax.rsqrt(var + GN_EPS) * ng_ref[...]
            o_ref[r0:r0 + c, :] = (_silu(g_ref[r0:r0 + c, :]) * y).astype(o_ref.dtype)


def retention_mixer(proj, norm_g, batch, seq, col0):
    m = proj.shape[0]
    width = norm_g.shape[0]
    heads = width // RET_HEAD_DIM
    c = min(RET_CHUNK, seq)
    rows = min(RET_ROWS, seq)
    nb = seq // rows
    half = RET_HEAD_DIM // 2
    cb0 = col0 // RET_HEAD_DIM

    hidx = jnp.arange(heads, dtype=F32)
    gamma = 1.0 - jnp.power(2.0, -5.0 - hidx)
    lg = jnp.log(gamma)
    lg2 = jnp.broadcast_to(jnp.stack([lg, lg[::-1]], axis=1)[:, :, None], (heads, 2, LANES))
    inv = jnp.power(ROPE_BASE, -jnp.arange(half, dtype=F32) / half)
    ang = jnp.arange(seq, dtype=F32)[:, None] * inv[None, :]
    cos, sin = jnp.cos(ang), jnp.sin(ang)

    def bblk(t):
        return jnp.maximum(nb - 1 - t, 0)

    def fblk(t):
        return jnp.maximum(t - nb, 0)

    def spec(j, which):
        return pl.BlockSpec((rows, RET_HEAD_DIM), lambda b, h, t: (b * nb + which(t), cb0 + j * heads + h))

    return pl.pallas_call(
        functools.partial(_ret_kernel, nb=nb, rows=rows, c=c),
        grid=(batch, heads, 2 * nb),
        in_specs=[pl.BlockSpec((1, 2, LANES), lambda b, h, t: (h, 0, 0)),
                  pl.BlockSpec((rows, half), lambda b, h, t: (bblk(t), 0)),
                  pl.BlockSpec((rows, half), lambda b, h, t: (bblk(t), 0)),
                  spec(0, bblk), spec(1, bblk), spec(2, bblk), spec(3, fblk),
                  pl.BlockSpec((1, RET_HEAD_DIM), lambda b, h, t: (0, h))],
        out_specs=pl.BlockSpec((rows, RET_HEAD_DIM), lambda b, h, t: (b * nb + fblk(t), h)),
        out_shape=jax.ShapeDtypeStruct((m, width), BF16),
        scratch_shapes=[pltpu.VMEM((RET_HEAD_DIM, RET_HEAD_DIM), F32),
                        pltpu.VMEM((RET_HEAD_DIM, RET_HEAD_DIM), F32),
                        pltpu.VMEM((seq, RET_HEAD_DIM), F32),
                        pltpu.VMEM((seq, RET_HEAD_DIM), F32),
                        pltpu.VMEM((seq, RET_HEAD_DIM), F32),
                        pltpu.VMEM((seq, RET_HEAD_DIM), BF16),
                        pltpu.VMEM((c, c), F32),
                        pltpu.VMEM((6, c, half), F32)],
        compiler_params=_cparams(("parallel", "parallel", "arbitrary")),
        name="retention",
    )(lg2, cos, sin, proj, proj, proj, proj, norm_g.reshape(1, width))


def _hgrn_kernel(lbl_ref, lvl_ref, up_ref, sg_ref, cf_ref, q_ref, z_ref, v_ref, g_ref, ng_ref, o_ref,
                 st_ref, ob_ref, *, nb, rows, sub, layer, depth):
    t = pl.program_id(2)
    d = HGRN_HEAD_DIM
    nlev = sub.bit_length() - 1

    def lower_bound(dirn):
        x = [lbl_ref[2 * l + dirn: 2 * l + dirn + 1, :] for l in range(depth)]
        mx = functools.reduce(jnp.maximum, x)
        e = [jnp.exp(xi - mx) for xi in x]
        tot = functools.reduce(lambda a, b: a + b, e)
        p = [ei / tot for ei in e]
        return functools.reduce(lambda a, b: a + b, p[:layer + 1]) - p[0]

    r_i = lax.broadcasted_iota(jnp.int32, (sub, sub), 0)
    c_i = lax.broadcasted_iota(jnp.int32, (sub, sub), 1)

    def split3(x):
        hi = x.astype(BF16)
        r1 = x - hi.astype(F32)
        mid = r1.astype(BF16)
        lo = (r1 - mid.astype(F32)).astype(BF16)
        return jnp.concatenate([hi, mid, lo], axis=1)

    def group_rows(x, size, pick):
        parts = [jnp.broadcast_to(x[base + pick:base + pick + 1, :], (size, d)) for base in range(0, sub, size)]
        return parts[0] if len(parts) == 1 else jnp.concatenate(parts, axis=0)

    def run(r0, rev, lb):
        f = lb + (1.0 - lb) * (1.0 / (1.0 + jnp.exp(-z_ref[r0:r0 + sub, :])))
        kk = 1.0 - f
        lf = jnp.log(f)
        q = q_ref[r0:r0 + sub, :] * (d ** -0.5)
        vb = v_ref[r0:r0 + sub, :].astype(BF16)

        tri = ((c_i >= r_i) if rev else (c_i <= r_i)).astype(BF16)
        parts = _dot(tri, split3(lf))
        cum = parts[:, :d] + parts[:, d:2 * d] + parts[:, 2 * d:]
        edge = cum[0:1, :] if rev else cum[sub - 1:sub, :]

        st = st_ref[...]
        o = _dot_nt((q * jnp.exp(cum)).astype(BF16), st.astype(BF16))
        kd = kk * jnp.exp(edge - cum)
        st_ref[...] = st * jnp.exp(edge) + _dot_tn(vb, kd.astype(BF16))

        lvl = lvl_ref[1 if rev else 0]
        a = jnp.where(lvl == nlev, jnp.sum(q * kk, axis=-1, keepdims=True), 0.0)
        lf_up = pltpu.roll(lf, sub - 1, 0)
        lf_dn = pltpu.roll(lf, 1, 0)
        qk = q - kk
        for lev in range(nlev):
            m = 1 << lev
            di = 1 if rev else 0
            if m == 1:
                arg = lf * cf_ref[di, 0]
            elif m == 2:
                arg = lf_up * cf_ref[di, 1] + lf * cf_ref[di, 2] + lf_dn * cf_ref[di, 3]
            else:
                bound = group_rows(cum, 2 * m, m if rev else m - 1)
                arg = ((bound - cum) if rev else (cum - bound)) * sg_ref[lev]
            up = up_ref[lev]
            y = ((q - up * qk) if rev else (kk + up * qk)) * jnp.exp2(arg)
            yb = y.astype(BF16)
            a = jnp.where(lvl == lev, _dot_nt(yb, yb), a)
        return o + _dot(a.astype(BF16), vb)

    nsb = rows // sub

    @pl.when(t == 0)
    def _():
        st_ref[...] = jnp.zeros_like(st_ref)

    @pl.when(t < nb)
    def _():
        lb = lower_bound(1)
        base = (nb - 1 - t) * rows
        for si in range(nsb - 1, -1, -1):
            ob_ref[pl.ds(pl.multiple_of(base + si * sub, sub), sub), :] = run(si * sub, True, lb)

    @pl.when(t == nb)
    def _():
        st_ref[...] = jnp.zeros_like(st_ref)

    @pl.when(t >= nb)
    def _():
        lb = lower_bound(0)
        base = (t - nb) * rows
        for si in range(nsb):
            r0 = si * sub
            o = run(r0, False, lb) + ob_ref[pl.ds(pl.multiple_of(base + r0, sub), sub), :]
            o = o * lax.rsqrt(jnp.mean(o * o, axis=-1, keepdims=True) + RMS_EPS)
            o_ref[r0:r0 + sub, :] = (_silu(g_ref[r0:r0 + sub, :]) * (o * ng_ref[...])).astype(o_ref.dtype)


def hgrn2_mixer(proj, lb_logits, norm_g, layer, batch, seq, col0):
    m = proj.shape[0]
    width = norm_g.shape[0]
    d = HGRN_HEAD_DIM
    heads = width // d
    depth = lb_logits.shape[0]
    rows = min(HGRN_ROWS, seq)
    nb = seq // rows
    cb0 = col0 // d

    def blk(t):
        return jnp.where(t < nb, nb - 1 - t, t - nb)

    def fblk(t):
        return jnp.maximum(t - nb, 0)

    def spec(j):
        return pl.BlockSpec((rows, d), lambda b, h, t: (b * nb + blk(t), cb0 + j * heads + h))

    zspec = pl.BlockSpec((rows, d),
                         lambda b, h, t: (b * nb + blk(t), cb0 + jnp.where(t < nb, 2, 1) * heads + h))
    sub = min(HGRN_SUB, rows)
    assert sub & (sub - 1) == 0 and rows % sub == 0
    nlev = sub.bit_length() - 1
    idx = np.arange(sub)
    x = idx[:, None] ^ idx[None, :]
    lev = np.where(x > 0, np.floor(np.log2(np.maximum(x, 1))), nlev).astype(np.int32)
    lvl = jnp.asarray(np.stack([np.where(idx[:, None] >= idx[None, :], lev, -1),
                                np.where(idx[:, None] <= idx[None, :], lev, -1)]), jnp.int32)
    upper = np.stack([((idx >> l) & 1).astype(np.float32) for l in range(nlev)])
    up = jnp.broadcast_to(jnp.asarray(upper)[:, :, None], (nlev, sub, d))
    sg = jnp.broadcast_to(jnp.asarray((2.0 * upper - 1.0) * np.float32(math.log2(math.e)))[:, :, None],
                          (nlev, sub, d))
    tab = pl.BlockSpec((nlev, sub, d), lambda b, h, t: (0, 0, 0))
    r4 = idx & 3
    pat = np.array([[[0, 1, 0, 1], [1, 0, 0, 0], [0, 0, 1, 1], [0, 0, 0, 1]],
                    [[1, 0, 1, 0], [1, 0, 0, 0], [1, 1, 0, 0], [0, 0, 0, 1]]], np.float32)
    cf = jnp.broadcast_to(jnp.asarray(pat[:, :, r4] * np.float32(math.log2(math.e)))[:, :, :, None],
                          (2, 4, sub, d))
    return pl.pallas_call(
        functools.partial(_hgrn_kernel, nb=nb, rows=rows, sub=sub, layer=layer, depth=depth),
        grid=(batch, heads, 2 * nb),
        in_specs=[pl.BlockSpec((2 * depth, d), lambda b, h, t: (0, h)),
                  pl.BlockSpec((2, sub, sub), lambda b, h, t: (0, 0, 0)), tab, tab,
                  pl.BlockSpec((2, 4, sub, d), lambda b, h, t: (0, 0, 0, 0)),
                  spec(0), zspec, spec(3),
                  pl.BlockSpec((rows, d), lambda b, h, t: (b * nb + fblk(t), cb0 + 4 * heads + h)),
                  pl.BlockSpec((1, d), lambda b, h, t: (0, h))],
        out_specs=pl.BlockSpec((rows, d), lambda b, h, t: (b * nb + fblk(t), h)),
        out_shape=jax.ShapeDtypeStruct((m, width), BF16),
        scratch_shapes=[pltpu.VMEM((d, d), F32),
                        pltpu.VMEM((seq, d), F32)],
        compiler_params=_cparams(("parallel", "parallel", "arbitrary")),
        name="hgrn2",
    )(lb_logits.reshape(2 * depth, width), lvl, up, sg, cf, proj, proj, proj, proj, norm_g.reshape(1, width))


def _dft_cos_sin(n):
    idx = np.arange(n)
    ang = 2.0 * np.pi * ((idx[:, None] * idx[None, :]) % n) / n
    return np.cos(ang), np.sin(ang)


def _fft_chan_kernel(z_ref, m_ref, vr_ref, vi_ref, *, groups):
    gd = FOURIER_GROUP_DIM
    mat = m_ref[...]
    for g in range(groups):
        pq = _dot3_rhs(z_ref[:, g * gd:(g + 1) * gd], mat)
        vr_ref[:, g * gd:(g + 1) * gd] = pq[:, :gd]
        vi_ref[:, g * gd:(g + 1) * gd] = pq[:, gd:]


def _fft_stage1_kernel(vr_ref, vi_ref, m_ref, twc_ref, tws_ref, ur_ref, ui_ref, *, n1, reps, nsub):
    for bi in range(nsub):
        x = jnp.concatenate([vr_ref[:, bi, :], vi_ref[:, bi, :]], axis=0)
        tt = _dot3_lhs(m_ref[...], x)
        tr, ti = tt[:n1], tt[n1:]
        cw = jnp.tile(twc_ref[bi], (1, reps))
        sw = jnp.tile(tws_ref[bi], (1, reps))
        ur_ref[:, bi, :] = tr * cw + ti * sw
        ui_ref[:, bi, :] = ti * cw - tr * sw


def _fft_stage2_kernel(ur_ref, ui_ref, m_ref, w_ref, b_ref, o_ref, y_ref, *, groups, scale, n2, nsub):
    gd = FOURIER_GROUP_DIM
    for ki in range(nsub):
        x = jnp.concatenate([ur_ref[ki * n2:(ki + 1) * n2, :], ui_ref[ki * n2:(ki + 1) * n2, :]], axis=0)
        spec = _dot3_lhs(m_ref[...], x) * scale
        for g in range(groups):
            y = _dot(spec[:, g * gd:(g + 1) * gd].astype(BF16), w_ref[g].astype(BF16))
            y_ref[:, ki, g * gd:(g + 1) * gd] = y + b_ref[:, g * gd:(g + 1) * gd]
    o_ref[...] = y_ref[...].astype(o_ref.dtype)


def fourier_mixer(proj, w, bias, batch, seq, col0):
    m = proj.shape[0]
    groups, gd, _ = w.shape
    width = groups * gd
    n1 = min(FFT_N1, seq)
    n2 = seq // n1
    br = min(512, m)

    cc, sc = _dft_cos_sin(gd)
    chan = _const3_rhs(np.concatenate([cc, -sc], axis=1))
    vr, vi = pl.pallas_call(
        functools.partial(_fft_chan_kernel, groups=groups),
        grid=(m // br,),
        in_specs=[pl.BlockSpec((br, width), lambda i: (i, col0 // width)),
                  pl.BlockSpec((3 * gd, 2 * gd), lambda i: (0, 0))],
        out_specs=[pl.BlockSpec((br, width), lambda i: (i, 0))] * 2,
        out_shape=[jax.ShapeDtypeStruct((m, width), F32)] * 2,
        compiler_params=_cparams(("parallel",)),
        name="fft_channels",
    )(proj, chan)

    c1, s1 = _dft_cos_sin(n1)
    m1 = _const3_lhs(np.block([[c1, s1], [-s1, c1]]))
    k1 = np.arange(n1)
    bb = np.arange(n2)
    tw = 2.0 * np.pi * ((bb[:, None] * k1[None, :]) % seq) / seq
    twc = jnp.broadcast_to(jnp.asarray(np.cos(tw), F32)[:, :, None], (n2, n1, LANES))
    tws = jnp.broadcast_to(jnp.asarray(np.sin(tw), F32)[:, :, None], (n2, n1, LANES))
    s1 = min(FFT_SUB1, n2)
    vspec = pl.BlockSpec((n1, s1, width), lambda bt, j: (bt, j, 0))
    tspec = pl.BlockSpec((s1, n1, LANES), lambda bt, j: (j, 0, 0))
    ur, ui = pl.pallas_call(
        functools.partial(_fft_stage1_kernel, n1=n1, reps=width // LANES, nsub=s1),
        grid=(batch, n2 // s1),
        in_specs=[vspec, vspec, pl.BlockSpec((2 * n1, 6 * n1), lambda bt, j: (0, 0)), tspec, tspec],
        out_specs=[vspec, vspec],
        out_shape=[jax.ShapeDtypeStruct((batch * n1, n2, width), F32)] * 2,
        compiler_params=_cparams(("parallel", "parallel")),
        name="fft_stage1",
    )(vr.reshape(batch * n1, n2, width), vi.reshape(batch * n1, n2, width), m1, twc, tws)

    c2, s2 = _dft_cos_sin(n2)
    m2 = _const3_lhs(np.concatenate([c2, s2], axis=1))
    s2 = min(FFT_SUB2, n1)
    uspec = pl.BlockSpec((s2 * n2, width), lambda bt, j: (bt * (n1 // s2) + j, 0))
    out = pl.pallas_call(
        functools.partial(_fft_stage2_kernel, groups=groups, scale=1.0 / math.sqrt(seq * gd), n2=n2, nsub=s2),
        grid=(batch, n1 // s2),
        in_specs=[uspec, uspec, pl.BlockSpec((n2, 6 * n2), lambda bt, j: (0, 0)),
                  pl.BlockSpec((groups, gd, gd), lambda bt, j: (0, 0, 0)),
                  pl.BlockSpec((1, width), lambda bt, j: (0, 0))],
        out_specs=pl.BlockSpec((n2, s2, width), lambda bt, j: (bt, j, 0)),
        out_shape=jax.ShapeDtypeStruct((batch * n2, n1, width), BF16),
        scratch_shapes=[pltpu.VMEM((n2, s2, width), F32)],
        compiler_params=_cparams(("parallel", "parallel")),
        name="fft_stage2",
    )(ur.reshape(m, width), ui.reshape(m, width), m2, w, bias.reshape(1, width))
    return out.reshape(m, width)


def _xattn_block_kernel(h_ref, y_ref, g_ref, wq_ref, wo_ref, k_ref, v_ref, ho_ref, f_ref, *, heads):
    hd = XATTN_HEAD_DIM
    h1 = h_ref[...] + _rms(y_ref[...], g_ref[0:1, :])
    c = _rms(h1, g_ref[1:2, :]).astype(BF16)
    q = (_dot(c, wq_ref[...]) * (hd ** -0.5)).astype(BF16)
    outs = []
    for hh in range(heads):
        cols = slice(hh * hd, (hh + 1) * hd)
        s = _dot_nt(q[:, cols], k_ref[:, cols])
        p = jnp.exp(s - jnp.max(s, axis=-1, keepdims=True))
        denom = jnp.sum(p, axis=-1, keepdims=True)
        outs.append((_dot(p.astype(BF16), v_ref[:, cols]) / denom).astype(BF16))
    xa = _dot(jnp.concatenate(outs, axis=1), wo_ref[...])
    h2 = h1 + _rms(xa, g_ref[2:3, :])
    ho_ref[...] = h2
    f_ref[...] = _rms(h2, g_ref[3:4, :]).astype(f_ref.dtype)


def xattn_block(h, y, gains, wq, wo, kx, vx, seq, mem_tokens):
    m, d = h.shape
    width = wq.shape[1]
    bm = min(XATTN_ROWS, seq)
    nb = seq // bm
    row = pl.BlockSpec((bm, d), lambda i: (i, 0))

    def const(shape):
        return pl.BlockSpec(shape, lambda i: (0, 0), pipeline_mode=pl.Buffered(1))

    kv = pl.BlockSpec((mem_tokens, width), lambda i: (i // nb, 0))
    return pl.pallas_call(
        functools.partial(_xattn_block_kernel, heads=width // XATTN_HEAD_DIM),
        grid=(m // bm,),
        in_specs=[row, row, const((4, d)), const((d, width)), const((width, d)), kv, kv],
        out_specs=[row, row],
        out_shape=[jax.ShapeDtypeStruct((m, d), F32), jax.ShapeDtypeStruct((m, d), BF16)],
        compiler_params=_cparams(("arbitrary",)),
        name="xattn_block",
    )(h, y, gains, wq, wo, kx, vx)


def kernel(x, mem, mem_norm_g, pre_mix_g, w_in, ret_norm_g, hgrn_lb_logits, hgrn_norm_g, fourier_w, fourier_b,
           w_out, post_mix_g, pre_xattn_g, xattn_wq, xattn_wk, xattn_wv, xattn_wo, post_xattn_g, pre_ffn_g,
           ffn_w_gate, ffn_w_up, ffn_w_down, post_ffn_g):
    batch, seq, d = x.shape
    depth = w_in.shape[0]
    mem_tokens = mem.shape[1]
    m = batch * seq
    ret_w = ret_norm_g.shape[1]
    hgrn_w = hgrn_norm_g.shape[1]

    h = x.reshape(m, d)
    mem_n = rmsnorm_rows(mem.reshape(batch * mem_tokens, d), mem_norm_g)
    a = rmsnorm_rows(h, pre_mix_g[0])
    for l in range(depth):
        proj = matmul([a], w_in, l, F32, *MM_TILE)
        y_ret = retention_mixer(proj, ret_norm_g[l], batch, seq, 0)
        y_hgrn = hgrn2_mixer(proj, hgrn_lb_logits, hgrn_norm_g[l], l, batch, seq, 4 * ret_w)
        y_fft = fourier_mixer(proj, fourier_w[l], fourier_b[l], batch, seq, 4 * ret_w + 5 * hgrn_w)
        mixed = matmul([y_ret, y_hgrn, y_fft], w_out, l, F32, *MM_TILE)
        kx = matmul([mem_n], xattn_wk, l, BF16, *MM_TILE_MEM)
        vx = matmul([mem_n], xattn_wv, l, BF16, *MM_TILE_MEM)
        gains = jnp.stack([post_mix_g[l], pre_xattn_g[l], post_xattn_g[l], pre_ffn_g[l]])
        h, f = xattn_block(h, mixed, gains, xattn_wq[l].astype(BF16), xattn_wo[l].astype(BF16), kx, vx,
                           seq, mem_tokens)
        hid = swiglu_matmul(f, ffn_w_gate, ffn_w_up, l, *MM_TILE_SWIGLU)
        ff = matmul_ktiled(hid, ffn_w_down, l, *MM_TILE_DOWN)
        h, a = add_norm(h, ff, post_ffn_g[l], pre_mix_g[l + 1] if l + 1 < depth else None)
    return h.reshape(batch, seq, d)
```

```python
import functools
import math

import numpy as np
import jax
import jax.numpy as jnp
from jax import lax
from jax.experimental import pallas as pl
from jax.experimental.pallas import tpu as pltpu

F32 = jnp.float32
BF16 = jnp.bfloat16

RET_HEAD_DIM = 256
HGRN_HEAD_DIM = 128
FOURIER_GROUP_DIM = 128
XATTN_HEAD_DIM = 256
ROPE_BASE = 10000.0
RMS_EPS = 1e-6
GN_EPS = 1e-5

LANES = 128
MXU_TILE = 256
V7X_VMEM_BYTES = 64 * 1024 * 1024
VMEM_LIMIT = V7X_VMEM_BYTES - 8 * 1024 * 1024

MM_TILE = (2048, 512)
MM_TILE_SWIGLU = (2048, 256)
MM_TILE_MEM = (512, 512)
MM_TILE_DOWN = (2048, 1024, 1024)
RET_CHUNK = MXU_TILE
RET_ROWS = 2048
HGRN_ROWS = 1024
SIDE_BN = 512
HGRN_SUB = 128
XATTN_ROWS = 256
FFT_N1 = 64
FFT_SUB1 = 8
FFT_SUB2 = 16


def _cparams(sem):
    return pltpu.CompilerParams(dimension_semantics=sem, vmem_limit_bytes=VMEM_LIMIT)


def _tile(n, preferred):
    for t in range(min(preferred, n) // LANES * LANES, 0, -LANES):
        if n % t == 0:
            return t
    return n


def _dot(a, b):
    return jnp.dot(a, b, preferred_element_type=F32)


def _dot_nt(a, b):
    return lax.dot_general(a, b, (((1,), (1,)), ((), ())), preferred_element_type=F32)


def _dot_tn(a, b):
    return lax.dot_general(a, b, (((0,), (0,)), ((), ())), preferred_element_type=F32)


def _split2(x):
    hi = x.astype(BF16)
    return hi, (x - hi.astype(F32)).astype(BF16)


def _const3_lhs(mat):
    hi, lo = _split2(jnp.asarray(mat, F32))
    return jnp.concatenate([hi, hi, lo], axis=1)


def _const3_rhs(mat):
    hi, lo = _split2(jnp.asarray(mat, F32))
    return jnp.concatenate([hi, hi, lo], axis=0)


def _dot3_lhs(m3, x):
    hi, lo = _split2(x)
    return _dot(m3, jnp.concatenate([hi, lo, hi], axis=0))


def _dot3_rhs(x, m3):
    hi, lo = _split2(x)
    return _dot(jnp.concatenate([hi, lo, hi], axis=1), m3)


def _rms(x, g):
    ms = jnp.mean(x * x, axis=-1, keepdims=True)
    return x * lax.rsqrt(ms + RMS_EPS) * g


def _silu(x):
    return x * (1.0 / (1.0 + jnp.exp(-x)))


def _norm_kernel(x_ref, g_ref, o_ref):
    o_ref[...] = _rms(x_ref[...], g_ref[...]).astype(o_ref.dtype)


def rmsnorm_rows(x, g, br=256):
    m, d = x.shape
    out_dtype = BF16
    br = min(br, m)
    return pl.pallas_call(
        _norm_kernel,
        grid=(m // br,),
        in_specs=[pl.BlockSpec((br, d), lambda i: (i, 0)),
                  pl.BlockSpec((1, d), lambda i: (0, 0))],
        out_specs=pl.BlockSpec((br, d), lambda i: (i, 0)),
        out_shape=jax.ShapeDtypeStruct((m, d), out_dtype),
        compiler_params=_cparams(("parallel",)),
        name="rmsnorm_rows",
    )(x, g.reshape(1, d))


def _add_norm_kernel(h_ref, y_ref, gp_ref, gn_ref, ho_ref, a_ref):
    h = h_ref[...] + _rms(y_ref[...], gp_ref[...])
    ho_ref[...] = h
    a_ref[...] = _rms(h, gn_ref[...]).astype(a_ref.dtype)


def _add_norm_last_kernel(h_ref, y_ref, gp_ref, ho_ref):
    ho_ref[...] = h_ref[...] + _rms(y_ref[...], gp_ref[...])


def add_norm(h, y, g_post, g_next, br=256):
    m, d = h.shape
    row = pl.BlockSpec((br, d), lambda i: (i, 0))
    vec = pl.BlockSpec((1, d), lambda i: (0, 0))
    if g_next is None:
        return pl.pallas_call(
            _add_norm_last_kernel, grid=(m // br,),
            in_specs=[row, row, vec], out_specs=row,
            out_shape=jax.ShapeDtypeStruct((m, d), F32),
            compiler_params=_cparams(("parallel",)), name="add_norm_last",
        )(h, y, g_post.reshape(1, d)), None
    return pl.pallas_call(
        _add_norm_kernel, grid=(m // br,),
        in_specs=[row, row, vec, vec], out_specs=[row, row],
        out_shape=[jax.ShapeDtypeStruct((m, d), F32), jax.ShapeDtypeStruct((m, d), BF16)],
        compiler_params=_cparams(("parallel",)), name="add_norm",
    )(h, y, g_post.reshape(1, d), g_next.reshape(1, d))


def _row_resident_spec(bm, kdim):
    return pl.BlockSpec((bm, kdim), lambda i, j: (i, 0), pipeline_mode=pl.Buffered(1))


def _mm_kernel(*refs, k_parts):
    a_refs, w_ref, o_ref = refs[:len(k_parts)], refs[len(k_parts)], refs[-1]
    acc, r0 = None, 0
    for a_ref, kp in zip(a_refs, k_parts):
        part = _dot(a_ref[...], w_ref[r0:r0 + kp, :].astype(BF16))
        acc = part if acc is None else acc + part
        r0 += kp
    o_ref[...] = acc.astype(o_ref.dtype)


def matmul(a_parts, w, layer, out_dtype, bm, bn, col0=0):
    m = a_parts[0].shape[0]
    k_parts = tuple(a.shape[1] for a in a_parts)
    kdim = w.shape[1]
    n = w.shape[2] - col0
    assert sum(k_parts) == kdim
    bm, bn = _tile(m, bm), _tile(math.gcd(n, col0) if col0 else n, bn)
    cb0 = col0 // bn
    return pl.pallas_call(
        functools.partial(_mm_kernel, k_parts=k_parts),
        grid=(m // bm, n // bn),
        in_specs=[_row_resident_spec(bm, kp) for kp in k_parts]
        + [pl.BlockSpec((None, kdim, bn), lambda i, j: (layer, 0, cb0 + j))],
        out_specs=pl.BlockSpec((bm, bn), lambda i, j: (i, j)),
        out_shape=jax.ShapeDtypeStruct((m, n), out_dtype),
        compiler_params=_cparams(("parallel", "arbitrary")),
        name="matmul",
    )(*a_parts, w)


def _swiglu_kernel(a_ref, wg_ref, wu_ref, o_ref):
    a = a_ref[...]
    gate = _dot(a, wg_ref[...].astype(BF16))
    up = _dot(a, wu_ref[...].astype(BF16))
    o_ref[...] = (_silu(gate) * up).astype(o_ref.dtype)


def swiglu_matmul(a, wg, wu, layer, bm, bn):
    m, kdim = a.shape
    n = wg.shape[2]
    bm, bn = _tile(m, bm), _tile(n, bn)
    wspec = pl.BlockSpec((None, kdim, bn), lambda i, j: (layer, 0, j))
    return pl.pallas_call(
        _swiglu_kernel,
        grid=(m // bm, n // bn),
        in_specs=[_row_resident_spec(bm, kdim), wspec, wspec],
        out_specs=pl.BlockSpec((bm, bn), lambda i, j: (i, j)),
        out_shape=jax.ShapeDtypeStruct((m, n), BF16),
        compiler_params=_cparams(("parallel", "arbitrary")),
        name="swiglu_matmul",
    )(a, wg, wu)


def _mm_ktiled_kernel(*refs, n_tail):
    a_ref, w_ref = refs[0], refs[1]
    t_refs, wt_ref, o_ref = refs[2:2 + n_tail], refs[2 + n_tail], refs[-1]
    k = pl.program_id(2)

    @pl.when(k == 0)
    def _():
        tail = jnp.concatenate([t[...] for t in t_refs], axis=1)
        o_ref[...] = _dot(a_ref[...], w_ref[...].astype(BF16)) + _dot(tail, wt_ref[...].astype(BF16))

    @pl.when(k > 0)
    def _():
        o_ref[...] += _dot(a_ref[...], w_ref[...].astype(BF16))


def matmul_ktiled(a, w, layer, bm, bn, bk):
    m, kdim = a.shape
    n = w.shape[2]
    bm, bn = _tile(m, bm), _tile(n, bn)
    nk = kdim // bk
    kmain = nk * bk
    ktail = kdim - kmain
    assert nk >= 1 and ktail > 0 and ktail % MXU_TILE == 0 and kmain % MXU_TILE == 0
    n_tail = ktail // MXU_TILE
    tb0 = kmain // MXU_TILE
    w_tail = w[layer, kmain:, :]
    tail_specs = [pl.BlockSpec((bm, MXU_TILE), functools.partial(lambda i, j, k, c: (i, c), c=tb0 + c))
                  for c in range(n_tail)]
    return pl.pallas_call(
        functools.partial(_mm_ktiled_kernel, n_tail=n_tail),
        grid=(m // bm, n // bn, nk),
        in_specs=[pl.BlockSpec((bm, bk), lambda i, j, k: (i, k)),
                  pl.BlockSpec((None, bk, bn), lambda i, j, k: (layer, k, j))]
        + tail_specs + [pl.BlockSpec((ktail, bn), lambda i, j, k: (0, j))],
        out_specs=pl.BlockSpec((bm, bn), lambda i, j, k: (i, j)),
        out_shape=jax.ShapeDtypeStruct((m, n), F32),
        compiler_params=_cparams(("parallel", "parallel", "arbitrary")),
        name="matmul_ktiled",
    )(a, w, *([a] * n_tail), w_tail)


def _ret_kernel(lg_ref, cos_ref, sin_ref, q_ref, k_ref, v_ref, g_ref, ng_ref, o_ref,
                sf_ref, sb_ref, ob_ref, qr_ref, kr_ref, vb_ref, dec_ref, vec_ref, *, nb, rows, c):
    t = pl.program_id(2)
    half = RET_HEAD_DIM // 2
    nch = rows // c

    @pl.when(t == 0)
    def _():
        lgf = lg_ref[0, 0:1, :]
        lgb = lg_ref[0, 1:2, :]
        i_row = lax.broadcasted_iota(jnp.int32, (c, half), 0).astype(F32)
        rel = (lax.broadcasted_iota(jnp.int32, (c, c), 0) - lax.broadcasted_iota(jnp.int32, (c, c), 1)).astype(F32)
        dec_ref[...] = jnp.exp(jnp.where(rel >= 0, rel * lgf[:, :1], -rel * lgb[:, :1]))
        vec_ref[0] = jnp.exp((i_row + 1.0) * lgf)
        vec_ref[1] = jnp.exp((c - 1.0 - i_row) * lgf)
        vec_ref[2] = jnp.exp((c - i_row) * lgb)
        vec_ref[3] = jnp.exp(i_row * lgb)
        vec_ref[4] = jnp.exp(c * lgf) + jnp.zeros((c, half), F32)
        vec_ref[5] = jnp.exp(c * lgb) + jnp.zeros((c, half), F32)
        sf_ref[...] = jnp.zeros_like(sf_ref)
        sb_ref[...] = jnp.zeros_like(sb_ref)

    def lanes2(d):
        return jnp.concatenate([d, d], axis=-1)

    @pl.when(t < nb)
    def _():
        base = (nb - 1 - t) * rows
        for ci in range(nch - 1, -1, -1):
            r0 = ci * c
            g0 = pl.multiple_of(base + r0, c)
            cos = cos_ref[r0:r0 + c, :]
            sin = sin_ref[r0:r0 + c, :]

            def rot(x):
                x1, x2 = x[:, :half], x[:, half:]
                return jnp.concatenate([x1 * cos - x2 * sin, x1 * sin + x2 * cos], axis=-1)

            q = rot(q_ref[r0:r0 + c, :]) * (RET_HEAD_DIM ** -0.5)
            k = rot(k_ref[r0:r0 + c, :])
            qr_ref[pl.ds(g0, c), :] = q
            kr_ref[pl.ds(g0, c), :] = k
            vb = v_ref[r0:r0 + c, :].astype(BF16)
            vb_ref[pl.ds(g0, c), :] = vb
            sb = sb_ref[...]
            ob_ref[pl.ds(g0, c), :] = _dot((q * lanes2(vec_ref[2])).astype(BF16), sb.astype(BF16))
            kd = k * lanes2(vec_ref[3])
            sb_ref[...] = vec_ref[5][0:1, 0:1] * sb + _dot_tn(kd.astype(BF16), vb)

    @pl.when(t >= nb)
    def _():
        base = (t - nb) * rows
        for ci in range(nch):
            r0 = ci * c
            g0 = pl.multiple_of(base + r0, c)
            q = qr_ref[pl.ds(g0, c), :]
            k = kr_ref[pl.ds(g0, c), :]
            vb = vb_ref[pl.ds(g0, c), :]
            sf = sf_ref[...]
            scores = _dot_nt(q.astype(BF16), k.astype(BF16)) * dec_ref[...]
            o = (_dot(scores.astype(BF16), vb) + _dot((q * lanes2(vec_ref[0])).astype(BF16), sf.astype(BF16))
                 + ob_ref[pl.ds(g0, c), :])
            kd = k * lanes2(vec_ref[1])
            sf_ref[...] = vec_ref[4][0:1, 0:1] * sf + _dot_tn(kd.astype(BF16), vb)
            mu = jnp.mean(o, axis=-1, keepdims=True)
            oc = o - mu
            var = jnp.mean(oc * oc, axis=-1, keepdims=True)
            y = oc * lax.rsqrt(var + GN_EPS) * ng_ref[...]
            o_ref[r0:r0 + c, :] = (_silu(g_ref[r0:r0 + c, :]) * y).astype(o_ref.dtype)


def retention_mixer(proj, norm_g, batch, seq, col0):
    m = proj.shape[0]
    width = norm_g.shape[0]
    heads = width // RET_HEAD_DIM
    c = min(RET_CHUNK, seq)
    rows = min(RET_ROWS, seq)
    nb = seq // rows
    half = RET_HEAD_DIM // 2
    cb0 = col0 // RET_HEAD_DIM

    hidx = jnp.arange(heads, dtype=F32)
    gamma = 1.0 - jnp.power(2.0, -5.0 - hidx)
    lg = jnp.log(gamma)
    lg2 = jnp.broadcast_to(jnp.stack([lg, lg[::-1]], axis=1)[:, :, None], (heads, 2, LANES))
    inv = jnp.power(ROPE_BASE, -jnp.arange(half, dtype=F32) / half)
    ang = jnp.arange(seq, dtype=F32)[:, None] * inv[None, :]
    cos, sin = jnp.cos(ang), jnp.sin(ang)

    def bblk(t):
        return jnp.maximum(nb - 1 - t, 0)

    def fblk(t):
        return jnp.maximum(t - nb, 0)

    def spec(j, which):
        return pl.BlockSpec((rows, RET_HEAD_DIM), lambda b, h, t: (b * nb + which(t), cb0 + j * heads + h))

    return pl.pallas_call(
        functools.partial(_ret_kernel, nb=nb, rows=rows, c=c),
        grid=(batch, heads, 2 * nb),
        in_specs=[pl.BlockSpec((1, 2, LANES), lambda b, h, t: (h, 0, 0)),
                  pl.BlockSpec((rows, half), lambda b, h, t: (bblk(t), 0)),
                  pl.BlockSpec((rows, half), lambda b, h, t: (bblk(t), 0)),
                  spec(0, bblk), spec(1, bblk), spec(2, bblk), spec(3, fblk),
                  pl.BlockSpec((1, RET_HEAD_DIM), lambda b, h, t: (0, h))],
        out_specs=pl.BlockSpec((rows, RET_HEAD_DIM), lambda b, h, t: (b * nb + fblk(t), h)),
        out_shape=jax.ShapeDtypeStruct((m, width), BF16),
        scratch_shapes=[pltpu.VMEM((RET_HEAD_DIM, RET_HEAD_DIM), F32),
                        pltpu.VMEM((RET_HEAD_DIM, RET_HEAD_DIM), F32),
                        pltpu.VMEM((seq, RET_HEAD_DIM), F32),
                        pltpu.VMEM((seq, RET_HEAD_DIM), F32),
                        pltpu.VMEM((seq, RET_HEAD_DIM), F32),
                        pltpu.VMEM((seq, RET_HEAD_DIM), BF16),
                        pltpu.VMEM((c, c), F32),
                        pltpu.VMEM((6, c, half), F32)],
        compiler_params=_cparams(("parallel", "parallel", "arbitrary")),
        name="retention",
    )(lg2, cos, sin, proj, proj, proj, proj, norm_g.reshape(1, width))


def _hgrn_kernel(lbl_ref, lvl_ref, up_ref, sg_ref, cf_ref, q_ref, z_ref, v_ref, g_ref, ng_ref, a_ref, w_ref,
                 o_ref, po_ref, st_ref, ob_ref, wb_ref, *, nb, rows, sub, layer, depth):
    t = pl.program_id(2)
    d = HGRN_HEAD_DIM
    nlev = sub.bit_length() - 1
    nsb = rows // sub
    srows = a_ref.shape[0] // nsb

    def lower_bound(dirn):
        x = [lbl_ref[2 * l + dirn: 2 * l + dirn + 1, :] for l in range(depth)]
        mx = functools.reduce(jnp.maximum, x)
        e = [jnp.exp(xi - mx) for xi in x]
        tot = functools.reduce(lambda a, b: a + b, e)
        p = [ei / tot for ei in e]
        return functools.reduce(lambda a, b: a + b, p[:layer + 1]) - p[0]

    r_i = lax.broadcasted_iota(jnp.int32, (sub, sub), 0)
    c_i = lax.broadcasted_iota(jnp.int32, (sub, sub), 1)

    def split3(x):
        hi = x.astype(BF16)
        r1 = x - hi.astype(F32)
        mid = r1.astype(BF16)
        lo = (r1 - mid.astype(F32)).astype(BF16)
        return jnp.concatenate([hi, mid, lo], axis=1)

    def group_rows(x, size, pick):
        parts = [jnp.broadcast_to(x[base + pick:base + pick + 1, :], (size, d)) for base in range(0, sub, size)]
        return parts[0] if len(parts) == 1 else jnp.concatenate(parts, axis=0)

    def head(r0, rev, lb):
        f = lb + (1.0 - lb) * (1.0 / (1.0 + jnp.exp(-z_ref[r0:r0 + sub, :])))
        kk = 1.0 - f
        lf = jnp.log(f)
        q = q_ref[r0:r0 + sub, :] * (d ** -0.5)
        vb = v_ref[r0:r0 + sub, :].astype(BF16)
        tri = ((c_i >= r_i) if rev else (c_i <= r_i)).astype(BF16)
        parts = _dot(tri, split3(lf))
        return kk, lf, q, vb, parts

    def rest(r0, rev, hd):
        kk, lf, q, vb, parts = hd
        s0 = (r0 // sub) * srows
        po_ref[s0:s0 + srows, :] = _dot(a_ref[s0:s0 + srows, :], wb_ref[...])
        cum = parts[:, :d] + parts[:, d:2 * d] + parts[:, 2 * d:]
        edge = cum[0:1, :] if rev else cum[sub - 1:sub, :]

        st = st_ref[...]
        o = _dot_nt((q * jnp.exp(cum)).astype(BF16), st.astype(BF16))
        kd = kk * jnp.exp(edge - cum)
        st_ref[...] = st * jnp.exp(edge) + _dot_tn(vb, kd.astype(BF16))

        lvl = lvl_ref[1 if rev else 0]
        a = jnp.where(lvl == nlev, jnp.sum(q * kk, axis=-1, keepdims=True), 0.0)
        lf_up = pltpu.roll(lf, sub - 1, 0)
        lf_dn = pltpu.roll(lf, 1, 0)
        qk = q - kk
        for lev in range(nlev):
            m = 1 << lev
            di = 1 if rev else 0
            if m == 1:
                arg = lf * cf_ref[di, 0]
            elif m == 2:
                arg = lf_up * cf_ref[di, 1] + lf * cf_ref[di, 2] + lf_dn * cf_ref[di, 3]
            else:
                bound = group_rows(cum, 2 * m, m if rev else m - 1)
                arg = ((bound - cum) if rev else (cum - bound)) * sg_ref[lev]
            up = up_ref[lev]
            y = ((q - up * qk) if rev else (kk + up * qk)) * jnp.exp2(arg)
            yb = y.astype(BF16)
            a = jnp.where(lvl == lev, _dot_nt(yb, yb), a)
        return o + _dot(a.astype(BF16), vb)

    @pl.when(t == 0)
    def _():
        st_ref[...] = jnp.zeros_like(st_ref)

    @pl.when(t < nb)
    def _():
        wb_ref[...] = w_ref[...].astype(BF16)
        lb = lower_bound(1)
        base = (nb - 1 - t) * rows
        order = list(range(nsb - 1, -1, -1))
        hd = head(order[0] * sub, True, lb)
        for n, si in enumerate(order):
            nxt = head(order[n + 1] * sub, True, lb) if n + 1 < nsb else None
            ob_ref[pl.ds(pl.multiple_of(base + si * sub, sub), sub), :] = rest(si * sub, True, hd)
            hd = nxt

    @pl.when(t == nb)
    def _():
        st_ref[...] = jnp.zeros_like(st_ref)

    @pl.when(t >= nb)
    def _():
        wb_ref[...] = w_ref[...].astype(BF16)
        lb = lower_bound(0)
        base = (t - nb) * rows
        hd = head(0, False, lb)
        for si in range(nsb):
            r0 = si * sub
            nxt = head(r0 + sub, False, lb) if si + 1 < nsb else None
            o = rest(r0, False, hd) + ob_ref[pl.ds(pl.multiple_of(base + r0, sub), sub), :]
            hd = nxt
            o = o * lax.rsqrt(jnp.mean(o * o, axis=-1, keepdims=True) + RMS_EPS)
            o_ref[r0:r0 + sub, :] = (_silu(g_ref[r0:r0 + sub, :]) * (o * ng_ref[...])).astype(o_ref.dtype)


def hgrn2_mixer(proj, lb_logits, norm_g, layer, batch, seq, col0, side_a, side_w, side_cols):
    m = proj.shape[0]
    width = norm_g.shape[0]
    d = HGRN_HEAD_DIM
    heads = width // d
    depth = lb_logits.shape[0]
    rows = min(HGRN_ROWS, seq)
    nb = seq // rows
    cb0 = col0 // d

    def blk(t):
        return jnp.where(t < nb, nb - 1 - t, t - nb)

    def fblk(t):
        return jnp.maximum(t - nb, 0)

    def spec(j):
        return pl.BlockSpec((rows, d), lambda b, h, t: (b * nb + blk(t), cb0 + j * heads + h))

    zspec = pl.BlockSpec((rows, d),
                         lambda b, h, t: (b * nb + blk(t), cb0 + jnp.where(t < nb, 2, 1) * heads + h))
    sub = min(HGRN_SUB, rows)
    assert sub & (sub - 1) == 0 and rows % sub == 0
    nlev = sub.bit_length() - 1
    idx = np.arange(sub)
    x = idx[:, None] ^ idx[None, :]
    lev = np.where(x > 0, np.floor(np.log2(np.maximum(x, 1))), nlev).astype(np.int32)
    lvl = jnp.asarray(np.stack([np.where(idx[:, None] >= idx[None, :], lev, -1),
                                np.where(idx[:, None] <= idx[None, :], lev, -1)]), jnp.int32)
    upper = np.stack([((idx >> l) & 1).astype(np.float32) for l in range(nlev)])
    up = jnp.broadcast_to(jnp.asarray(upper)[:, :, None], (nlev, sub, d))
    sg = jnp.broadcast_to(jnp.asarray((2.0 * upper - 1.0) * np.float32(math.log2(math.e)))[:, :, None],
                          (nlev, sub, d))
    tab = pl.BlockSpec((nlev, sub, d), lambda b, h, t: (0, 0, 0))
    r4 = idx & 3
    pat = np.array([[[0, 1, 0, 1], [1, 0, 0, 0], [0, 0, 1, 1], [0, 0, 0, 1]],
                    [[1, 0, 1, 0], [1, 0, 0, 0], [1, 1, 0, 0], [0, 0, 0, 1]]], np.float32)
    cf = jnp.broadcast_to(jnp.asarray(pat[:, :, r4] * np.float32(math.log2(math.e)))[:, :, :, None],
                          (2, 4, sub, d))
    steps = batch * heads * 2 * nb
    kdim = side_a.shape[1]
    sbn = _tile(side_cols, SIDE_BN)
    col_tiles = side_cols // sbn
    assert steps % col_tiles == 0 and m % (steps // col_tiles) == 0
    sbm = m // (steps // col_tiles)

    def step(b, h, t):
        return (b * heads + h) * (2 * nb) + t

    return pl.pallas_call(
        functools.partial(_hgrn_kernel, nb=nb, rows=rows, sub=sub, layer=layer, depth=depth),
        grid=(batch, heads, 2 * nb),
        in_specs=[pl.BlockSpec((2 * depth, d), lambda b, h, t: (0, h)),
                  pl.BlockSpec((2, sub, sub), lambda b, h, t: (0, 0, 0)), tab, tab,
                  pl.BlockSpec((2, 4, sub, d), lambda b, h, t: (0, 0, 0, 0)),
                  spec(0), zspec, spec(3),
                  pl.BlockSpec((rows, d), lambda b, h, t: (b * nb + fblk(t), cb0 + 4 * heads + h)),
                  pl.BlockSpec((1, d), lambda b, h, t: (0, h)),
                  pl.BlockSpec((sbm, kdim), lambda b, h, t: (step(b, h, t) // col_tiles, 0)),
                  pl.BlockSpec((None, kdim, sbn), lambda b, h, t: (layer, 0, step(b, h, t) % col_tiles))],
        out_specs=[pl.BlockSpec((rows, d), lambda b, h, t: (b * nb + fblk(t), h)),
                   pl.BlockSpec((sbm, sbn), lambda b, h, t: (step(b, h, t) // col_tiles, step(b, h, t) % col_tiles))],
        out_shape=[jax.ShapeDtypeStruct((m, width), BF16), jax.ShapeDtypeStruct((m, side_cols), F32)],
        scratch_shapes=[pltpu.VMEM((d, d), F32),
                        pltpu.VMEM((seq, d), F32),
                        pltpu.VMEM((kdim, sbn), BF16)],
        compiler_params=_cparams(("arbitrary", "arbitrary", "arbitrary")),
        name="hgrn2",
    )(lb_logits.reshape(2 * depth, width), lvl, up, sg, cf, proj, proj, proj, proj, norm_g.reshape(1, width),
      side_a, side_w)


def _dft_cos_sin(n):
    idx = np.arange(n)
    ang = 2.0 * np.pi * ((idx[:, None] * idx[None, :]) % n) / n
    return np.cos(ang), np.sin(ang)


def _fft_chan_kernel(z_ref, m_ref, vr_ref, vi_ref, *, groups):
    gd = FOURIER_GROUP_DIM
    mat = m_ref[...]
    for g in range(groups):
        pq = _dot3_rhs(z_ref[:, g * gd:(g + 1) * gd], mat)
        vr_ref[:, g * gd:(g + 1) * gd] = pq[:, :gd]
        vi_ref[:, g * gd:(g + 1) * gd] = pq[:, gd:]


def _fft_stage1_kernel(vr_ref, vi_ref, m_ref, twc_ref, tws_ref, ur_ref, ui_ref, *, n1, reps, nsub):
    for bi in range(nsub):
        x = jnp.concatenate([vr_ref[:, bi, :], vi_ref[:, bi, :]], axis=0)
        tt = _dot3_lhs(m_ref[...], x)
        tr, ti = tt[:n1], tt[n1:]
        cw = jnp.tile(twc_ref[bi], (1, reps))
        sw = jnp.tile(tws_ref[bi], (1, reps))
        ur_ref[:, bi, :] = tr * cw + ti * sw
        ui_ref[:, bi, :] = ti * cw - tr * sw


def _fft_stage2_kernel(ur_ref, ui_ref, m_ref, w_ref, b_ref, o_ref, y_ref, *, groups, scale, n2, nsub):
    gd = FOURIER_GROUP_DIM
    for ki in range(nsub):
        x = jnp.concatenate([ur_ref[ki * n2:(ki + 1) * n2, :], ui_ref[ki * n2:(ki + 1) * n2, :]], axis=0)
        spec = _dot3_lhs(m_ref[...], x) * scale
        for g in range(groups):
            y = _dot(spec[:, g * gd:(g + 1) * gd].astype(BF16), w_ref[g].astype(BF16))
            y_ref[:, ki, g * gd:(g + 1) * gd] = y + b_ref[:, g * gd:(g + 1) * gd]
    o_ref[...] = y_ref[...].astype(o_ref.dtype)


def fourier_mixer(proj, w, bias, batch, seq, col0):
    m = proj.shape[0]
    groups, gd, _ = w.shape
    width = groups * gd
    n1 = min(FFT_N1, seq)
    n2 = seq // n1
    br = min(512, m)

    cc, sc = _dft_cos_sin(gd)
    chan = _const3_rhs(np.concatenate([cc, -sc], axis=1))
    vr, vi = pl.pallas_call(
        functools.partial(_fft_chan_kernel, groups=groups),
        grid=(m // br,),
        in_specs=[pl.BlockSpec((br, width), lambda i: (i, col0 // width)),
                  pl.BlockSpec((3 * gd, 2 * gd), lambda i: (0, 0))],
        out_specs=[pl.BlockSpec((br, width), lambda i: (i, 0))] * 2,
        out_shape=[jax.ShapeDtypeStruct((m, width), F32)] * 2,
        compiler_params=_cparams(("parallel",)),
        name="fft_channels",
    )(proj, chan)

    c1, s1 = _dft_cos_sin(n1)
    m1 = _const3_lhs(np.block([[c1, s1], [-s1, c1]]))
    k1 = np.arange(n1)
    bb = np.arange(n2)
    tw = 2.0 * np.pi * ((bb[:, None] * k1[None, :]) % seq) / seq
    twc = jnp.broadcast_to(jnp.asarray(np.cos(tw), F32)[:, :, None], (n2, n1, LANES))
    tws = jnp.broadcast_to(jnp.asarray(np.sin(tw), F32)[:, :, None], (n2, n1, LANES))
    s1 = min(FFT_SUB1, n2)
    vspec = pl.BlockSpec((n1, s1, width), lambda bt, j: (bt, j, 0))
    tspec = pl.BlockSpec((s1, n1, LANES), lambda bt, j: (j, 0, 0))
    ur, ui = pl.pallas_call(
        functools.partial(_fft_stage1_kernel, n1=n1, reps=width // LANES, nsub=s1),
        grid=(batch, n2 // s1),
        in_specs=[vspec, vspec, pl.BlockSpec((2 * n1, 6 * n1), lambda bt, j: (0, 0)), tspec, tspec],
        out_specs=[vspec, vspec],
        out_shape=[jax.ShapeDtypeStruct((batch * n1, n2, width), F32)] * 2,
        compiler_params=_cparams(("parallel", "parallel")),
        name="fft_stage1",
    )(vr.reshape(batch * n1, n2, width), vi.reshape(batch * n1, n2, width), m1, twc, tws)

    c2, s2 = _dft_cos_sin(n2)
    m2 = _const3_lhs(np.concatenate([c2, s2], axis=1))
    s2 = min(FFT_SUB2, n1)
    uspec = pl.BlockSpec((s2 * n2, width), lambda bt, j: (bt * (n1 // s2) + j, 0))
    out = pl.pallas_call(
        functools.partial(_fft_stage2_kernel, groups=groups, scale=1.0 / math.sqrt(seq * gd), n2=n2, nsub=s2),
        grid=(batch, n1 // s2),
        in_specs=[uspec, uspec, pl.BlockSpec((n2, 6 * n2), lambda bt, j: (0, 0)),
                  pl.BlockSpec((groups, gd, gd), lambda bt, j: (0, 0, 0)),
                  pl.BlockSpec((1, width), lambda bt, j: (0, 0))],
        out_specs=pl.BlockSpec((n2, s2, width), lambda bt, j: (bt, j, 0)),
        out_shape=jax.ShapeDtypeStruct((batch * n2, n1, width), BF16),
        scratch_shapes=[pltpu.VMEM((n2, s2, width), F32)],
        compiler_params=_cparams(("parallel", "parallel")),
        name="fft_stage2",
    )(ur.reshape(m, width), ui.reshape(m, width), m2, w, bias.reshape(1, width))
    return out.reshape(m, width)


def _xattn_block_kernel(h_ref, y_ref, g_ref, wq_ref, wo_ref, k_ref, v_ref, ho_ref, f_ref, *, heads):
    hd = XATTN_HEAD_DIM
    h1 = h_ref[...] + _rms(y_ref[...], g_ref[0:1, :])
    c = _rms(h1, g_ref[1:2, :]).astype(BF16)
    q = (_dot(c, wq_ref[...]) * (hd ** -0.5)).astype(BF16)
    outs = []
    for hh in range(heads):
        cols = slice(hh * hd, (hh + 1) * hd)
        s = _dot_nt(q[:, cols], k_ref[:, cols])
        p = jnp.exp(s - jnp.max(s, axis=-1, keepdims=True))
        denom = jnp.sum(p, axis=-1, keepdims=True)
        outs.append((_dot(p.astype(BF16), v_ref[:, cols]) / denom).astype(BF16))
    xa = _dot(jnp.concatenate(outs, axis=1), wo_ref[...])
    h2 = h1 + _rms(xa, g_ref[2:3, :])
    ho_ref[...] = h2
    f_ref[...] = _rms(h2, g_ref[3:4, :]).astype(f_ref.dtype)


def xattn_block(h, y, gains, wq, wo, kx, vx, seq, mem_tokens):
    m, d = h.shape
    width = wq.shape[1]
    bm = min(XATTN_ROWS, seq)
    nb = seq // bm
    row = pl.BlockSpec((bm, d), lambda i: (i, 0))

    def const(shape):
        return pl.BlockSpec(shape, lambda i: (0, 0), pipeline_mode=pl.Buffered(1))

    kv = pl.BlockSpec((mem_tokens, width), lambda i: (i // nb, 0))
    return pl.pallas_call(
        functools.partial(_xattn_block_kernel, heads=width // XATTN_HEAD_DIM),
        grid=(m // bm,),
        in_specs=[row, row, const((4, d)), const((d, width)), const((width, d)), kv, kv],
        out_specs=[row, row],
        out_shape=[jax.ShapeDtypeStruct((m, d), F32), jax.ShapeDtypeStruct((m, d), BF16)],
        compiler_params=_cparams(("arbitrary",)),
        name="xattn_block",
    )(h, y, gains, wq, wo, kx, vx)


def kernel(x, mem, mem_norm_g, pre_mix_g, w_in, ret_norm_g, hgrn_lb_logits, hgrn_norm_g, fourier_w, fourier_b,
           w_out, post_mix_g, pre_xattn_g, xattn_wq, xattn_wk, xattn_wv, xattn_wo, post_xattn_g, pre_ffn_g,
           ffn_w_gate, ffn_w_up, ffn_w_down, post_ffn_g):
    batch, seq, d = x.shape
    depth = w_in.shape[0]
    mem_tokens = mem.shape[1]
    m = batch * seq
    ret_w = ret_norm_g.shape[1]
    hgrn_w = hgrn_norm_g.shape[1]

    h = x.reshape(m, d)
    mem_n = rmsnorm_rows(mem.reshape(batch * mem_tokens, d), mem_norm_g)
    a = rmsnorm_rows(h, pre_mix_g[0])
    for l in range(depth):
        proj_hf = matmul([a], w_in, l, F32, *MM_TILE, col0=4 * ret_w)
        y_hgrn, proj_r = hgrn2_mixer(proj_hf, hgrn_lb_logits, hgrn_norm_g[l], l, batch, seq, 0, a, w_in, 4 * ret_w)
        y_ret = retention_mixer(proj_r, ret_norm_g[l], batch, seq, 0)
        y_fft = fourier_mixer(proj_hf, fourier_w[l], fourier_b[l], batch, seq, 5 * hgrn_w)
        mixed = matmul([y_ret, y_hgrn, y_fft], w_out, l, F32, *MM_TILE)
        kx = matmul([mem_n], xattn_wk, l, BF16, *MM_TILE_MEM)
        vx = matmul([mem_n], xattn_wv, l, BF16, *MM_TILE_MEM)
        gains = jnp.stack([post_mix_g[l], pre_xattn_g[l], post_xattn_g[l], pre_ffn_g[l]])
        h, f = xattn_block(h, mixed, gains, xattn_wq[l].astype(BF16), xattn_wo[l].astype(BF16), kx, vx,
                           seq, mem_tokens)
        hid = swiglu_matmul(f, ffn_w_gate, ffn_w_up, l, *MM_TILE_SWIGLU)
        ff = matmul_ktiled(hid, ffn_w_down, l, *MM_TILE_DOWN)
        h, a = add_norm(h, ff, post_ffn_g[l], pre_mix_g[l + 1] if l + 1 < depth else None)
    return h.reshape(batch, seq, d)
```

```python
import functools
import math

import numpy as np
import jax
import jax.numpy as jnp
from jax import lax
from jax.experimental import pallas as pl
from jax.experimental.pallas import tpu as pltpu

F32 = jnp.float32
BF16 = jnp.bfloat16

RET_HEAD_DIM = 256
HGRN_HEAD_DIM = 128
FOURIER_GROUP_DIM = 128
XATTN_HEAD_DIM = 256
ROPE_BASE = 10000.0
RMS_EPS = 1e-6
GN_EPS = 1e-5

LANES = 128
MXU_TILE = 256
V7X_VMEM_BYTES = 64 * 1024 * 1024
VMEM_LIMIT = V7X_VMEM_BYTES - 8 * 1024 * 1024

MM_TILE = (2048, 512)
MM_TILE_SWIGLU = (2048, 256)
MM_TILE_MEM = (512, 512)
MM_TILE_DOWN = (2048, 1024, 1024)
RET_CHUNK = MXU_TILE
RET_ROWS = 2048
HGRN_ROWS = 1024
SIDE_BN = 512
HGRN_SUB = 128
XATTN_ROWS = 256
FFT_N1 = 64
FFT_SUB1 = 8
FFT_SUB2 = 16


def _cparams(sem):
    return pltpu.CompilerParams(dimension_semantics=sem, vmem_limit_bytes=VMEM_LIMIT)


def _tile(n, preferred):
    for t in range(min(preferred, n) // LANES * LANES, 0, -LANES):
        if n % t == 0:
            return t
    return n


def _dot(a, b):
    return jnp.dot(a, b, preferred_element_type=F32)


def _dot_nt(a, b):
    return lax.dot_general(a, b, (((1,), (1,)), ((), ())), preferred_element_type=F32)


def _dot_tn(a, b):
    return lax.dot_general(a, b, (((0,), (0,)), ((), ())), preferred_element_type=F32)


def _split2(x):
    hi = x.astype(BF16)
    return hi, (x - hi.astype(F32)).astype(BF16)


def _const3_lhs(mat):
    hi, lo = _split2(jnp.asarray(mat, F32))
    return jnp.concatenate([hi, hi, lo], axis=1)


def _const3_rhs(mat):
    hi, lo = _split2(jnp.asarray(mat, F32))
    return jnp.concatenate([hi, hi, lo], axis=0)


def _dot3_lhs(m3, x):
    hi, lo = _split2(x)
    return _dot(m3, jnp.concatenate([hi, lo, hi], axis=0))


def _dot3_rhs(x, m3):
    hi, lo = _split2(x)
    return _dot(jnp.concatenate([hi, lo, hi], axis=1), m3)


def _rms(x, g):
    ms = jnp.mean(x * x, axis=-1, keepdims=True)
    return x * lax.rsqrt(ms + RMS_EPS) * g


def _silu(x):
    return x * (1.0 / (1.0 + jnp.exp(-x)))


def _norm_kernel(x_ref, g_ref, o_ref):
    o_ref[...] = _rms(x_ref[...], g_ref[...]).astype(o_ref.dtype)


def rmsnorm_rows(x, g, br=256):
    m, d = x.shape
    out_dtype = BF16
    br = min(br, m)
    return pl.pallas_call(
        _norm_kernel,
        grid=(m // br,),
        in_specs=[pl.BlockSpec((br, d), lambda i: (i, 0)),
                  pl.BlockSpec((1, d), lambda i: (0, 0))],
        out_specs=pl.BlockSpec((br, d), lambda i: (i, 0)),
        out_shape=jax.ShapeDtypeStruct((m, d), out_dtype),
        compiler_params=_cparams(("parallel",)),
        name="rmsnorm_rows",
    )(x, g.reshape(1, d))


def _add_norm_kernel(h_ref, y_ref, gp_ref, gn_ref, ho_ref, a_ref):
    h = h_ref[...] + _rms(y_ref[...], gp_ref[...])
    ho_ref[...] = h
    a_ref[...] = _rms(h, gn_ref[...]).astype(a_ref.dtype)


def _add_norm_last_kernel(h_ref, y_ref, gp_ref, ho_ref):
    ho_ref[...] = h_ref[...] + _rms(y_ref[...], gp_ref[...])


def add_norm(h, y, g_post, g_next, br=256):
    m, d = h.shape
    row = pl.BlockSpec((br, d), lambda i: (i, 0))
    vec = pl.BlockSpec((1, d), lambda i: (0, 0))
    if g_next is None:
        return pl.pallas_call(
            _add_norm_last_kernel, grid=(m // br,),
            in_specs=[row, row, vec], out_specs=row,
            out_shape=jax.ShapeDtypeStruct((m, d), F32),
            compiler_params=_cparams(("parallel",)), name="add_norm_last",
        )(h, y, g_post.reshape(1, d)), None
    return pl.pallas_call(
        _add_norm_kernel, grid=(m // br,),
        in_specs=[row, row, vec, vec], out_specs=[row, row],
        out_shape=[jax.ShapeDtypeStruct((m, d), F32), jax.ShapeDtypeStruct((m, d), BF16)],
        compiler_params=_cparams(("parallel",)), name="add_norm",
    )(h, y, g_post.reshape(1, d), g_next.reshape(1, d))


def _row_resident_spec(bm, kdim):
    return pl.BlockSpec((bm, kdim), lambda i, j: (i, 0), pipeline_mode=pl.Buffered(1))


def _mm_kernel(*refs, k_parts):
    a_refs, w_ref, o_ref = refs[:len(k_parts)], refs[len(k_parts)], refs[-1]
    acc, r0 = None, 0
    for a_ref, kp in zip(a_refs, k_parts):
        part = _dot(a_ref[...], w_ref[r0:r0 + kp, :].astype(BF16))
        acc = part if acc is None else acc + part
        r0 += kp
    o_ref[...] = acc.astype(o_ref.dtype)


def matmul(a_parts, w, layer, out_dtype, bm, bn, col0=0):
    m = a_parts[0].shape[0]
    k_parts = tuple(a.shape[1] for a in a_parts)
    kdim = w.shape[1]
    n = w.shape[2] - col0
    assert sum(k_parts) == kdim
    bm, bn = _tile(m, bm), _tile(math.gcd(n, col0) if col0 else n, bn)
    cb0 = col0 // bn
    return pl.pallas_call(
        functools.partial(_mm_kernel, k_parts=k_parts),
        grid=(m // bm, n // bn),
        in_specs=[_row_resident_spec(bm, kp) for kp in k_parts]
        + [pl.BlockSpec((None, kdim, bn), lambda i, j: (layer, 0, cb0 + j))],
        out_specs=pl.BlockSpec((bm, bn), lambda i, j: (i, j)),
        out_shape=jax.ShapeDtypeStruct((m, n), out_dtype),
        compiler_params=_cparams(("parallel", "arbitrary")),
        name="matmul",
    )(*a_parts, w)


def _swiglu_kernel(a_ref, wg_ref, wu_ref, o_ref):
    a = a_ref[...]
    gate = _dot(a, wg_ref[...].astype(BF16))
    up = _dot(a, wu_ref[...].astype(BF16))
    o_ref[...] = (_silu(gate) * up).astype(o_ref.dtype)


def swiglu_matmul(a, wg, wu, layer, bm, bn):
    m, kdim = a.shape
    n = wg.shape[2]
    bm, bn = _tile(m, bm), _tile(n, bn)
    wspec = pl.BlockSpec((None, kdim, bn), lambda i, j: (layer, 0, j))
    return pl.pallas_call(
        _swiglu_kernel,
        grid=(m // bm, n // bn),
        in_specs=[_row_resident_spec(bm, kdim), wspec, wspec],
        out_specs=pl.BlockSpec((bm, bn), lambda i, j: (i, j)),
        out_shape=jax.ShapeDtypeStruct((m, n), BF16),
        compiler_params=_cparams(("parallel", "arbitrary")),
        name="swiglu_matmul",
    )(a, wg, wu)


def _mm_ktiled_kernel(*refs, n_tail):
    a_ref, w_ref = refs[0], refs[1]
    t_refs, wt_ref, o_ref = refs[2:2 + n_tail], refs[2 + n_tail], refs[-1]
    k = pl.program_id(2)

    @pl.when(k == 0)
    def _():
        tail = jnp.concatenate([t[...] for t in t_refs], axis=1)
        o_ref[...] = _dot(a_ref[...], w_ref[...].astype(BF16)) + _dot(tail, wt_ref[...].astype(BF16))

    @pl.when(k > 0)
    def _():
        o_ref[...] += _dot(a_ref[...], w_ref[...].astype(BF16))


def matmul_ktiled(a, w, layer, bm, bn, bk):
    m, kdim = a.shape
    n = w.shape[2]
    bm, bn = _tile(m, bm), _tile(n, bn)
    nk = kdim // bk
    kmain = nk * bk
    ktail = kdim - kmain
    assert nk >= 1 and ktail > 0 and ktail % MXU_TILE == 0 and kmain % MXU_TILE == 0
    n_tail = ktail // MXU_TILE
    tb0 = kmain // MXU_TILE
    w_tail = w[layer, kmain:, :]
    tail_specs = [pl.BlockSpec((bm, MXU_TILE), functools.partial(lambda i, j, k, c: (i, c), c=tb0 + c))
                  for c in range(n_tail)]
    return pl.pallas_call(
        functools.partial(_mm_ktiled_kernel, n_tail=n_tail),
        grid=(m // bm, n // bn, nk),
        in_specs=[pl.BlockSpec((bm, bk), lambda i, j, k: (i, k)),
                  pl.BlockSpec((None, bk, bn), lambda i, j, k: (layer, k, j))]
        + tail_specs + [pl.BlockSpec((ktail, bn), lambda i, j, k: (0, j))],
        out_specs=pl.BlockSpec((bm, bn), lambda i, j, k: (i, j)),
        out_shape=jax.ShapeDtypeStruct((m, n), F32),
        compiler_params=_cparams(("parallel", "parallel", "arbitrary")),
        name="matmul_ktiled",
    )(a, w, *([a] * n_tail), w_tail)


def _ret_kernel(lg_ref, cos_ref, sin_ref, q_ref, k_ref, v_ref, g_ref, ng_ref, o_ref,
                sf_ref, sb_ref, ob_ref, qr_ref, kr_ref, vb_ref, dec_ref, vec_ref, *, nb, rows, c):
    t = pl.program_id(2)
    half = RET_HEAD_DIM // 2
    nch = rows // c

    @pl.when(t == 0)
    def _():
        lgf = lg_ref[0, 0:1, :]
        lgb = lg_ref[0, 1:2, :]
        i_row = lax.broadcasted_iota(jnp.int32, (c, half), 0).astype(F32)
        rel = (lax.broadcasted_iota(jnp.int32, (c, c), 0) - lax.broadcasted_iota(jnp.int32, (c, c), 1)).astype(F32)
        dec_ref[...] = jnp.exp(jnp.where(rel >= 0, rel * lgf[:, :1], -rel * lgb[:, :1]))
        vec_ref[0] = jnp.exp((i_row + 1.0) * lgf)
        vec_ref[1] = jnp.exp((c - 1.0 - i_row) * lgf)
        vec_ref[2] = jnp.exp((c - i_row) * lgb)
        vec_ref[3] = jnp.exp(i_row * lgb)
        vec_ref[4] = jnp.exp(c * lgf) + jnp.zeros((c, half), F32)
        vec_ref[5] = jnp.exp(c * lgb) + jnp.zeros((c, half), F32)
        sf_ref[...] = jnp.zeros_like(sf_ref)
        sb_ref[...] = jnp.zeros_like(sb_ref)

    def lanes2(d):
        return jnp.concatenate([d, d], axis=-1)

    @pl.when(t < nb)
    def _():
        base = (nb - 1 - t) * rows
        for ci in range(nch - 1, -1, -1):
            r0 = ci * c
            g0 = pl.multiple_of(base + r0, c)
            cos = cos_ref[r0:r0 + c, :]
            sin = sin_ref[r0:r0 + c, :]

            def rot(x):
                x1, x2 = x[:, :half], x[:, half:]
                return jnp.concatenate([x1 * cos - x2 * sin, x1 * sin + x2 * cos], axis=-1)

            q = rot(q_ref[r0:r0 + c, :]) * (RET_HEAD_DIM ** -0.5)
            k = rot(k_ref[r0:r0 + c, :])
            qr_ref[pl.ds(g0, c), :] = q
            kr_ref[pl.ds(g0, c), :] = k
            vb = v_ref[r0:r0 + c, :].astype(BF16)
            vb_ref[pl.ds(g0, c), :] = vb
            sb = sb_ref[...]
            ob_ref[pl.ds(g0, c), :] = _dot((q * lanes2(vec_ref[2])).astype(BF16), sb.astype(BF16))
            kd = k * lanes2(vec_ref[3])
            sb_ref[...] = vec_ref[5][0:1, 0:1] * sb + _dot_tn(kd.astype(BF16), vb)

    @pl.when(t >= nb)
    def _():
        base = (t - nb) * rows
        for ci in range(nch):
            r0 = ci * c
            g0 = pl.multiple_of(base + r0, c)
            q = qr_ref[pl.ds(g0, c), :]
            k = kr_ref[pl.ds(g0, c), :]
            vb = vb_ref[pl.ds(g0, c), :]
            sf = sf_ref[...]
            scores = _dot_nt(q.astype(BF16), k.astype(BF16)) * dec_ref[...]
            o = (_dot(scores.astype(BF16), vb) + _dot((q * lanes2(vec_ref[0])).astype(BF16), sf.astype(BF16))
                 + ob_ref[pl.ds(g0, c), :])
            kd = k * lanes2(vec_ref[1])
            sf_ref[...] = vec_ref[4][0:1, 0:1] * sf + _dot_tn(kd.astype(BF16), vb)
            mu = jnp.mean(o, axis=-1, keepdims=True)
            oc = o - mu
            var = jnp.mean(oc * oc, axis=-1, keepdims=True)
            y = oc * lax.rsqrt(var + GN_EPS) * ng_ref[...]
            o_ref[r0:r0 + c, :] = (_silu(g_ref[r0:r0 + c, :]) * y).astype(o_ref.dtype)


def retention_mixer(proj, norm_g, batch, seq, col0):
    m = proj.shape[0]
    width = norm_g.shape[0]
    heads = width // RET_HEAD_DIM
    c = min(RET_CHUNK, seq)
    rows = min(RET_ROWS, seq)
    nb = seq // rows
    half = RET_HEAD_DIM // 2
    cb0 = col0 // RET_HEAD_DIM

    hidx = jnp.arange(heads, dtype=F32)
    gamma = 1.0 - jnp.power(2.0, -5.0 - hidx)
    lg = jnp.log(gamma)
    lg2 = jnp.broadcast_to(jnp.stack([lg, lg[::-1]], axis=1)[:, :, None], (heads, 2, LANES))
    inv = jnp.power(ROPE_BASE, -jnp.arange(half, dtype=F32) / half)
    ang = jnp.arange(seq, dtype=F32)[:, None] * inv[None, :]
    cos, sin = jnp.cos(ang), jnp.sin(ang)

    def bblk(t):
        return jnp.maximum(nb - 1 - t, 0)

    def fblk(t):
        return jnp.maximum(t - nb, 0)

    def spec(j, which):
        return pl.BlockSpec((rows, RET_HEAD_DIM), lambda b, h, t: (b * nb + which(t), cb0 + j * heads + h))

    return pl.pallas_call(
        functools.partial(_ret_kernel, nb=nb, rows=rows, c=c),
        grid=(batch, heads, 2 * nb),
        in_specs=[pl.BlockSpec((1, 2, LANES), lambda b, h, t: (h, 0, 0)),
                  pl.BlockSpec((rows, half), lambda b, h, t: (bblk(t), 0)),
                  pl.BlockSpec((rows, half), lambda b, h, t: (bblk(t), 0)),
                  spec(0, bblk), spec(1, bblk), spec(2, bblk), spec(3, fblk),
                  pl.BlockSpec((1, RET_HEAD_DIM), lambda b, h, t: (0, h))],
        out_specs=pl.BlockSpec((rows, RET_HEAD_DIM), lambda b, h, t: (b * nb + fblk(t), h)),
        out_shape=jax.ShapeDtypeStruct((m, width), BF16),
        scratch_shapes=[pltpu.VMEM((RET_HEAD_DIM, RET_HEAD_DIM), F32),
                        pltpu.VMEM((RET_HEAD_DIM, RET_HEAD_DIM), F32),
                        pltpu.VMEM((seq, RET_HEAD_DIM), F32),
                        pltpu.VMEM((seq, RET_HEAD_DIM), F32),
                        pltpu.VMEM((seq, RET_HEAD_DIM), F32),
                        pltpu.VMEM((seq, RET_HEAD_DIM), BF16),
                        pltpu.VMEM((c, c), F32),
                        pltpu.VMEM((6, c, half), F32)],
        compiler_params=_cparams(("parallel", "parallel", "arbitrary")),
        name="retention",
    )(lg2, cos, sin, proj, proj, proj, proj, norm_g.reshape(1, width))


def _hgrn_kernel(lbl_ref, lvl_ref, up_ref, sg_ref, cf_ref, q_ref, z_ref, v_ref, g_ref, ng_ref, a_ref, w_ref,
                 o_ref, po_ref, st_ref, ob_ref, wb_ref, *, nb, rows, sub, layer, depth):
    t = pl.program_id(2)
    d = HGRN_HEAD_DIM
    nlev = sub.bit_length() - 1
    nsb = rows // sub
    srows = a_ref.shape[0] // nsb

    def lower_bound(dirn):
        x = [lbl_ref[2 * l + dirn: 2 * l + dirn + 1, :] for l in range(depth)]
        mx = functools.reduce(jnp.maximum, x)
        e = [jnp.exp(xi - mx) for xi in x]
        tot = functools.reduce(lambda a, b: a + b, e)
        p = [ei / tot for ei in e]
        return functools.reduce(lambda a, b: a + b, p[:layer + 1]) - p[0]

    r_i = lax.broadcasted_iota(jnp.int32, (sub, sub), 0)
    c_i = lax.broadcasted_iota(jnp.int32, (sub, sub), 1)

    def split3(x):
        hi = x.astype(BF16)
        r1 = x - hi.astype(F32)
        mid = r1.astype(BF16)
        lo = (r1 - mid.astype(F32)).astype(BF16)
        return jnp.concatenate([hi, mid, lo], axis=1)

    def group_rows(x, size, pick):
        parts = [jnp.broadcast_to(x[base + pick:base + pick + 1, :], (size, d)) for base in range(0, sub, size)]
        return parts[0] if len(parts) == 1 else jnp.concatenate(parts, axis=0)

    def head(r0, rev, lb):
        f = lb + (1.0 - lb) * (1.0 / (1.0 + jnp.exp(-z_ref[r0:r0 + sub, :])))
        kk = 1.0 - f
        lf = jnp.log(f)
        q = q_ref[r0:r0 + sub, :] * (d ** -0.5)
        vb = v_ref[r0:r0 + sub, :].astype(BF16)
        tri = ((c_i >= r_i) if rev else (c_i <= r_i)).astype(BF16)
        parts = _dot(tri, split3(lf))
        return kk, lf, q, vb, parts

    def side_slice(n):
        po_ref[n * srows:(n + 1) * srows, :] = _dot(a_ref[n * srows:(n + 1) * srows, :], wb_ref[...])

    def rest(r0, rev, hd, next_slice):
        kk, lf, q, vb, parts = hd
        cum = parts[:, :d] + parts[:, d:2 * d] + parts[:, 2 * d:]
        edge = cum[0:1, :] if rev else cum[sub - 1:sub, :]

        st = st_ref[...]
        o = _dot_nt((q * jnp.exp(cum)).astype(BF16), st.astype(BF16))
        kd = kk * jnp.exp(edge - cum)
        st_ref[...] = st * jnp.exp(edge) + _dot_tn(vb, kd.astype(BF16))

        lf_up = pltpu.roll(lf, sub - 1, 0)
        lf_dn = pltpu.roll(lf, 1, 0)
        qk = q - kk
        prods = []
        for lev in range(nlev):
            m = 1 << lev
            di = 1 if rev else 0
            if m == 1:
                arg = lf * cf_ref[di, 0]
            elif m == 2:
                arg = lf_up * cf_ref[di, 1] + lf * cf_ref[di, 2] + lf_dn * cf_ref[di, 3]
            else:
                bound = group_rows(cum, 2 * m, m if rev else m - 1)
                arg = ((bound - cum) if rev else (cum - bound)) * sg_ref[lev]
            up = up_ref[lev]
            y = ((q - up * qk) if rev else (kk + up * qk)) * jnp.exp2(arg)
            yb = y.astype(BF16)
            prods.append(_dot_nt(yb, yb))
        if next_slice is not None:
            side_slice(next_slice)
        lvl = lvl_ref[1 if rev else 0]
        a = jnp.where(lvl == nlev, jnp.sum(q * kk, axis=-1, keepdims=True), 0.0)
        for lev in range(nlev):
            a = jnp.where(lvl == lev, prods[lev], a)
        return o + _dot(a.astype(BF16), vb)

    @pl.when(t == 0)
    def _():
        st_ref[...] = jnp.zeros_like(st_ref)

    @pl.when(t < nb)
    def _():
        wb_ref[...] = w_ref[...].astype(BF16)
        lb = lower_bound(1)
        base = (nb - 1 - t) * rows
        order = list(range(nsb - 1, -1, -1))
        hd = head(order[0] * sub, True, lb)
        side_slice(0)
        for n, si in enumerate(order):
            nxt = head(order[n + 1] * sub, True, lb) if n + 1 < nsb else None
            ob_ref[pl.ds(pl.multiple_of(base + si * sub, sub), sub), :] = rest(
                si * sub, True, hd, n + 1 if n + 1 < nsb else None)
            hd = nxt

    @pl.when(t == nb)
    def _():
        st_ref[...] = jnp.zeros_like(st_ref)

    @pl.when(t >= nb)
    def _():
        wb_ref[...] = w_ref[...].astype(BF16)
        lb = lower_bound(0)
        base = (t - nb) * rows
        hd = head(0, False, lb)
        side_slice(0)
        for si in range(nsb):
            r0 = si * sub
            nxt = head(r0 + sub, False, lb) if si + 1 < nsb else None
            o = rest(r0, False, hd, si + 1 if si + 1 < nsb else None)
            o = o + ob_ref[pl.ds(pl.multiple_of(base + r0, sub), sub), :]
            hd = nxt
            o = o * lax.rsqrt(jnp.mean(o * o, axis=-1, keepdims=True) + RMS_EPS)
            o_ref[r0:r0 + sub, :] = (_silu(g_ref[r0:r0 + sub, :]) * (o * ng_ref[...])).astype(o_ref.dtype)


def hgrn2_mixer(proj, lb_logits, norm_g, layer, batch, seq, col0, side_a, side_w, side_cols):
    m = proj.shape[0]
    width = norm_g.shape[0]
    d = HGRN_HEAD_DIM
    heads = width // d
    depth = lb_logits.shape[0]
    rows = min(HGRN_ROWS, seq)
    nb = seq // rows
    cb0 = col0 // d

    def blk(t):
        return jnp.where(t < nb, nb - 1 - t, t - nb)

    def fblk(t):
        return jnp.maximum(t - nb, 0)

    def spec(j):
        return pl.BlockSpec((rows, d), lambda b, h, t: (b * nb + blk(t), cb0 + j * heads + h))

    zspec = pl.BlockSpec((rows, d),
                         lambda b, h, t: (b * nb + blk(t), cb0 + jnp.where(t < nb, 2, 1) * heads + h))
    sub = min(HGRN_SUB, rows)
    assert sub & (sub - 1) == 0 and rows % sub == 0
    nlev = sub.bit_length() - 1
    idx = np.arange(sub)
    x = idx[:, None] ^ idx[None, :]
    lev = np.where(x > 0, np.floor(np.log2(np.maximum(x, 1))), nlev).astype(np.int32)
    lvl = jnp.asarray(np.stack([np.where(idx[:, None] >= idx[None, :], lev, -1),
                                np.where(idx[:, None] <= idx[None, :], lev, -1)]), jnp.int32)
    upper = np.stack([((idx >> l) & 1).astype(np.float32) for l in range(nlev)])
    up = jnp.broadcast_to(jnp.asarray(upper)[:, :, None], (nlev, sub, d))
    sg = jnp.broadcast_to(jnp.asarray((2.0 * upper - 1.0) * np.float32(math.log2(math.e)))[:, :, None],
                          (nlev, sub, d))
    tab = pl.BlockSpec((nlev, sub, d), lambda b, h, t: (0, 0, 0))
    r4 = idx & 3
    pat = np.array([[[0, 1, 0, 1], [1, 0, 0, 0], [0, 0, 1, 1], [0, 0, 0, 1]],
                    [[1, 0, 1, 0], [1, 0, 0, 0], [1, 1, 0, 0], [0, 0, 0, 1]]], np.float32)
    cf = jnp.broadcast_to(jnp.asarray(pat[:, :, r4] * np.float32(math.log2(math.e)))[:, :, :, None],
                          (2, 4, sub, d))
    steps = batch * heads * 2 * nb
    kdim = side_a.shape[1]
    sbn = _tile(side_cols, SIDE_BN)
    col_tiles = side_cols // sbn
    assert steps % col_tiles == 0 and m % (steps // col_tiles) == 0
    sbm = m // (steps // col_tiles)
    assert sbm % (rows // sub) == 0

    def step(b, h, t):
        return (b * heads + h) * (2 * nb) + t

    return pl.pallas_call(
        functools.partial(_hgrn_kernel, nb=nb, rows=rows, sub=sub, layer=layer, depth=depth),
        grid=(batch, heads, 2 * nb),
        in_specs=[pl.BlockSpec((2 * depth, d), lambda b, h, t: (0, h)),
                  pl.BlockSpec((2, sub, sub), lambda b, h, t: (0, 0, 0)), tab, tab,
                  pl.BlockSpec((2, 4, sub, d), lambda b, h, t: (0, 0, 0, 0)),
                  spec(0), zspec, spec(3),
                  pl.BlockSpec((rows, d), lambda b, h, t: (b * nb + fblk(t), cb0 + 4 * heads + h)),
                  pl.BlockSpec((1, d), lambda b, h, t: (0, h)),
                  pl.BlockSpec((sbm, kdim), lambda b, h, t: (step(b, h, t) // col_tiles, 0)),
                  pl.BlockSpec((None, kdim, sbn), lambda b, h, t: (layer, 0, step(b, h, t) % col_tiles))],
        out_specs=[pl.BlockSpec((rows, d), lambda b, h, t: (b * nb + fblk(t), h)),
                   pl.BlockSpec((sbm, sbn), lambda b, h, t: (step(b, h, t) // col_tiles, step(b, h, t) % col_tiles))],
        out_shape=[jax.ShapeDtypeStruct((m, width), BF16), jax.ShapeDtypeStruct((m, side_cols), F32)],
        scratch_shapes=[pltpu.VMEM((d, d), F32),
                        pltpu.VMEM((seq, d), F32),
                        pltpu.VMEM((kdim, sbn), BF16)],
        compiler_params=_cparams(("arbitrary", "arbitrary", "arbitrary")),
        name="hgrn2",
    )(lb_logits.reshape(2 * depth, width), lvl, up, sg, cf, proj, proj, proj, proj, norm_g.reshape(1, width),
      side_a, side_w)


def _dft_cos_sin(n):
    idx = np.arange(n)
    ang = 2.0 * np.pi * ((idx[:, None] * idx[None, :]) % n) / n
    return np.cos(ang), np.sin(ang)


def _fft_chan_kernel(z_ref, m_ref, vr_ref, vi_ref, *, groups):
    gd = FOURIER_GROUP_DIM
    mat = m_ref[...]
    for g in range(groups):
        pq = _dot3_rhs(z_ref[:, g * gd:(g + 1) * gd], mat)
        vr_ref[:, g * gd:(g + 1) * gd] = pq[:, :gd]
        vi_ref[:, g * gd:(g + 1) * gd] = pq[:, gd:]


def _fft_stage1_kernel(vr_ref, vi_ref, m_ref, twc_ref, tws_ref, ur_ref, ui_ref, *, n1, reps, nsub):
    for bi in range(nsub):
        x = jnp.concatenate([vr_ref[:, bi, :], vi_ref[:, bi, :]], axis=0)
        tt = _dot3_lhs(m_ref[...], x)
        tr, ti = tt[:n1], tt[n1:]
        cw = jnp.tile(twc_ref[bi], (1, reps))
        sw = jnp.tile(tws_ref[bi], (1, reps))
        ur_ref[:, bi, :] = tr * cw + ti * sw
        ui_ref[:, bi, :] = ti * cw - tr * sw


def _fft_stage2_kernel(ur_ref, ui_ref, m_ref, w_ref, b_ref, o_ref, y_ref, *, groups, scale, n2, nsub):
    gd = FOURIER_GROUP_DIM
    for ki in range(nsub):
        x = jnp.concatenate([ur_ref[ki * n2:(ki + 1) * n2, :], ui_ref[ki * n2:(ki + 1) * n2, :]], axis=0)
        spec = _dot3_lhs(m_ref[...], x) * scale
        for g in range(groups):
            y = _dot(spec[:, g * gd:(g + 1) * gd].astype(BF16), w_ref[g].astype(BF16))
            y_ref[:, ki, g * gd:(g + 1) * gd] = y + b_ref[:, g * gd:(g + 1) * gd]
    o_ref[...] = y_ref[...].astype(o_ref.dtype)


def fourier_mixer(proj, w, bias, batch, seq, col0):
    m = proj.shape[0]
    groups, gd, _ = w.shape
    width = groups * gd
    n1 = min(FFT_N1, seq)
    n2 = seq // n1
    br = min(512, m)

    cc, sc = _dft_cos_sin(gd)
    chan = _const3_rhs(np.concatenate([cc, -sc], axis=1))
    vr, vi = pl.pallas_call(
        functools.partial(_fft_chan_kernel, groups=groups),
        grid=(m // br,),
        in_specs=[pl.BlockSpec((br, width), lambda i: (i, col0 // width)),
                  pl.BlockSpec((3 * gd, 2 * gd), lambda i: (0, 0))],
        out_specs=[pl.BlockSpec((br, width), lambda i: (i, 0))] * 2,
        out_shape=[jax.ShapeDtypeStruct((m, width), F32)] * 2,
        compiler_params=_cparams(("parallel",)),
        name="fft_channels",
    )(proj, chan)

    c1, s1 = _dft_cos_sin(n1)
    m1 = _const3_lhs(np.block([[c1, s1], [-s1, c1]]))
    k1 = np.arange(n1)
    bb = np.arange(n2)
    tw = 2.0 * np.pi * ((bb[:, None] * k1[None, :]) % seq) / seq
    twc = jnp.broadcast_to(jnp.asarray(np.cos(tw), F32)[:, :, None], (n2, n1, LANES))
    tws = jnp.broadcast_to(jnp.asarray(np.sin(tw), F32)[:, :, None], (n2, n1, LANES))
    s1 = min(FFT_SUB1, n2)
    vspec = pl.BlockSpec((n1, s1, width), lambda bt, j: (bt, j, 0))
    tspec = pl.BlockSpec((s1, n1, LANES), lambda bt, j: (j, 0, 0))
    ur, ui = pl.pallas_call(
        functools.partial(_fft_stage1_kernel, n1=n1, reps=width // LANES, nsub=s1),
        grid=(batch, n2 // s1),
        in_specs=[vspec, vspec, pl.BlockSpec((2 * n1, 6 * n1), lambda bt, j: (0, 0)), tspec, tspec],
        out_specs=[vspec, vspec],
        out_shape=[jax.ShapeDtypeStruct((batch * n1, n2, width), F32)] * 2,
        compiler_params=_cparams(("parallel", "parallel")),
        name="fft_stage1",
    )(vr.reshape(batch * n1, n2, width), vi.reshape(batch * n1, n2, width), m1, twc, tws)

    c2, s2 = _dft_cos_sin(n2)
    m2 = _const3_lhs(np.concatenate([c2, s2], axis=1))
    s2 = min(FFT_SUB2, n1)
    uspec = pl.BlockSpec((s2 * n2, width), lambda bt, j: (bt * (n1 // s2) + j, 0))
    out = pl.pallas_call(
        functools.partial(_fft_stage2_kernel, groups=groups, scale=1.0 / math.sqrt(seq * gd), n2=n2, nsub=s2),
        grid=(batch, n1 // s2),
        in_specs=[uspec, uspec, pl.BlockSpec((n2, 6 * n2), lambda bt, j: (0, 0)),
                  pl.BlockSpec((groups, gd, gd), lambda bt, j: (0, 0, 0)),
                  pl.BlockSpec((1, width), lambda bt, j: (0, 0))],
        out_specs=pl.BlockSpec((n2, s2, width), lambda bt, j: (bt, j, 0)),
        out_shape=jax.ShapeDtypeStruct((batch * n2, n1, width), BF16),
        scratch_shapes=[pltpu.VMEM((n2, s2, width), F32)],
        compiler_params=_cparams(("parallel", "parallel")),
        name="fft_stage2",
    )(ur.reshape(m, width), ui.reshape(m, width), m2, w, bias.reshape(1, width))
    return out.reshape(m, width)


def _xattn_block_kernel(h_ref, y_ref, g_ref, wq_ref, wo_ref, k_ref, v_ref, ho_ref, f_ref, *, heads):
    hd = XATTN_HEAD_DIM
    h1 = h_ref[...] + _rms(y_ref[...], g_ref[0:1, :])
    c = _rms(h1, g_ref[1:2, :]).astype(BF16)
    q = (_dot(c, wq_ref[...]) * (hd ** -0.5)).astype(BF16)
    outs = []
    for hh in range(heads):
        cols = slice(hh * hd, (hh + 1) * hd)
        s = _dot_nt(q[:, cols], k_ref[:, cols])
        p = jnp.exp(s - jnp.max(s, axis=-1, keepdims=True))
        denom = jnp.sum(p, axis=-1, keepdims=True)
        outs.append((_dot(p.astype(BF16), v_ref[:, cols]) / denom).astype(BF16))
    xa = _dot(jnp.concatenate(outs, axis=1), wo_ref[...])
    h2 = h1 + _rms(xa, g_ref[2:3, :])
    ho_ref[...] = h2
    f_ref[...] = _rms(h2, g_ref[3:4, :]).astype(f_ref.dtype)


def xattn_block(h, y, gains, wq, wo, kx, vx, seq, mem_tokens):
    m, d = h.shape
    width = wq.shape[1]
    bm = min(XATTN_ROWS, seq)
    nb = seq // bm
    row = pl.BlockSpec((bm, d), lambda i: (i, 0))

    def const(shape):
        return pl.BlockSpec(shape, lambda i: (0, 0), pipeline_mode=pl.Buffered(1))

    kv = pl.BlockSpec((mem_tokens, width), lambda i: (i // nb, 0))
    return pl.pallas_call(
        functools.partial(_xattn_block_kernel, heads=width // XATTN_HEAD_DIM),
        grid=(m // bm,),
        in_specs=[row, row, const((4, d)), const((d, width)), const((width, d)), kv, kv],
        out_specs=[row, row],
        out_shape=[jax.ShapeDtypeStruct((m, d), F32), jax.ShapeDtypeStruct((m, d), BF16)],
        compiler_params=_cparams(("arbitrary",)),
        name="xattn_block",
    )(h, y, gains, wq, wo, kx, vx)


def kernel(x, mem, mem_norm_g, pre_mix_g, w_in, ret_norm_g, hgrn_lb_logits, hgrn_norm_g, fourier_w, fourier_b,
           w_out, post_mix_g, pre_xattn_g, xattn_wq, xattn_wk, xattn_wv, xattn_wo, post_xattn_g, pre_ffn_g,
           ffn_w_gate, ffn_w_up, ffn_w_down, post_ffn_g):
    batch, seq, d = x.shape
    depth = w_in.shape[0]
    mem_tokens = mem.shape[1]
    m = batch * seq
    ret_w = ret_norm_g.shape[1]
    hgrn_w = hgrn_norm_g.shape[1]

    h = x.reshape(m, d)
    mem_n = rmsnorm_rows(mem.reshape(batch * mem_tokens, d), mem_norm_g)
    a = rmsnorm_rows(h, pre_mix_g[0])
    for l in range(depth):
        proj_hf = matmul([a], w_in, l, F32, *MM_TILE, col0=4 * ret_w)
        y_hgrn, proj_r = hgrn2_mixer(proj_hf, hgrn_lb_logits, hgrn_norm_g[l], l, batch, seq, 0, a, w_in, 4 * ret_w)
        y_ret = retention_mixer(proj_r, ret_norm_g[l], batch, seq, 0)
        y_fft = fourier_mixer(proj_hf, fourier_w[l], fourier_b[l], batch, seq, 5 * hgrn_w)
        mixed = matmul([y_ret, y_hgrn, y_fft], w_out, l, F32, *MM_TILE)
        kx = matmul([mem_n], xattn_wk, l, BF16, *MM_TILE_MEM)
        vx = matmul([mem_n], xattn_wv, l, BF16, *MM_TILE_MEM)
        gains = jnp.stack([post_mix_g[l], pre_xattn_g[l], post_xattn_g[l], pre_ffn_g[l]])
        h, f = xattn_block(h, mixed, gains, xattn_wq[l].astype(BF16), xattn_wo[l].astype(BF16), kx, vx,
                           seq, mem_tokens)
        hid = swiglu_matmul(f, ffn_w_gate, ffn_w_up, l, *MM_TILE_SWIGLU)
        ff = matmul_ktiled(hid, ffn_w_down, l, *MM_TILE_DOWN)
        h, a = add_norm(h, ff, post_ffn_g[l], pre_mix_g[l + 1] if l + 1 < depth else None)
    return h.reshape(batch, seq, d)
```

```python
import functools
import math

import numpy as np
import jax
import jax.numpy as jnp
from jax import lax
from jax.experimental import pallas as pl
from jax.experimental.pallas import tpu as pltpu

F32 = jnp.float32
BF16 = jnp.bfloat16

RET_HEAD_DIM = 256
HGRN_HEAD_DIM = 128
FOURIER_GROUP_DIM = 128
XATTN_HEAD_DIM = 256
ROPE_BASE = 10000.0
RMS_EPS = 1e-6
GN_EPS = 1e-5

LANES = 128
MXU_TILE = 256
V7X_VMEM_BYTES = 64 * 1024 * 1024
VMEM_LIMIT = V7X_VMEM_BYTES - 8 * 1024 * 1024

MM_TILE = (2048, 512)
MM_TILE_SWIGLU = (2048, 256)
MM_TILE_MEM = (512, 512)
MM_TILE_DOWN = (2048, 1024, 1024)
RET_CHUNK = MXU_TILE
RET_ROWS = 2048
HGRN_ROWS = 1024
SIDE_BN = 512
HGRN_SUB = 128
XATTN_ROWS = 256
FFT_N1 = 64
FFT_SUB1 = 8
FFT_SUB2 = 16


def _cparams(sem):
    return pltpu.CompilerParams(dimension_semantics=sem, vmem_limit_bytes=VMEM_LIMIT)


def _tile(n, preferred):
    for t in range(min(preferred, n) // LANES * LANES, 0, -LANES):
        if n % t == 0:
            return t
    return n


def _dot(a, b):
    return jnp.dot(a, b, preferred_element_type=F32)


def _dot_nt(a, b):
    return lax.dot_general(a, b, (((1,), (1,)), ((), ())), preferred_element_type=F32)


def _dot_tn(a, b):
    return lax.dot_general(a, b, (((0,), (0,)), ((), ())), preferred_element_type=F32)


def _split2(x):
    hi = x.astype(BF16)
    return hi, (x - hi.astype(F32)).astype(BF16)


def _const3_lhs(mat):
    hi, lo = _split2(jnp.asarray(mat, F32))
    return jnp.concatenate([hi, hi, lo], axis=1)


def _const3_rhs(mat):
    hi, lo = _split2(jnp.asarray(mat, F32))
    return jnp.concatenate([hi, hi, lo], axis=0)


def _dot3_lhs(m3, x):
    hi, lo = _split2(x)
    return _dot(m3, jnp.concatenate([hi, lo, hi], axis=0))


def _dot3_rhs(x, m3):
    hi, lo = _split2(x)
    return _dot(jnp.concatenate([hi, lo, hi], axis=1), m3)


def _rms(x, g):
    ms = jnp.mean(x * x, axis=-1, keepdims=True)
    return x * lax.rsqrt(ms + RMS_EPS) * g


def _silu(x):
    return x * (1.0 / (1.0 + jnp.exp(-x)))


def _norm_kernel(x_ref, g_ref, o_ref):
    o_ref[...] = _rms(x_ref[...], g_ref[...]).astype(o_ref.dtype)


def rmsnorm_rows(x, g, br=256):
    m, d = x.shape
    out_dtype = BF16
    br = min(br, m)
    return pl.pallas_call(
        _norm_kernel,
        grid=(m // br,),
        in_specs=[pl.BlockSpec((br, d), lambda i: (i, 0)),
                  pl.BlockSpec((1, d), lambda i: (0, 0))],
        out_specs=pl.BlockSpec((br, d), lambda i: (i, 0)),
        out_shape=jax.ShapeDtypeStruct((m, d), out_dtype),
        compiler_params=_cparams(("parallel",)),
        name="rmsnorm_rows",
    )(x, g.reshape(1, d))


def _add_norm_kernel(h_ref, y_ref, gp_ref, gn_ref, ho_ref, a_ref):
    h = h_ref[...] + _rms(y_ref[...], gp_ref[...])
    ho_ref[...] = h
    a_ref[...] = _rms(h, gn_ref[...]).astype(a_ref.dtype)


def _add_norm_last_kernel(h_ref, y_ref, gp_ref, ho_ref):
    ho_ref[...] = h_ref[...] + _rms(y_ref[...], gp_ref[...])


def add_norm(h, y, g_post, g_next, br=256):
    m, d = h.shape
    row = pl.BlockSpec((br, d), lambda i: (i, 0))
    vec = pl.BlockSpec((1, d), lambda i: (0, 0))
    if g_next is None:
        return pl.pallas_call(
            _add_norm_last_kernel, grid=(m // br,),
            in_specs=[row, row, vec], out_specs=row,
            out_shape=jax.ShapeDtypeStruct((m, d), F32),
            compiler_params=_cparams(("parallel",)), name="add_norm_last",
        )(h, y, g_post.reshape(1, d)), None
    return pl.pallas_call(
        _add_norm_kernel, grid=(m // br,),
        in_specs=[row, row, vec, vec], out_specs=[row, row],
        out_shape=[jax.ShapeDtypeStruct((m, d), F32), jax.ShapeDtypeStruct((m, d), BF16)],
        compiler_params=_cparams(("parallel",)), name="add_norm",
    )(h, y, g_post.reshape(1, d), g_next.reshape(1, d))


def _row_resident_spec(bm, kdim):
    return pl.BlockSpec((bm, kdim), lambda i, j: (i, 0), pipeline_mode=pl.Buffered(1))


def _mm_kernel(*refs, k_parts):
    a_refs, w_ref, o_ref = refs[:len(k_parts)], refs[len(k_parts)], refs[-1]
    acc, r0 = None, 0
    for a_ref, kp in zip(a_refs, k_parts):
        part = _dot(a_ref[...], w_ref[r0:r0 + kp, :].astype(BF16))
        acc = part if acc is None else acc + part
        r0 += kp
    o_ref[...] = acc.astype(o_ref.dtype)


def matmul(a_parts, w, layer, out_dtype, bm, bn, col0=0):
    m = a_parts[0].shape[0]
    k_parts = tuple(a.shape[1] for a in a_parts)
    kdim = w.shape[1]
    n = w.shape[2] - col0
    assert sum(k_parts) == kdim
    bm, bn = _tile(m, bm), _tile(math.gcd(n, col0) if col0 else n, bn)
    cb0 = col0 // bn
    return pl.pallas_call(
        functools.partial(_mm_kernel, k_parts=k_parts),
        grid=(m // bm, n // bn),
        in_specs=[_row_resident_spec(bm, kp) for kp in k_parts]
        + [pl.BlockSpec((None, kdim, bn), lambda i, j: (layer, 0, cb0 + j))],
        out_specs=pl.BlockSpec((bm, bn), lambda i, j: (i, j)),
        out_shape=jax.ShapeDtypeStruct((m, n), out_dtype),
        compiler_params=_cparams(("parallel", "arbitrary")),
        name="matmul",
    )(*a_parts, w)


def _swiglu_kernel(a_ref, wg_ref, wu_ref, o_ref):
    a = a_ref[...]
    gate = _dot(a, wg_ref[...].astype(BF16))
    up = _dot(a, wu_ref[...].astype(BF16))
    o_ref[...] = (_silu(gate) * up).astype(o_ref.dtype)


def swiglu_matmul(a, wg, wu, layer, bm, bn):
    m, kdim = a.shape
    n = wg.shape[2]
    bm, bn = _tile(m, bm), _tile(n, bn)
    wspec = pl.BlockSpec((None, kdim, bn), lambda i, j: (layer, 0, j))
    return pl.pallas_call(
        _swiglu_kernel,
        grid=(m // bm, n // bn),
        in_specs=[_row_resident_spec(bm, kdim), wspec, wspec],
        out_specs=pl.BlockSpec((bm, bn), lambda i, j: (i, j)),
        out_shape=jax.ShapeDtypeStruct((m, n), BF16),
        compiler_params=_cparams(("parallel", "arbitrary")),
        name="swiglu_matmul",
    )(a, wg, wu)


def _mm_ktiled_kernel(*refs, n_tail):
    a_ref, w_ref = refs[0], refs[1]
    t_refs, wt_refs, o_ref = refs[2:2 + n_tail], refs[2 + n_tail:2 + 2 * n_tail], refs[-1]
    k = pl.program_id(2)

    @pl.when(k == 0)
    def _():
        tail = jnp.concatenate([t[...] for t in t_refs], axis=1)
        w_tail = jnp.concatenate([r[...] for r in wt_refs], axis=0)
        o_ref[...] = _dot(a_ref[...], w_ref[...].astype(BF16)) + _dot(tail, w_tail.astype(BF16))

    @pl.when(k > 0)
    def _():
        o_ref[...] += _dot(a_ref[...], w_ref[...].astype(BF16))


def matmul_ktiled(a, w, layer, bm, bn, bk):
    m, kdim = a.shape
    n = w.shape[2]
    bm, bn = _tile(m, bm), _tile(n, bn)
    nk = kdim // bk
    kmain = nk * bk
    ktail = kdim - kmain
    assert nk >= 1 and ktail > 0 and ktail % MXU_TILE == 0 and kmain % MXU_TILE == 0
    n_tail = ktail // MXU_TILE
    tb0 = kmain // MXU_TILE
    a_tail_specs = [pl.BlockSpec((bm, MXU_TILE), functools.partial(lambda i, j, k, c: (i, c), c=tb0 + c))
                    for c in range(n_tail)]
    w_tail_specs = [pl.BlockSpec((None, MXU_TILE, bn), functools.partial(lambda i, j, k, c: (layer, c, j), c=tb0 + c))
                    for c in range(n_tail)]
    return pl.pallas_call(
        functools.partial(_mm_ktiled_kernel, n_tail=n_tail),
        grid=(m // bm, n // bn, nk),
        in_specs=[pl.BlockSpec((bm, bk), lambda i, j, k: (i, k)),
                  pl.BlockSpec((None, bk, bn), lambda i, j, k: (layer, k, j))]
        + a_tail_specs + w_tail_specs,
        out_specs=pl.BlockSpec((bm, bn), lambda i, j, k: (i, j)),
        out_shape=jax.ShapeDtypeStruct((m, n), F32),
        compiler_params=_cparams(("parallel", "parallel", "arbitrary")),
        name="matmul_ktiled",
    )(a, w, *([a] * n_tail), *([w] * n_tail))


def _ret_kernel(lg_ref, cos_ref, sin_ref, q_ref, k_ref, v_ref, g_ref, ng_ref, o_ref,
                sf_ref, sb_ref, ob_ref, qr_ref, kr_ref, vb_ref, dec_ref, vec_ref, *, nb, rows, c):
    t = pl.program_id(2)
    half = RET_HEAD_DIM // 2
    nch = rows // c

    @pl.when(t == 0)
    def _():
        lgf = lg_ref[0, 0:1, :]
        lgb = lg_ref[0, 1:2, :]
        i_row = lax.broadcasted_iota(jnp.int32, (c, half), 0).astype(F32)
        rel = (lax.broadcasted_iota(jnp.int32, (c, c), 0) - lax.broadcasted_iota(jnp.int32, (c, c), 1)).astype(F32)
        dec_ref[...] = jnp.exp(jnp.where(rel >= 0, rel * lgf[:, :1], -rel * lgb[:, :1]))
        vec_ref[0] = jnp.exp((i_row + 1.0) * lgf)
        vec_ref[1] = jnp.exp((c - 1.0 - i_row) * lgf)
        vec_ref[2] = jnp.exp((c - i_row) * lgb)
        vec_ref[3] = jnp.exp(i_row * lgb)
        vec_ref[4] = jnp.exp(c * lgf) + jnp.zeros((c, half), F32)
        vec_ref[5] = jnp.exp(c * lgb) + jnp.zeros((c, half), F32)
        sf_ref[...] = jnp.zeros_like(sf_ref)
        sb_ref[...] = jnp.zeros_like(sb_ref)

    def lanes2(d):
        return jnp.concatenate([d, d], axis=-1)

    @pl.when(t < nb)
    def _():
        base = (nb - 1 - t) * rows
        for ci in range(nch - 1, -1, -1):
            r0 = ci * c
            g0 = pl.multiple_of(base + r0, c)
            cos = cos_ref[r0:r0 + c, :]
            sin = sin_ref[r0:r0 + c, :]

            def rot(x):
                x1, x2 = x[:, :half], x[:, half:]
                return jnp.concatenate([x1 * cos - x2 * sin, x1 * sin + x2 * cos], axis=-1)

            q = rot(q_ref[r0:r0 + c, :]) * (RET_HEAD_DIM ** -0.5)
            k = rot(k_ref[r0:r0 + c, :])
            qr_ref[pl.ds(g0, c), :] = q
            kr_ref[pl.ds(g0, c), :] = k
            vb = v_ref[r0:r0 + c, :].astype(BF16)
            vb_ref[pl.ds(g0, c), :] = vb
            sb = sb_ref[...]
            ob_ref[pl.ds(g0, c), :] = _dot((q * lanes2(vec_ref[2])).astype(BF16), sb.astype(BF16))
            kd = k * lanes2(vec_ref[3])
            sb_ref[...] = vec_ref[5][0:1, 0:1] * sb + _dot_tn(kd.astype(BF16), vb)

    @pl.when(t >= nb)
    def _():
        base = (t - nb) * rows
        for ci in range(nch):
            r0 = ci * c
            g0 = pl.multiple_of(base + r0, c)
            q = qr_ref[pl.ds(g0, c), :]
            k = kr_ref[pl.ds(g0, c), :]
            vb = vb_ref[pl.ds(g0, c), :]
            sf = sf_ref[...]
            scores = _dot_nt(q.astype(BF16), k.astype(BF16)) * dec_ref[...]
            o = (_dot(scores.astype(BF16), vb) + _dot((q * lanes2(vec_ref[0])).astype(BF16), sf.astype(BF16))
                 + ob_ref[pl.ds(g0, c), :])
            kd = k * lanes2(vec_ref[1])
            sf_ref[...] = vec_ref[4][0:1, 0:1] * sf + _dot_tn(kd.astype(BF16), vb)
            mu = jnp.mean(o, axis=-1, keepdims=True)
            oc = o - mu
            var = jnp.mean(oc * oc, axis=-1, keepdims=True)
            y = oc * lax.rsqrt(var + GN_EPS) * ng_ref[...]
            o_ref[r0:r0 + c, :] = (_silu(g_ref[r0:r0 + c, :]) * y).astype(o_ref.dtype)


def retention_mixer(proj, norm_g, batch, seq, col0):
    m = proj.shape[0]
    width = norm_g.shape[0]
    heads = width // RET_HEAD_DIM
    c = min(RET_CHUNK, seq)
    rows = min(RET_ROWS, seq)
    nb = seq // rows
    half = RET_HEAD_DIM // 2
    cb0 = col0 // RET_HEAD_DIM

    hidx = jnp.arange(heads, dtype=F32)
    gamma = 1.0 - jnp.power(2.0, -5.0 - hidx)
    lg = jnp.log(gamma)
    lg2 = jnp.broadcast_to(jnp.stack([lg, lg[::-1]], axis=1)[:, :, None], (heads, 2, LANES))
    inv = jnp.power(ROPE_BASE, -jnp.arange(half, dtype=F32) / half)
    ang = jnp.arange(seq, dtype=F32)[:, None] * inv[None, :]
    cos, sin = jnp.cos(ang), jnp.sin(ang)

    def bblk(t):
        return jnp.maximum(nb - 1 - t, 0)

    def fblk(t):
        return jnp.maximum(t - nb, 0)

    def spec(j, which):
        return pl.BlockSpec((rows, RET_HEAD_DIM), lambda b, h, t: (b * nb + which(t), cb0 + j * heads + h))

    return pl.pallas_call(
        functools.partial(_ret_kernel, nb=nb, rows=rows, c=c),
        grid=(batch, heads, 2 * nb),
        in_specs=[pl.BlockSpec((1, 2, LANES), lambda b, h, t: (h, 0, 0)),
                  pl.BlockSpec((rows, half), lambda b, h, t: (bblk(t), 0)),
                  pl.BlockSpec((rows, half), lambda b, h, t: (bblk(t), 0)),
                  spec(0, bblk), spec(1, bblk), spec(2, bblk), spec(3, fblk),
                  pl.BlockSpec((1, RET_HEAD_DIM), lambda b, h, t: (0, h))],
        out_specs=pl.BlockSpec((rows, RET_HEAD_DIM), lambda b, h, t: (b * nb + fblk(t), h)),
        out_shape=jax.ShapeDtypeStruct((m, width), BF16),
        scratch_shapes=[pltpu.VMEM((RET_HEAD_DIM, RET_HEAD_DIM), F32),
                        pltpu.VMEM((RET_HEAD_DIM, RET_HEAD_DIM), F32),
                        pltpu.VMEM((seq, RET_HEAD_DIM), F32),
                        pltpu.VMEM((seq, RET_HEAD_DIM), F32),
                        pltpu.VMEM((seq, RET_HEAD_DIM), F32),
                        pltpu.VMEM((seq, RET_HEAD_DIM), BF16),
                        pltpu.VMEM((c, c), F32),
                        pltpu.VMEM((6, c, half), F32)],
        compiler_params=_cparams(("parallel", "parallel", "arbitrary")),
        name="retention",
    )(lg2, cos, sin, proj, proj, proj, proj, norm_g.reshape(1, width))


def _hgrn_kernel(lbl_ref, lvl_ref, up_ref, sg_ref, cf_ref, q_ref, z_ref, v_ref, g_ref, ng_ref, a_ref, w_ref,
                 o_ref, po_ref, st_ref, ob_ref, wb_ref, *, nb, rows, sub, layer, depth):
    t = pl.program_id(2)
    d = HGRN_HEAD_DIM
    nlev = sub.bit_length() - 1
    nsb = rows // sub
    srows = a_ref.shape[0] // nsb

    def lower_bound(dirn):
        x = [lbl_ref[2 * l + dirn: 2 * l + dirn + 1, :] for l in range(depth)]
        mx = functools.reduce(jnp.maximum, x)
        e = [jnp.exp(xi - mx) for xi in x]
        tot = functools.reduce(lambda a, b: a + b, e)
        p = [ei / tot for ei in e]
        return functools.reduce(lambda a, b: a + b, p[:layer + 1]) - p[0]

    r_i = lax.broadcasted_iota(jnp.int32, (sub, sub), 0)
    c_i = lax.broadcasted_iota(jnp.int32, (sub, sub), 1)

    def split3(x):
        hi = x.astype(BF16)
        r1 = x - hi.astype(F32)
        mid = r1.astype(BF16)
        lo = (r1 - mid.astype(F32)).astype(BF16)
        return jnp.concatenate([hi, mid, lo], axis=1)

    def group_rows(x, size, pick):
        parts = [jnp.broadcast_to(x[base + pick:base + pick + 1, :], (size, d)) for base in range(0, sub, size)]
        return parts[0] if len(parts) == 1 else jnp.concatenate(parts, axis=0)

    def head(r0, rev, lb):
        f = lb + (1.0 - lb) * (1.0 / (1.0 + jnp.exp(-z_ref[r0:r0 + sub, :])))
        kk = 1.0 - f
        lf = jnp.log(f)
        q = q_ref[r0:r0 + sub, :] * (d ** -0.5)
        vb = v_ref[r0:r0 + sub, :].astype(BF16)
        tri = ((c_i >= r_i) if rev else (c_i <= r_i)).astype(BF16)
        parts = _dot(tri, split3(lf))
        return kk, lf, q, vb, parts

    def side_slice(n):
        po_ref[n * srows:(n + 1) * srows, :] = _dot(a_ref[n * srows:(n + 1) * srows, :], wb_ref[...])

    def rest(r0, rev, hd, next_slice):
        kk, lf, q, vb, parts = hd
        cum = parts[:, :d] + parts[:, d:2 * d] + parts[:, 2 * d:]
        edge = cum[0:1, :] if rev else cum[sub - 1:sub, :]

        st = st_ref[...]
        o = _dot_nt((q * jnp.exp(cum)).astype(BF16), st.astype(BF16))
        kd = kk * jnp.exp(edge - cum)
        st_ref[...] = st * jnp.exp(edge) + _dot_tn(vb, kd.astype(BF16))

        lf_up = pltpu.roll(lf, sub - 1, 0)
        lf_dn = pltpu.roll(lf, 1, 0)
        qk = q - kk
        prods = []
        for lev in range(nlev):
            m = 1 << lev
            di = 1 if rev else 0
            if m == 1:
                arg = lf * cf_ref[di, 0]
            elif m == 2:
                arg = lf_up * cf_ref[di, 1] + lf * cf_ref[di, 2] + lf_dn * cf_ref[di, 3]
            else:
                bound = group_rows(cum, 2 * m, m if rev else m - 1)
                arg = ((bound - cum) if rev else (cum - bound)) * sg_ref[lev]
            up = up_ref[lev]
            y = ((q - up * qk) if rev else (kk + up * qk)) * jnp.exp2(arg)
            yb = y.astype(BF16)
            prods.append(_dot_nt(yb, yb))
        if next_slice is not None:
            side_slice(next_slice)
        lvl = lvl_ref[1 if rev else 0]
        a = jnp.where(lvl == nlev, jnp.sum(q * kk, axis=-1, keepdims=True), 0.0)
        for lev in range(nlev):
            a = jnp.where(lvl == lev, prods[lev], a)
        return o + _dot(a.astype(BF16), vb)

    @pl.when(t == 0)
    def _():
        st_ref[...] = jnp.zeros_like(st_ref)

    @pl.when(t < nb)
    def _():
        wb_ref[...] = w_ref[...].astype(BF16)
        lb = lower_bound(1)
        base = (nb - 1 - t) * rows
        order = list(range(nsb - 1, -1, -1))
        hd = head(order[0] * sub, True, lb)
        side_slice(0)
        for n, si in enumerate(order):
            nxt = head(order[n + 1] * sub, True, lb) if n + 1 < nsb else None
            ob_ref[pl.ds(pl.multiple_of(base + si * sub, sub), sub), :] = rest(
                si * sub, True, hd, n + 1 if n + 1 < nsb else None)
            hd = nxt

    @pl.when(t == nb)
    def _():
        st_ref[...] = jnp.zeros_like(st_ref)

    @pl.when(t >= nb)
    def _():
        wb_ref[...] = w_ref[...].astype(BF16)
        lb = lower_bound(0)
        base = (t - nb) * rows
        hd = head(0, False, lb)
        side_slice(0)
        for si in range(nsb):
            r0 = si * sub
            nxt = head(r0 + sub, False, lb) if si + 1 < nsb else None
            o = rest(r0, False, hd, si + 1 if si + 1 < nsb else None)
            o = o + ob_ref[pl.ds(pl.multiple_of(base + r0, sub), sub), :]
            hd = nxt
            o = o * lax.rsqrt(jnp.mean(o * o, axis=-1, keepdims=True) + RMS_EPS)
            o_ref[r0:r0 + sub, :] = (_silu(g_ref[r0:r0 + sub, :]) * (o * ng_ref[...])).astype(o_ref.dtype)


def hgrn2_mixer(proj, lb_logits, norm_g, layer, batch, seq, col0, side_a, side_w, side_cols):
    m = proj.shape[0]
    width = norm_g.shape[0]
    d = HGRN_HEAD_DIM
    heads = width // d
    depth = lb_logits.shape[0]
    rows = min(HGRN_ROWS, seq)
    nb = seq // rows
    cb0 = col0 // d

    def blk(t):
        return jnp.where(t < nb, nb - 1 - t, t - nb)

    def fblk(t):
        return jnp.maximum(t - nb, 0)

    def spec(j):
        return pl.BlockSpec((rows, d), lambda b, h, t: (b * nb + blk(t), cb0 + j * heads + h))

    zspec = pl.BlockSpec((rows, d),
                         lambda b, h, t: (b * nb + blk(t), cb0 + jnp.where(t < nb, 2, 1) * heads + h))
    sub = min(HGRN_SUB, rows)
    assert sub & (sub - 1) == 0 and rows % sub == 0
    nlev = sub.bit_length() - 1
    idx = np.arange(sub)
    x = idx[:, None] ^ idx[None, :]
    lev = np.where(x > 0, np.floor(np.log2(np.maximum(x, 1))), nlev).astype(np.int32)
    lvl = jnp.asarray(np.stack([np.where(idx[:, None] >= idx[None, :], lev, -1),
                                np.where(idx[:, None] <= idx[None, :], lev, -1)]), jnp.int32)
    upper = np.stack([((idx >> l) & 1).astype(np.float32) for l in range(nlev)])
    up = jnp.broadcast_to(jnp.asarray(upper)[:, :, None], (nlev, sub, d))
    sg = jnp.broadcast_to(jnp.asarray((2.0 * upper - 1.0) * np.float32(math.log2(math.e)))[:, :, None],
                          (nlev, sub, d))
    tab = pl.BlockSpec((nlev, sub, d), lambda b, h, t: (0, 0, 0))
    r4 = idx & 3
    pat = np.array([[[0, 1, 0, 1], [1, 0, 0, 0], [0, 0, 1, 1], [0, 0, 0, 1]],
                    [[1, 0, 1, 0], [1, 0, 0, 0], [1, 1, 0, 0], [0, 0, 0, 1]]], np.float32)
    cf = jnp.broadcast_to(jnp.asarray(pat[:, :, r4] * np.float32(math.log2(math.e)))[:, :, :, None],
                          (2, 4, sub, d))
    steps = batch * heads * 2 * nb
    kdim = side_a.shape[1]
    sbn = _tile(side_cols, SIDE_BN)
    col_tiles = side_cols // sbn
    assert steps % col_tiles == 0 and m % (steps // col_tiles) == 0
    sbm = m // (steps // col_tiles)
    assert sbm % (rows // sub) == 0

    def step(b, h, t):
        return (b * heads + h) * (2 * nb) + t

    return pl.pallas_call(
        functools.partial(_hgrn_kernel, nb=nb, rows=rows, sub=sub, layer=layer, depth=depth),
        grid=(batch, heads, 2 * nb),
        in_specs=[pl.BlockSpec((2 * depth, d), lambda b, h, t: (0, h)),
                  pl.BlockSpec((2, sub, sub), lambda b, h, t: (0, 0, 0)), tab, tab,
                  pl.BlockSpec((2, 4, sub, d), lambda b, h, t: (0, 0, 0, 0)),
                  spec(0), zspec, spec(3),
                  pl.BlockSpec((rows, d), lambda b, h, t: (b * nb + fblk(t), cb0 + 4 * heads + h)),
                  pl.BlockSpec((1, d), lambda b, h, t: (0, h)),
                  pl.BlockSpec((sbm, kdim), lambda b, h, t: (step(b, h, t) // col_tiles, 0)),
                  pl.BlockSpec((None, kdim, sbn), lambda b, h, t: (layer, 0, step(b, h, t) % col_tiles))],
        out_specs=[pl.BlockSpec((rows, d), lambda b, h, t: (b * nb + fblk(t), h)),
                   pl.BlockSpec((sbm, sbn), lambda b, h, t: (step(b, h, t) // col_tiles, step(b, h, t) % col_tiles))],
        out_shape=[jax.ShapeDtypeStruct((m, width), BF16), jax.ShapeDtypeStruct((m, side_cols), F32)],
        scratch_shapes=[pltpu.VMEM((d, d), F32),
                        pltpu.VMEM((seq, d), F32),
                        pltpu.VMEM((kdim, sbn), BF16)],
        compiler_params=_cparams(("arbitrary", "arbitrary", "arbitrary")),
        name="hgrn2",
    )(lb_logits.reshape(2 * depth, width), lvl, up, sg, cf, proj, proj, proj, proj, norm_g.reshape(1, width),
      side_a, side_w)


def _dft_cos_sin(n):
    idx = np.arange(n)
    ang = 2.0 * np.pi * ((idx[:, None] * idx[None, :]) % n) / n
    return np.cos(ang), np.sin(ang)


def _fft_chan_kernel(z_ref, m_ref, vr_ref, vi_ref, *, groups):
    gd = FOURIER_GROUP_DIM
    mat = m_ref[...]
    for g in range(groups):
        pq = _dot3_rhs(z_ref[:, g * gd:(g + 1) * gd], mat)
        vr_ref[:, g * gd:(g + 1) * gd] = pq[:, :gd]
        vi_ref[:, g * gd:(g + 1) * gd] = pq[:, gd:]


def _fft_stage1_kernel(vr_ref, vi_ref, m_ref, twc_ref, tws_ref, ur_ref, ui_ref, *, n1, reps, nsub):
    for bi in range(nsub):
        x = jnp.concatenate([vr_ref[:, bi, :], vi_ref[:, bi, :]], axis=0)
        tt = _dot3_lhs(m_ref[...], x)
        tr, ti = tt[:n1], tt[n1:]
        cw = jnp.tile(twc_ref[bi], (1, reps))
        sw = jnp.tile(tws_ref[bi], (1, reps))
        ur_ref[:, bi, :] = tr * cw + ti * sw
        ui_ref[:, bi, :] = ti * cw - tr * sw


def _fft_stage2_kernel(ur_ref, ui_ref, m_ref, w_ref, b_ref, o_ref, y_ref, *, groups, scale, n2, nsub):
    gd = FOURIER_GROUP_DIM
    for ki in range(nsub):
        x = jnp.concatenate([ur_ref[ki * n2:(ki + 1) * n2, :], ui_ref[ki * n2:(ki + 1) * n2, :]], axis=0)
        spec = _dot3_lhs(m_ref[...], x) * scale
        for g in range(groups):
            y = _dot(spec[:, g * gd:(g + 1) * gd].astype(BF16), w_ref[g].astype(BF16))
            y_ref[:, ki, g * gd:(g + 1) * gd] = y + b_ref[:, g * gd:(g + 1) * gd]
    o_ref[...] = y_ref[...].astype(o_ref.dtype)


def fourier_mixer(proj, w, bias, batch, seq, col0):
    m = proj.shape[0]
    groups, gd, _ = w.shape
    width = groups * gd
    n1 = min(FFT_N1, seq)
    n2 = seq // n1
    br = min(512, m)

    cc, sc = _dft_cos_sin(gd)
    chan = _const3_rhs(np.concatenate([cc, -sc], axis=1))
    vr, vi = pl.pallas_call(
        functools.partial(_fft_chan_kernel, groups=groups),
        grid=(m // br,),
        in_specs=[pl.BlockSpec((br, width), lambda i: (i, col0 // width)),
                  pl.BlockSpec((3 * gd, 2 * gd), lambda i: (0, 0))],
        out_specs=[pl.BlockSpec((br, width), lambda i: (i, 0))] * 2,
        out_shape=[jax.ShapeDtypeStruct((m, width), F32)] * 2,
        compiler_params=_cparams(("parallel",)),
        name="fft_channels",
    )(proj, chan)

    c1, s1 = _dft_cos_sin(n1)
    m1 = _const3_lhs(np.block([[c1, s1], [-s1, c1]]))
    k1 = np.arange(n1)
    bb = np.arange(n2)
    tw = 2.0 * np.pi * ((bb[:, None] * k1[None, :]) % seq) / seq
    twc = jnp.broadcast_to(jnp.asarray(np.cos(tw), F32)[:, :, None], (n2, n1, LANES))
    tws = jnp.broadcast_to(jnp.asarray(np.sin(tw), F32)[:, :, None], (n2, n1, LANES))
    s1 = min(FFT_SUB1, n2)
    vspec = pl.BlockSpec((n1, s1, width), lambda bt, j: (bt, j, 0))
    tspec = pl.BlockSpec((s1, n1, LANES), lambda bt, j: (j, 0, 0))
    ur, ui = pl.pallas_call(
        functools.partial(_fft_stage1_kernel, n1=n1, reps=width // LANES, nsub=s1),
        grid=(batch, n2 // s1),
        in_specs=[vspec, vspec, pl.BlockSpec((2 * n1, 6 * n1), lambda bt, j: (0, 0)), tspec, tspec],
        out_specs=[vspec, vspec],
        out_shape=[jax.ShapeDtypeStruct((batch * n1, n2, width), F32)] * 2,
        compiler_params=_cparams(("parallel", "parallel")),
        name="fft_stage1",
    )(vr.reshape(batch * n1, n2, width), vi.reshape(batch * n1, n2, width), m1, twc, tws)

    c2, s2 = _dft_cos_sin(n2)
    m2 = _const3_lhs(np.concatenate([c2, s2], axis=1))
    s2 = min(FFT_SUB2, n1)
    uspec = pl.BlockSpec((s2 * n2, width), lambda bt, j: (bt * (n1 // s2) + j, 0))
    out = pl.pallas_call(
        functools.partial(_fft_stage2_kernel, groups=groups, scale=1.0 / math.sqrt(seq * gd), n2=n2, nsub=s2),
        grid=(batch, n1 // s2),
        in_specs=[uspec, uspec, pl.BlockSpec((n2, 6 * n2), lambda bt, j: (0, 0)),
                  pl.BlockSpec((groups, gd, gd), lambda bt, j: (0, 0, 0)),
                  pl.BlockSpec((1, width), lambda bt, j: (0, 0))],
        out_specs=pl.BlockSpec((n2, s2, width), lambda bt, j: (bt, j, 0)),
        out_shape=jax.ShapeDtypeStruct((batch * n2, n1, width), BF16),
        scratch_shapes=[pltpu.VMEM((n2, s2, width), F32)],
        compiler_params=_cparams(("parallel", "parallel")),
        name="fft_stage2",
    )(ur.reshape(m, width), ui.reshape(m, width), m2, w, bias.reshape(1, width))
    return out.reshape(m, width)


def _xattn_block_kernel(h_ref, y_ref, g_ref, wq_ref, wo_ref, k_ref, v_ref, ho_ref, f_ref, *, heads):
    hd = XATTN_HEAD_DIM
    h1 = h_ref[...] + _rms(y_ref[...], g_ref[0:1, :])
    c = _rms(h1, g_ref[1:2, :]).astype(BF16)
    q = (_dot(c, wq_ref[...]) * (hd ** -0.5)).astype(BF16)
    outs = []
    for hh in range(heads):
        cols = slice(hh * hd, (hh + 1) * hd)
        s = _dot_nt(q[:, cols], k_ref[:, cols])
        p = jnp.exp(s - jnp.max(s, axis=-1, keepdims=True))
        denom = jnp.sum(p, axis=-1, keepdims=True)
        outs.append((_dot(p.astype(BF16), v_ref[:, cols]) / denom).astype(BF16))
    xa = _dot(jnp.concatenate(outs, axis=1), wo_ref[...])
    h2 = h1 + _rms(xa, g_ref[2:3, :])
    ho_ref[...] = h2
    f_ref[...] = _rms(h2, g_ref[3:4, :]).astype(f_ref.dtype)


def xattn_block(h, y, gains, wq, wo, kx, vx, seq, mem_tokens):
    m, d = h.shape
    width = wq.shape[1]
    bm = min(XATTN_ROWS, seq)
    nb = seq // bm
    row = pl.BlockSpec((bm, d), lambda i: (i, 0))

    def const(shape):
        return pl.BlockSpec(shape, lambda i: (0, 0), pipeline_mode=pl.Buffered(1))

    kv = pl.BlockSpec((mem_tokens, width), lambda i: (i // nb, 0))
    return pl.pallas_call(
        functools.partial(_xattn_block_kernel, heads=width // XATTN_HEAD_DIM),
        grid=(m // bm,),
        in_specs=[row, row, const((4, d)), const((d, width)), const((width, d)), kv, kv],
        out_specs=[row, row],
        out_shape=[jax.ShapeDtypeStruct((m, d), F32), jax.ShapeDtypeStruct((m, d), BF16)],
        compiler_params=_cparams(("arbitrary",)),
        name="xattn_block",
    )(h, y, gains, wq, wo, kx, vx)


def kernel(x, mem, mem_norm_g, pre_mix_g, w_in, ret_norm_g, hgrn_lb_logits, hgrn_norm_g, fourier_w, fourier_b,
           w_out, post_mix_g, pre_xattn_g, xattn_wq, xattn_wk, xattn_wv, xattn_wo, post_xattn_g, pre_ffn_g,
           ffn_w_gate, ffn_w_up, ffn_w_down, post_ffn_g):
    batch, seq, d = x.shape
    depth = w_in.shape[0]
    mem_tokens = mem.shape[1]
    m = batch * seq
    ret_w = ret_norm_g.shape[1]
    hgrn_w = hgrn_norm_g.shape[1]

    h = x.reshape(m, d)
    mem_n = rmsnorm_rows(mem.reshape(batch * mem_tokens, d), mem_norm_g)
    a = rmsnorm_rows(h, pre_mix_g[0])
    for l in range(depth):
        proj_hf = matmul([a], w_in, l, F32, *MM_TILE, col0=4 * ret_w)
        y_hgrn, proj_r = hgrn2_mixer(proj_hf, hgrn_lb_logits, hgrn_norm_g[l], l, batch, seq, 0, a, w_in, 4 * ret_w)
        y_ret = retention_mixer(proj_r, ret_norm_g[l], batch, seq, 0)
        y_fft = fourier_mixer(proj_hf, fourier_w[l], fourier_b[l], batch, seq, 5 * hgrn_w)
        mixed = matmul([y_ret, y_hgrn, y_fft], w_out, l, F32, *MM_TILE)
        kx = matmul([mem_n], xattn_wk, l, BF16, *MM_TILE_MEM)
        vx = matmul([mem_n], xattn_wv, l, BF16, *MM_TILE_MEM)
        gains = jnp.stack([post_mix_g[l], pre_xattn_g[l], post_xattn_g[l], pre_ffn_g[l]])
        h, f = xattn_block(h, mixed, gains, xattn_wq[l].astype(BF16), xattn_wo[l].astype(BF16), kx, vx,
                           seq, mem_tokens)
        hid = swiglu_matmul(f, ffn_w_gate, ffn_w_up, l, *MM_TILE_SWIGLU)
        ff = matmul_ktiled(hid, ffn_w_down, l, *MM_TILE_DOWN)
        h, a = add_norm(h, ff, post_ffn_g[l], pre_mix_g[l + 1] if l + 1 < depth else None)
    return h.reshape(batch, seq, d)
```

```python
import functools
import math

import numpy as np
import jax
import jax.numpy as jnp
from jax import lax
from jax.experimental import pallas as pl
from jax.experimental.pallas import tpu as pltpu

F32 = jnp.float32
BF16 = jnp.bfloat16

RET_HEAD_DIM = 256
HGRN_HEAD_DIM = 128
FOURIER_GROUP_DIM = 128
XATTN_HEAD_DIM = 256
ROPE_BASE = 10000.0
RMS_EPS = 1e-6
GN_EPS = 1e-5

LANES = 128
MXU_TILE = 256
V7X_VMEM_BYTES = 64 * 1024 * 1024
VMEM_LIMIT = V7X_VMEM_BYTES - 8 * 1024 * 1024

MM_TILE = (2048, 512)
MM_TILE_SWIGLU = (2048, 256)
MM_COLS_MEM = 512
MM_TILE_DOWN = (2048, 1024, 1024)
RET_CHUNK = MXU_TILE
RET_ROWS = 2048
HGRN_ROWS = 1024
SIDE_BN = 512
HGRN_SUB = 128
XATTN_ROWS = 256
FFT_N1 = 64
FFT_SUB1 = 8
FFT_SUB2 = 16


def _cparams(sem):
    return pltpu.CompilerParams(dimension_semantics=sem, vmem_limit_bytes=VMEM_LIMIT)


def _tile(n, preferred):
    for t in range(min(preferred, n) // LANES * LANES, 0, -LANES):
        if n % t == 0:
            return t
    return n


def _dot(a, b):
    return jnp.dot(a, b, preferred_element_type=F32)


def _dot_nt(a, b):
    return lax.dot_general(a, b, (((1,), (1,)), ((), ())), preferred_element_type=F32)


def _dot_tn(a, b):
    return lax.dot_general(a, b, (((0,), (0,)), ((), ())), preferred_element_type=F32)


def _split2(x):
    hi = x.astype(BF16)
    return hi, (x - hi.astype(F32)).astype(BF16)


def _const3_lhs(mat):
    hi, lo = _split2(jnp.asarray(mat, F32))
    return jnp.concatenate([hi, hi, lo], axis=1)


def _const3_rhs(mat):
    hi, lo = _split2(jnp.asarray(mat, F32))
    return jnp.concatenate([hi, hi, lo], axis=0)


def _dot3_lhs(m3, x):
    hi, lo = _split2(x)
    return _dot(m3, jnp.concatenate([hi, lo, hi], axis=0))


def _dot3_rhs(x, m3):
    hi, lo = _split2(x)
    return _dot(jnp.concatenate([hi, lo, hi], axis=1), m3)


def _rms(x, g):
    ms = jnp.mean(x * x, axis=-1, keepdims=True)
    return x * lax.rsqrt(ms + RMS_EPS) * g


def _silu(x):
    return x * (1.0 / (1.0 + jnp.exp(-x)))


def _norm_kernel(x_ref, g_ref, o_ref):
    o_ref[...] = _rms(x_ref[...], g_ref[...]).astype(o_ref.dtype)


def rmsnorm_rows(x, g, br=256):
    m, d = x.shape
    out_dtype = BF16
    br = min(br, m)
    return pl.pallas_call(
        _norm_kernel,
        grid=(m // br,),
        in_specs=[pl.BlockSpec((br, d), lambda i: (i, 0)),
                  pl.BlockSpec((1, d), lambda i: (0, 0))],
        out_specs=pl.BlockSpec((br, d), lambda i: (i, 0)),
        out_shape=jax.ShapeDtypeStruct((m, d), out_dtype),
        compiler_params=_cparams(("parallel",)),
        name="rmsnorm_rows",
    )(x, g.reshape(1, d))


def _add_norm_kernel(h_ref, y_ref, gp_ref, gn_ref, ho_ref, a_ref):
    h = h_ref[...] + _rms(y_ref[...], gp_ref[...])
    ho_ref[...] = h
    a_ref[...] = _rms(h, gn_ref[...]).astype(a_ref.dtype)


def _add_norm_last_kernel(h_ref, y_ref, gp_ref, ho_ref):
    ho_ref[...] = h_ref[...] + _rms(y_ref[...], gp_ref[...])


def add_norm(h, y, g_post, g_next, br=256):
    m, d = h.shape
    row = pl.BlockSpec((br, d), lambda i: (i, 0))
    vec = pl.BlockSpec((1, d), lambda i: (0, 0))
    if g_next is None:
        return pl.pallas_call(
            _add_norm_last_kernel, grid=(m // br,),
            in_specs=[row, row, vec], out_specs=row,
            out_shape=jax.ShapeDtypeStruct((m, d), F32),
            compiler_params=_cparams(("parallel",)), name="add_norm_last",
        )(h, y, g_post.reshape(1, d)), None
    return pl.pallas_call(
        _add_norm_kernel, grid=(m // br,),
        in_specs=[row, row, vec, vec], out_specs=[row, row],
        out_shape=[jax.ShapeDtypeStruct((m, d), F32), jax.ShapeDtypeStruct((m, d), BF16)],
        compiler_params=_cparams(("parallel",)), name="add_norm",
    )(h, y, g_post.reshape(1, d), g_next.reshape(1, d))


def _row_resident_spec(bm, kdim):
    return pl.BlockSpec((bm, kdim), lambda i, j: (i, 0), pipeline_mode=pl.Buffered(1))


def _mm_kernel(*refs, k_parts):
    a_refs, w_ref, o_ref = refs[:len(k_parts)], refs[len(k_parts)], refs[-1]
    acc, r0 = None, 0
    for a_ref, kp in zip(a_refs, k_parts):
        part = _dot(a_ref[...], w_ref[r0:r0 + kp, :].astype(BF16))
        acc = part if acc is None else acc + part
        r0 += kp
    o_ref[...] = acc.astype(o_ref.dtype)


def matmul(a_parts, w, layer, out_dtype, bm, bn, col0=0):
    m = a_parts[0].shape[0]
    k_parts = tuple(a.shape[1] for a in a_parts)
    kdim = w.shape[1]
    n = w.shape[2] - col0
    assert sum(k_parts) == kdim
    bm, bn = _tile(m, bm), _tile(math.gcd(n, col0) if col0 else n, bn)
    cb0 = col0 // bn
    return pl.pallas_call(
        functools.partial(_mm_kernel, k_parts=k_parts),
        grid=(m // bm, n // bn),
        in_specs=[_row_resident_spec(bm, kp) for kp in k_parts]
        + [pl.BlockSpec((None, kdim, bn), lambda i, j: (layer, 0, cb0 + j))],
        out_specs=pl.BlockSpec((bm, bn), lambda i, j: (i, j)),
        out_shape=jax.ShapeDtypeStruct((m, n), out_dtype),
        compiler_params=_cparams(("parallel", "arbitrary")),
        name="matmul",
    )(*a_parts, w)


def _kv_kernel(a_ref, wk_ref, wv_ref, k_ref, v_ref):
    a = a_ref[...]
    k_ref[...] = _dot(a, wk_ref[...].astype(BF16)).astype(k_ref.dtype)
    v_ref[...] = _dot(a, wv_ref[...].astype(BF16)).astype(v_ref.dtype)


def memory_kv(mem_n, wk, wv, bn):
    m, kdim = mem_n.shape
    depth, _, n = wk.shape
    bn = _tile(n, bn)
    wspec = pl.BlockSpec((None, kdim, bn), lambda l, j: (l, 0, j))
    ospec = pl.BlockSpec((None, m, bn), lambda l, j: (l, 0, j))
    return pl.pallas_call(
        _kv_kernel,
        grid=(depth, n // bn),
        in_specs=[pl.BlockSpec((m, kdim), lambda l, j: (0, 0)), wspec, wspec],
        out_specs=[ospec, ospec],
        out_shape=[jax.ShapeDtypeStruct((depth, m, n), BF16)] * 2,
        compiler_params=_cparams(("parallel", "parallel")),
        name="memory_kv",
    )(mem_n, wk, wv)


def _swiglu_kernel(a_ref, wg_ref, wu_ref, o_ref):
    a = a_ref[...]
    gate = _dot(a, wg_ref[...].astype(BF16))
    up = _dot(a, wu_ref[...].astype(BF16))
    o_ref[...] = (_silu(gate) * up).astype(o_ref.dtype)


def swiglu_matmul(a, wg, wu, layer, bm, bn):
    m, kdim = a.shape
    n = wg.shape[2]
    bm, bn = _tile(m, bm), _tile(n, bn)
    wspec = pl.BlockSpec((None, kdim, bn), lambda i, j: (layer, 0, j))
    return pl.pallas_call(
        _swiglu_kernel,
        grid=(m // bm, n // bn),
        in_specs=[_row_resident_spec(bm, kdim), wspec, wspec],
        out_specs=pl.BlockSpec((bm, bn), lambda i, j: (i, j)),
        out_shape=jax.ShapeDtypeStruct((m, n), BF16),
        compiler_params=_cparams(("parallel", "arbitrary")),
        name="swiglu_matmul",
    )(a, wg, wu)


def _mm_ktiled_kernel(*refs, n_tail):
    a_ref, w_ref = refs[0], refs[1]
    t_refs, wt_refs, o_ref = refs[2:2 + n_tail], refs[2 + n_tail:2 + 2 * n_tail], refs[-1]
    k = pl.program_id(2)

    @pl.when(k == 0)
    def _():
        tail = jnp.concatenate([t[...] for t in t_refs], axis=1)
        w_tail = jnp.concatenate([r[...] for r in wt_refs], axis=0)
        o_ref[...] = _dot(a_ref[...], w_ref[...].astype(BF16)) + _dot(tail, w_tail.astype(BF16))

    @pl.when(k > 0)
    def _():
        o_ref[...] += _dot(a_ref[...], w_ref[...].astype(BF16))


def matmul_ktiled(a, w, layer, bm, bn, bk):
    m, kdim = a.shape
    n = w.shape[2]
    bm, bn = _tile(m, bm), _tile(n, bn)
    nk = kdim // bk
    kmain = nk * bk
    ktail = kdim - kmain
    assert nk >= 1 and ktail > 0 and ktail % MXU_TILE == 0 and kmain % MXU_TILE == 0
    n_tail = ktail // MXU_TILE
    tb0 = kmain // MXU_TILE
    a_tail_specs = [pl.BlockSpec((bm, MXU_TILE), functools.partial(lambda i, j, k, c: (i, c), c=tb0 + c))
                    for c in range(n_tail)]
    w_tail_specs = [pl.BlockSpec((None, MXU_TILE, bn), functools.partial(lambda i, j, k, c: (layer, c, j), c=tb0 + c))
                    for c in range(n_tail)]
    return pl.pallas_call(
        functools.partial(_mm_ktiled_kernel, n_tail=n_tail),
        grid=(m // bm, n // bn, nk),
        in_specs=[pl.BlockSpec((bm, bk), lambda i, j, k: (i, k)),
                  pl.BlockSpec((None, bk, bn), lambda i, j, k: (layer, k, j))]
        + a_tail_specs + w_tail_specs,
        out_specs=pl.BlockSpec((bm, bn), lambda i, j, k: (i, j)),
        out_shape=jax.ShapeDtypeStruct((m, n), F32),
        compiler_params=_cparams(("parallel", "parallel", "arbitrary")),
        name="matmul_ktiled",
    )(a, w, *([a] * n_tail), *([w] * n_tail))


def _ret_kernel(lg_ref, cos_ref, sin_ref, q_ref, k_ref, v_ref, g_ref, ng_ref, o_ref,
                sf_ref, sb_ref, ob_ref, qr_ref, kr_ref, vb_ref, dec_ref, vec_ref, *, nb, rows, c):
    t = pl.program_id(2)
    half = RET_HEAD_DIM // 2
    nch = rows // c

    @pl.when(t == 0)
    def _():
        lgf = lg_ref[0, 0:1, :]
        lgb = lg_ref[0, 1:2, :]
        i_row = lax.broadcasted_iota(jnp.int32, (c, half), 0).astype(F32)
        rel = (lax.broadcasted_iota(jnp.int32, (c, c), 0) - lax.broadcasted_iota(jnp.int32, (c, c), 1)).astype(F32)
        dec_ref[...] = jnp.exp(jnp.where(rel >= 0, rel * lgf[:, :1], -rel * lgb[:, :1]))
        vec_ref[0] = jnp.exp((i_row + 1.0) * lgf)
        vec_ref[1] = jnp.exp((c - 1.0 - i_row) * lgf)
        vec_ref[2] = jnp.exp((c - i_row) * lgb)
        vec_ref[3] = jnp.exp(i_row * lgb)
        vec_ref[4] = jnp.exp(c * lgf) + jnp.zeros((c, half), F32)
        vec_ref[5] = jnp.exp(c * lgb) + jnp.zeros((c, half), F32)
        sf_ref[...] = jnp.zeros_like(sf_ref)
        sb_ref[...] = jnp.zeros_like(sb_ref)

    def lanes2(d):
        return jnp.concatenate([d, d], axis=-1)

    @pl.when(t < nb)
    def _():
        base = (nb - 1 - t) * rows
        for ci in range(nch - 1, -1, -1):
            r0 = ci * c
            g0 = pl.multiple_of(base + r0, c)
            cos = cos_ref[r0:r0 + c, :]
            sin = sin_ref[r0:r0 + c, :]

            def rot(x):
                x1, x2 = x[:, :half], x[:, half:]
                return jnp.concatenate([x1 * cos - x2 * sin, x1 * sin + x2 * cos], axis=-1)

            q = rot(q_ref[r0:r0 + c, :]) * (RET_HEAD_DIM ** -0.5)
            k = rot(k_ref[r0:r0 + c, :])
            qr_ref[pl.ds(g0, c), :] = q
            kr_ref[pl.ds(g0, c), :] = k
            vb = v_ref[r0:r0 + c, :].astype(BF16)
            vb_ref[pl.ds(g0, c), :] = vb
            sb = sb_ref[...]
            ob_ref[pl.ds(g0, c), :] = _dot((q * lanes2(vec_ref[2])).astype(BF16), sb.astype(BF16))
            kd = k * lanes2(vec_ref[3])
            sb_ref[...] = vec_ref[5][0:1, 0:1] * sb + _dot_tn(kd.astype(BF16), vb)

    @pl.when(t >= nb)
    def _():
        base = (t - nb) * rows
        for ci in range(nch):
            r0 = ci * c
            g0 = pl.multiple_of(base + r0, c)
            q = qr_ref[pl.ds(g0, c), :]
            k = kr_ref[pl.ds(g0, c), :]
            vb = vb_ref[pl.ds(g0, c), :]
            sf = sf_ref[...]
            scores = _dot_nt(q.astype(BF16), k.astype(BF16)) * dec_ref[...]
            o = (_dot(scores.astype(BF16), vb) + _dot((q * lanes2(vec_ref[0])).astype(BF16), sf.astype(BF16))
                 + ob_ref[pl.ds(g0, c), :])
            kd = k * lanes2(vec_ref[1])
            sf_ref[...] = vec_ref[4][0:1, 0:1] * sf + _dot_tn(kd.astype(BF16), vb)
            mu = jnp.mean(o, axis=-1, keepdims=True)
            oc = o - mu
            var = jnp.mean(oc * oc, axis=-1, keepdims=True)
            y = oc * lax.rsqrt(var + GN_EPS) * ng_ref[...]
            o_ref[r0:r0 + c, :] = (_silu(g_ref[r0:r0 + c, :]) * y).astype(o_ref.dtype)


def retention_mixer(proj, norm_g, batch, seq, col0):
    m = proj.shape[0]
    width = norm_g.shape[0]
    heads = width // RET_HEAD_DIM
    c = min(RET_CHUNK, seq)
    rows = min(RET_ROWS, seq)
    nb = seq // rows
    half = RET_HEAD_DIM // 2
    cb0 = col0 // RET_HEAD_DIM

    hidx = jnp.arange(heads, dtype=F32)
    gamma = 1.0 - jnp.power(2.0, -5.0 - hidx)
    lg = jnp.log(gamma)
    lg2 = jnp.broadcast_to(jnp.stack([lg, lg[::-1]], axis=1)[:, :, None], (heads, 2, LANES))
    inv = jnp.power(ROPE_BASE, -jnp.arange(half, dtype=F32) / half)
    ang = jnp.arange(seq, dtype=F32)[:, None] * inv[None, :]
    cos, sin = jnp.cos(ang), jnp.sin(ang)

    def bblk(t):
        return jnp.maximum(nb - 1 - t, 0)

    def fblk(t):
        return jnp.maximum(t - nb, 0)

    def spec(j, which):
        return pl.BlockSpec((rows, RET_HEAD_DIM), lambda b, h, t: (b * nb + which(t), cb0 + j * heads + h))

    return pl.pallas_call(
        functools.partial(_ret_kernel, nb=nb, rows=rows, c=c),
        grid=(batch, heads, 2 * nb),
        in_specs=[pl.BlockSpec((1, 2, LANES), lambda b, h, t: (h, 0, 0)),
                  pl.BlockSpec((rows, half), lambda b, h, t: (bblk(t), 0)),
                  pl.BlockSpec((rows, half), lambda b, h, t: (bblk(t), 0)),
                  spec(0, bblk), spec(1, bblk), spec(2, bblk), spec(3, fblk),
                  pl.BlockSpec((1, RET_HEAD_DIM), lambda b, h, t: (0, h))],
        out_specs=pl.BlockSpec((rows, RET_HEAD_DIM), lambda b, h, t: (b * nb + fblk(t), h)),
        out_shape=jax.ShapeDtypeStruct((m, width), BF16),
        scratch_shapes=[pltpu.VMEM((RET_HEAD_DIM, RET_HEAD_DIM), F32),
                        pltpu.VMEM((RET_HEAD_DIM, RET_HEAD_DIM), F32),
                        pltpu.VMEM((seq, RET_HEAD_DIM), F32),
                        pltpu.VMEM((seq, RET_HEAD_DIM), F32),
                        pltpu.VMEM((seq, RET_HEAD_DIM), F32),
                        pltpu.VMEM((seq, RET_HEAD_DIM), BF16),
                        pltpu.VMEM((c, c), F32),
                        pltpu.VMEM((6, c, half), F32)],
        compiler_params=_cparams(("parallel", "parallel", "arbitrary")),
        name="retention",
    )(lg2, cos, sin, proj, proj, proj, proj, norm_g.reshape(1, width))


def _hgrn_kernel(lbl_ref, lvl_ref, up_ref, sg_ref, cf_ref, q_ref, z_ref, v_ref, g_ref, ng_ref, a_ref, w_ref,
                 o_ref, po_ref, st_ref, ob_ref, wb_ref, *, nb, rows, sub, layer, depth):
    t = pl.program_id(2)
    d = HGRN_HEAD_DIM
    nlev = sub.bit_length() - 1
    nsb = rows // sub
    srows = a_ref.shape[0] // nsb

    def lower_bound(dirn):
        x = [lbl_ref[2 * l + dirn: 2 * l + dirn + 1, :] for l in range(depth)]
        mx = functools.reduce(jnp.maximum, x)
        e = [jnp.exp(xi - mx) for xi in x]
        tot = functools.reduce(lambda a, b: a + b, e)
        p = [ei / tot for ei in e]
        return functools.reduce(lambda a, b: a + b, p[:layer + 1]) - p[0]

    r_i = lax.broadcasted_iota(jnp.int32, (sub, sub), 0)
    c_i = lax.broadcasted_iota(jnp.int32, (sub, sub), 1)

    def split3(x):
        hi = x.astype(BF16)
        r1 = x - hi.astype(F32)
        mid = r1.astype(BF16)
        lo = (r1 - mid.astype(F32)).astype(BF16)
        return jnp.concatenate([hi, mid, lo], axis=1)

    def group_rows(x, size, pick):
        parts = [jnp.broadcast_to(x[base + pick:base + pick + 1, :], (size, d)) for base in range(0, sub, size)]
        return parts[0] if len(parts) == 1 else jnp.concatenate(parts, axis=0)

    def head(r0, rev, lb):
        f = lb + (1.0 - lb) * (1.0 / (1.0 + jnp.exp(-z_ref[r0:r0 + sub, :])))
        kk = 1.0 - f
        lf = jnp.log(f)
        q = q_ref[r0:r0 + sub, :] * (d ** -0.5)
        vb = v_ref[r0:r0 + sub, :].astype(BF16)
        tri = ((c_i >= r_i) if rev else (c_i <= r_i)).astype(BF16)
        parts = _dot(tri, split3(lf))
        return kk, lf, q, vb, parts

    def side_slice(n):
        po_ref[n * srows:(n + 1) * srows, :] = _dot(a_ref[n * srows:(n + 1) * srows, :], wb_ref[...])

    def rest(r0, rev, hd, next_slice):
        kk, lf, q, vb, parts = hd
        cum = parts[:, :d] + parts[:, d:2 * d] + parts[:, 2 * d:]
        edge = cum[0:1, :] if rev else cum[sub - 1:sub, :]

        st = st_ref[...]
        o = _dot_nt((q * jnp.exp(cum)).astype(BF16), st.astype(BF16))
        kd = kk * jnp.exp(edge - cum)
        st_ref[...] = st * jnp.exp(edge) + _dot_tn(vb, kd.astype(BF16))

        lf_up = pltpu.roll(lf, sub - 1, 0)
        lf_dn = pltpu.roll(lf, 1, 0)
        qk = q - kk
        prods = []
        for lev in range(nlev):
            m = 1 << lev
            di = 1 if rev else 0
            if m == 1:
                arg = lf * cf_ref[di, 0]
            elif m == 2:
                arg = lf_up * cf_ref[di, 1] + lf * cf_ref[di, 2] + lf_dn * cf_ref[di, 3]
            else:
                bound = group_rows(cum, 2 * m, m if rev else m - 1)
                arg = ((bound - cum) if rev else (cum - bound)) * sg_ref[lev]
            up = up_ref[lev]
            y = ((q - up * qk) if rev else (kk + up * qk)) * jnp.exp2(arg)
            yb = y.astype(BF16)
            prods.append(_dot_nt(yb, yb))
        if next_slice is not None:
            side_slice(next_slice)
        lvl = lvl_ref[1 if rev else 0]
        a = jnp.where(lvl == nlev, jnp.sum(q * kk, axis=-1, keepdims=True), 0.0)
        for lev in range(nlev):
            a = jnp.where(lvl == lev, prods[lev], a)
        return o + _dot(a.astype(BF16), vb)

    @pl.when(t == 0)
    def _():
        st_ref[...] = jnp.zeros_like(st_ref)

    @pl.when(t < nb)
    def _():
        wb_ref[...] = w_ref[...].astype(BF16)
        lb = lower_bound(1)
        base = (nb - 1 - t) * rows
        order = list(range(nsb - 1, -1, -1))
        hd = head(order[0] * sub, True, lb)
        side_slice(0)
        for n, si in enumerate(order):
            nxt = head(order[n + 1] * sub, True, lb) if n + 1 < nsb else None
            ob_ref[pl.ds(pl.multiple_of(base + si * sub, sub), sub), :] = rest(
                si * sub, True, hd, n + 1 if n + 1 < nsb else None)
            hd = nxt

    @pl.when(t == nb)
    def _():
        st_ref[...] = jnp.zeros_like(st_ref)

    @pl.when(t >= nb)
    def _():
        wb_ref[...] = w_ref[...].astype(BF16)
        lb = lower_bound(0)
        base = (t - nb) * rows
        hd = head(0, False, lb)
        side_slice(0)
        for si in range(nsb):
            r0 = si * sub
            nxt = head(r0 + sub, False, lb) if si + 1 < nsb else None
            o = rest(r0, False, hd, si + 1 if si + 1 < nsb else None)
            o = o + ob_ref[pl.ds(pl.multiple_of(base + r0, sub), sub), :]
            hd = nxt
            o = o * lax.rsqrt(jnp.mean(o * o, axis=-1, keepdims=True) + RMS_EPS)
            o_ref[r0:r0 + sub, :] = (_silu(g_ref[r0:r0 + sub, :]) * (o * ng_ref[...])).astype(o_ref.dtype)


def hgrn2_mixer(proj, lb_logits, norm_g, layer, batch, seq, col0, side_a, side_w, side_cols):
    m = proj.shape[0]
    width = norm_g.shape[0]
    d = HGRN_HEAD_DIM
    heads = width // d
    depth = lb_logits.shape[0]
    rows = min(HGRN_ROWS, seq)
    nb = seq // rows
    cb0 = col0 // d

    def blk(t):
        return jnp.where(t < nb, nb - 1 - t, t - nb)

    def fblk(t):
        return jnp.maximum(t - nb, 0)

    def spec(j):
        return pl.BlockSpec((rows, d), lambda b, h, t: (b * nb + blk(t), cb0 + j * heads + h))

    zspec = pl.BlockSpec((rows, d),
                         lambda b, h, t: (b * nb + blk(t), cb0 + jnp.where(t < nb, 2, 1) * heads + h))
    sub = min(HGRN_SUB, rows)
    assert sub & (sub - 1) == 0 and rows % sub == 0
    nlev = sub.bit_length() - 1
    idx = np.arange(sub)
    x = idx[:, None] ^ idx[None, :]
    lev = np.where(x > 0, np.floor(np.log2(np.maximum(x, 1))), nlev).astype(np.int32)
    lvl = jnp.asarray(np.stack([np.where(idx[:, None] >= idx[None, :], lev, -1),
                                np.where(idx[:, None] <= idx[None, :], lev, -1)]), jnp.int32)
    upper = np.stack([((idx >> l) & 1).astype(np.float32) for l in range(nlev)])
    up = jnp.broadcast_to(jnp.asarray(upper)[:, :, None], (nlev, sub, d))
    sg = jnp.broadcast_to(jnp.asarray((2.0 * upper - 1.0) * np.float32(math.log2(math.e)))[:, :, None],
                          (nlev, sub, d))
    tab = pl.BlockSpec((nlev, sub, d), lambda b, h, t: (0, 0, 0))
    r4 = idx & 3
    pat = np.array([[[0, 1, 0, 1], [1, 0, 0, 0], [0, 0, 1, 1], [0, 0, 0, 1]],
                    [[1, 0, 1, 0], [1, 0, 0, 0], [1, 1, 0, 0], [0, 0, 0, 1]]], np.float32)
    cf = jnp.broadcast_to(jnp.asarray(pat[:, :, r4] * np.float32(math.log2(math.e)))[:, :, :, None],
                          (2, 4, sub, d))
    steps = batch * heads * 2 * nb
    kdim = side_a.shape[1]
    sbn = _tile(side_cols, SIDE_BN)
    col_tiles = side_cols // sbn
    assert steps % col_tiles == 0 and m % (steps // col_tiles) == 0
    sbm = m // (steps // col_tiles)
    assert sbm % (rows // sub) == 0

    def step(b, h, t):
        return (b * heads + h) * (2 * nb) + t

    return pl.pallas_call(
        functools.partial(_hgrn_kernel, nb=nb, rows=rows, sub=sub, layer=layer, depth=depth),
        grid=(batch, heads, 2 * nb),
        in_specs=[pl.BlockSpec((2 * depth, d), lambda b, h, t: (0, h)),
                  pl.BlockSpec((2, sub, sub), lambda b, h, t: (0, 0, 0)), tab, tab,
                  pl.BlockSpec((2, 4, sub, d), lambda b, h, t: (0, 0, 0, 0)),
                  spec(0), zspec, spec(3),
                  pl.BlockSpec((rows, d), lambda b, h, t: (b * nb + fblk(t), cb0 + 4 * heads + h)),
                  pl.BlockSpec((1, d), lambda b, h, t: (0, h)),
                  pl.BlockSpec((sbm, kdim), lambda b, h, t: (step(b, h, t) // col_tiles, 0)),
                  pl.BlockSpec((None, kdim, sbn), lambda b, h, t: (layer, 0, step(b, h, t) % col_tiles))],
        out_specs=[pl.BlockSpec((rows, d), lambda b, h, t: (b * nb + fblk(t), h)),
                   pl.BlockSpec((sbm, sbn), lambda b, h, t: (step(b, h, t) // col_tiles, step(b, h, t) % col_tiles))],
        out_shape=[jax.ShapeDtypeStruct((m, width), BF16), jax.ShapeDtypeStruct((m, side_cols), F32)],
        scratch_shapes=[pltpu.VMEM((d, d), F32),
                        pltpu.VMEM((seq, d), F32),
                        pltpu.VMEM((kdim, sbn), BF16)],
        compiler_params=_cparams(("arbitrary", "arbitrary", "arbitrary")),
        name="hgrn2",
    )(lb_logits.reshape(2 * depth, width), lvl, up, sg, cf, proj, proj, proj, proj, norm_g.reshape(1, width),
      side_a, side_w)


def _dft_cos_sin(n):
    idx = np.arange(n)
    ang = 2.0 * np.pi * ((idx[:, None] * idx[None, :]) % n) / n
    return np.cos(ang), np.sin(ang)


def _fft_chan_kernel(z_ref, m_ref, vr_ref, vi_ref, *, groups):
    gd = FOURIER_GROUP_DIM
    mat = m_ref[...]
    for g in range(groups):
        pq = _dot3_rhs(z_ref[:, g * gd:(g + 1) * gd], mat)
        vr_ref[:, g * gd:(g + 1) * gd] = pq[:, :gd]
        vi_ref[:, g * gd:(g + 1) * gd] = pq[:, gd:]


def _fft_stage1_kernel(vr_ref, vi_ref, m_ref, twc_ref, tws_ref, ur_ref, ui_ref, *, n1, reps, nsub):
    for bi in range(nsub):
        x = jnp.concatenate([vr_ref[:, bi, :], vi_ref[:, bi, :]], axis=0)
        tt = _dot3_lhs(m_ref[...], x)
        tr, ti = tt[:n1], tt[n1:]
        cw = jnp.tile(twc_ref[bi], (1, reps))
        sw = jnp.tile(tws_ref[bi], (1, reps))
        ur_ref[:, bi, :] = tr * cw + ti * sw
        ui_ref[:, bi, :] = ti * cw - tr * sw


def _fft_stage2_kernel(ur_ref, ui_ref, m_ref, w_ref, b_ref, o_ref, y_ref, *, groups, scale, n2, nsub):
    gd = FOURIER_GROUP_DIM
    for ki in range(nsub):
        x = jnp.concatenate([ur_ref[ki * n2:(ki + 1) * n2, :], ui_ref[ki * n2:(ki + 1) * n2, :]], axis=0)
        spec = _dot3_lhs(m_ref[...], x) * scale
        for g in range(groups):
            y = _dot(spec[:, g * gd:(g + 1) * gd].astype(BF16), w_ref[g].astype(BF16))
            y_ref[:, ki, g * gd:(g + 1) * gd] = y + b_ref[:, g * gd:(g + 1) * gd]
    o_ref[...] = y_ref[...].astype(o_ref.dtype)


def fourier_mixer(proj, w, bias, batch, seq, col0):
    m = proj.shape[0]
    groups, gd, _ = w.shape
    width = groups * gd
    n1 = min(FFT_N1, seq)
    n2 = seq // n1
    br = min(512, m)

    cc, sc = _dft_cos_sin(gd)
    chan = _const3_rhs(np.concatenate([cc, -sc], axis=1))
    vr, vi = pl.pallas_call(
        functools.partial(_fft_chan_kernel, groups=groups),
        grid=(m // br,),
        in_specs=[pl.BlockSpec((br, width), lambda i: (i, col0 // width)),
                  pl.BlockSpec((3 * gd, 2 * gd), lambda i: (0, 0))],
        out_specs=[pl.BlockSpec((br, width), lambda i: (i, 0))] * 2,
        out_shape=[jax.ShapeDtypeStruct((m, width), F32)] * 2,
        compiler_params=_cparams(("parallel",)),
        name="fft_channels",
    )(proj, chan)

    c1, s1 = _dft_cos_sin(n1)
    m1 = _const3_lhs(np.block([[c1, s1], [-s1, c1]]))
    k1 = np.arange(n1)
    bb = np.arange(n2)
    tw = 2.0 * np.pi * ((bb[:, None] * k1[None, :]) % seq) / seq
    twc = jnp.broadcast_to(jnp.asarray(np.cos(tw), F32)[:, :, None], (n2, n1, LANES))
    tws = jnp.broadcast_to(jnp.asarray(np.sin(tw), F32)[:, :, None], (n2, n1, LANES))
    s1 = min(FFT_SUB1, n2)
    vspec = pl.BlockSpec((n1, s1, width), lambda bt, j: (bt, j, 0))
    tspec = pl.BlockSpec((s1, n1, LANES), lambda bt, j: (j, 0, 0))
    ur, ui = pl.pallas_call(
        functools.partial(_fft_stage1_kernel, n1=n1, reps=width // LANES, nsub=s1),
        grid=(batch, n2 // s1),
        in_specs=[vspec, vspec, pl.BlockSpec((2 * n1, 6 * n1), lambda bt, j: (0, 0)), tspec, tspec],
        out_specs=[vspec, vspec],
        out_shape=[jax.ShapeDtypeStruct((batch * n1, n2, width), F32)] * 2,
        compiler_params=_cparams(("parallel", "parallel")),
        name="fft_stage1",
    )(vr.reshape(batch * n1, n2, width), vi.reshape(batch * n1, n2, width), m1, twc, tws)

    c2, s2 = _dft_cos_sin(n2)
    m2 = _const3_lhs(np.concatenate([c2, s2], axis=1))
    s2 = min(FFT_SUB2, n1)
    uspec = pl.BlockSpec((s2 * n2, width), lambda bt, j: (bt * (n1 // s2) + j, 0))
    out = pl.pallas_call(
        functools.partial(_fft_stage2_kernel, groups=groups, scale=1.0 / math.sqrt(seq * gd), n2=n2, nsub=s2),
        grid=(batch, n1 // s2),
        in_specs=[uspec, uspec, pl.BlockSpec((n2, 6 * n2), lambda bt, j: (0, 0)),
                  pl.BlockSpec((groups, gd, gd), lambda bt, j: (0, 0, 0)),
                  pl.BlockSpec((1, width), lambda bt, j: (0, 0))],
        out_specs=pl.BlockSpec((n2, s2, width), lambda bt, j: (bt, j, 0)),
        out_shape=jax.ShapeDtypeStruct((batch * n2, n1, width), BF16),
        scratch_shapes=[pltpu.VMEM((n2, s2, width), F32)],
        compiler_params=_cparams(("parallel", "parallel")),
        name="fft_stage2",
    )(ur.reshape(m, width), ui.reshape(m, width), m2, w, bias.reshape(1, width))
    return out.reshape(m, width)


def _xattn_block_kernel(h_ref, y_ref, g_ref, wq_ref, wo_ref, k_ref, v_ref, ho_ref, f_ref, *, heads):
    hd = XATTN_HEAD_DIM
    h1 = h_ref[...] + _rms(y_ref[...], g_ref[0:1, :])
    c = _rms(h1, g_ref[1:2, :]).astype(BF16)
    q = (_dot(c, wq_ref[...]) * (hd ** -0.5)).astype(BF16)
    outs = []
    for hh in range(heads):
        cols = slice(hh * hd, (hh + 1) * hd)
        s = _dot_nt(q[:, cols], k_ref[:, cols])
        p = jnp.exp(s - jnp.max(s, axis=-1, keepdims=True))
        denom = jnp.sum(p, axis=-1, keepdims=True)
        outs.append((_dot(p.astype(BF16), v_ref[:, cols]) / denom).astype(BF16))
    xa = _dot(jnp.concatenate(outs, axis=1), wo_ref[...])
    h2 = h1 + _rms(xa, g_ref[2:3, :])
    ho_ref[...] = h2
    f_ref[...] = _rms(h2, g_ref[3:4, :]).astype(f_ref.dtype)


def xattn_block(h, y, gains, wq, wo, kx, vx, seq, mem_tokens):
    m, d = h.shape
    width = wq.shape[1]
    bm = min(XATTN_ROWS, seq)
    nb = seq // bm
    row = pl.BlockSpec((bm, d), lambda i: (i, 0))

    def const(shape):
        return pl.BlockSpec(shape, lambda i: (0, 0), pipeline_mode=pl.Buffered(1))

    kv = pl.BlockSpec((mem_tokens, width), lambda i: (i // nb, 0))
    return pl.pallas_call(
        functools.partial(_xattn_block_kernel, heads=width // XATTN_HEAD_DIM),
        grid=(m // bm,),
        in_specs=[row, row, const((4, d)), const((d, width)), const((width, d)), kv, kv],
        out_specs=[row, row],
        out_shape=[jax.ShapeDtypeStruct((m, d), F32), jax.ShapeDtypeStruct((m, d), BF16)],
        compiler_params=_cparams(("arbitrary",)),
        name="xattn_block",
    )(h, y, gains, wq, wo, kx, vx)


def kernel(x, mem, mem_norm_g, pre_mix_g, w_in, ret_norm_g, hgrn_lb_logits, hgrn_norm_g, fourier_w, fourier_b,
           w_out, post_mix_g, pre_xattn_g, xattn_wq, xattn_wk, xattn_wv, xattn_wo, post_xattn_g, pre_ffn_g,
           ffn_w_gate, ffn_w_up, ffn_w_down, post_ffn_g):
    batch, seq, d = x.shape
    depth = w_in.shape[0]
    mem_tokens = mem.shape[1]
    m = batch * seq
    ret_w = ret_norm_g.shape[1]
    hgrn_w = hgrn_norm_g.shape[1]

    h = x.reshape(m, d)
    mem_n = rmsnorm_rows(mem.reshape(batch * mem_tokens, d), mem_norm_g)
    kx_all, vx_all = memory_kv(mem_n, xattn_wk, xattn_wv, MM_COLS_MEM)
    a = rmsnorm_rows(h, pre_mix_g[0])
    for l in range(depth):
        proj_hf = matmul([a], w_in, l, F32, *MM_TILE, col0=4 * ret_w)
        y_hgrn, proj_r = hgrn2_mixer(proj_hf, hgrn_lb_logits, hgrn_norm_g[l], l, batch, seq, 0, a, w_in, 4 * ret_w)
        y_ret = retention_mixer(proj_r, ret_norm_g[l], batch, seq, 0)
        y_fft = fourier_mixer(proj_hf, fourier_w[l], fourier_b[l], batch, seq, 5 * hgrn_w)
        mixed = matmul([y_ret, y_hgrn, y_fft], w_out, l, F32, *MM_TILE)
        gains = jnp.stack([post_mix_g[l], pre_xattn_g[l], post_xattn_g[l], pre_ffn_g[l]])
        h, f = xattn_block(h, mixed, gains, xattn_wq[l].astype(BF16), xattn_wo[l].astype(BF16),
                           kx_all[l], vx_all[l], seq, mem_tokens)
        hid = swiglu_matmul(f, ffn_w_gate, ffn_w_up, l, *MM_TILE_SWIGLU)
        ff = matmul_ktiled(hid, ffn_w_down, l, *MM_TILE_DOWN)
        h, a = add_norm(h, ff, post_ffn_g[l], pre_mix_g[l + 1] if l + 1 < depth else None)
    return h.reshape(batch, seq, d)
```
